```python
import math, functools
import jax, jax.numpy as jnp
from jax import lax
import numpy as np

D_MODEL = 2048
BATCH = 2
SEQ = 4096
DEPTH = 1
DEC_BATCH = 128
DEC_SEQ = 1
PAST_LEN = 16384
PAGE_SIZE = 128

V_HEAD = 128
MLA_HEADS = D_MODEL // V_HEAD
Q_LORA = 512
KV_LORA = 512
QK_NOPE = 128
QK_ROPE = 64
ROPE_THETA = 10000.0
Q_BLOCK = 128
MLA_SCALE = 1.0 / math.sqrt(QK_NOPE + QK_ROPE)
GLA_HEADS = 4
GLA_DK = D_MODEL // 2 // GLA_HEADS
GLA_DV = D_MODEL // GLA_HEADS
GLA_LOWRANK = 16
GLA_TAU = 16.0
GLA_CHUNK = 64
N_EXPERTS = 32
TOP_K = 4
D_FF = D_MODEL
SWIGLU_LIMIT = 7.0
SWIGLU_ALPHA = 1.702
EXPERT_BLOCK = 128
PLE_DIM = 256
EPS = 1e-6
IN_SPLITS = (Q_LORA, KV_LORA, QK_ROPE, GLA_HEADS * GLA_DK, GLA_HEADS * GLA_DK, GLA_HEADS * GLA_DV, GLA_HEADS * GLA_DV, GLA_LOWRANK, D_MODEL, D_MODEL)
D_IN = sum(IN_SPLITS)
IN_OFFSETS = tuple(int(o) for o in np.cumsum(IN_SPLITS)[:-1])

kernel_name = 'hybrid_mla_gla_moe_ple_step'


def rmsnorm(x, g):
    xf = x.astype(jnp.float32)
    xf = xf * lax.rsqrt(jnp.mean(xf * xf, axis=-1, keepdims=True) + EPS)
    return (xf * g.astype(jnp.float32)).astype(x.dtype)


def apply_rope(x, pos):
    half = QK_ROPE // 2
    inv_freq = jnp.power(ROPE_THETA, -jnp.arange(half, dtype=jnp.float32) / half)
    ang = pos.astype(jnp.float32)[:, None] * inv_freq[None, :]
    cos = jnp.cos(ang)[None, :, None, :]
    sin = jnp.sin(ang)[None, :, None, :]
    xf = x.astype(jnp.float32)
    x1, x2 = xf[..., :half], xf[..., half:]
    return jnp.concatenate([x1 * cos - x2 * sin, x2 * cos + x1 * sin], axis=-1).astype(x.dtype)


def mla_attend_prompt(q_lat, q_rope, c_kv, k_rope):
    b, s, h, _ = q_lat.shape
    nb = s // Q_BLOCK
    ql = q_lat.reshape(b, nb, Q_BLOCK, h, KV_LORA).swapaxes(0, 1)
    qr = q_rope.reshape(b, nb, Q_BLOCK, h, QK_ROPE).swapaxes(0, 1)
    k_pos = jnp.arange(s)

    def one_block(args):
        ql_b, qr_b, blk = args
        q_pos = blk * Q_BLOCK + jnp.arange(Q_BLOCK)
        sc = (jnp.einsum('bqhc,bkc->bhqk', ql_b, c_kv, preferred_element_type=jnp.float32)
              + jnp.einsum('bqhr,bkr->bhqk', qr_b, k_rope, preferred_element_type=jnp.float32)) * MLA_SCALE
        sc = jnp.where(k_pos[None, :] <= q_pos[:, None], sc, -jnp.inf)
        pr = jax.nn.softmax(sc, axis=-1).astype(c_kv.dtype)
        return jnp.einsum('bhqk,bkc->bqhc', pr, c_kv)

    o = lax.map(one_block, (ql, qr, jnp.arange(nb)))
    return o.swapaxes(0, 1).reshape(b, s, h, KV_LORA)


def mla_attend_paged(q_lat, q_rope, c_new, kr_new, cache_c, cache_kr, page_table):
    t = q_lat.shape[1]
    past = page_table.shape[1] * PAGE_SIZE
    mask = jnp.concatenate([jnp.ones((t, past), bool), jnp.tril(jnp.ones((t, t), bool))], axis=1)

    def one_seq(args):
        ql, qr, cn, krn, pages = args
        c_all = jnp.concatenate([cache_c[pages].reshape(past, KV_LORA), cn.astype(cache_c.dtype)], axis=0)
        kr_all = jnp.concatenate([cache_kr[pages].reshape(past, QK_ROPE), krn.astype(cache_kr.dtype)], axis=0)
        sc = (jnp.einsum('thc,kc->htk', ql, c_all, preferred_element_type=jnp.float32)
              + jnp.einsum('thr,kr->htk', qr, kr_all, preferred_element_type=jnp.float32)) * MLA_SCALE
        sc = jnp.where(mask[None], sc, -jnp.inf)
        pr = jax.nn.softmax(sc, axis=-1).astype(c_all.dtype)
        return jnp.einsum('htk,kc->thc', pr, c_all).astype(ql.dtype)

    return lax.map(one_seq, (q_lat, q_rope, c_new, kr_new, page_table))


def gla_chunked(q, k, v, log_a, s0):
    b, t, h, dk = q.shape
    dv = v.shape[-1]
    c = math.gcd(t, GLA_CHUNK)
    nc = t // c

    def to_chunks(z):
        return z.astype(jnp.float32).reshape(b, nc, c, h, z.shape[-1]).transpose(1, 0, 3, 2, 4)

    tri = jnp.tril(jnp.ones((c, c), bool))[:, :, None]

    def step(state, inp):
        qc, kc, vc, ac = inp
        bc = jnp.cumsum(ac, axis=2)
        o_inter = jnp.einsum('bhid,bhdv->bhiv', qc * jnp.exp(bc), state)
        decay = jnp.exp(jnp.where(tri, bc[:, :, :, None, :] - bc[:, :, None, :, :], -jnp.inf))
        att = jnp.einsum('bhid,bhjd,bhijd->bhij', qc, kc, decay)
        o = o_inter + jnp.einsum('bhij,bhjv->bhiv', att, vc)
        b_last = bc[:, :, -1:, :]
        state = (jnp.exp(b_last[:, :, 0, :])[..., None] * state
                 + jnp.einsum('bhjd,bhjv->bhdv', kc * jnp.exp(b_last - bc), vc))
        return state, o

    s_fin, o = lax.scan(step, s0.astype(jnp.float32), (to_chunks(q), to_chunks(k), to_chunks(v), to_chunks(log_a)))
    return o.transpose(1, 0, 3, 2, 4).reshape(b, t, h, dv), s_fin


def moe(x, w_router, b_router, w_gu, b_gu, w_down, b_down):
    n_tok, d = x.shape
    logits = jnp.dot(x, w_router, preferred_element_type=jnp.float32) + b_router.astype(jnp.float32)
    top_v, top_i = lax.top_k(logits, TOP_K)
    gates = jax.nn.softmax(top_v, axis=-1)
    n_asg = n_tok * TOP_K
    e_flat = top_i.reshape(n_asg)
    tok_flat = (jnp.arange(n_asg) // TOP_K).astype(jnp.int32)
    onehot = jax.nn.one_hot(e_flat, N_EXPERTS, dtype=jnp.int32)
    rank = jnp.take_along_axis(jnp.cumsum(onehot, axis=0), e_flat[:, None], axis=1)[:, 0] - 1
    counts = onehot.sum(axis=0)
    padded = (counts + EXPERT_BLOCK - 1) // EXPERT_BLOCK * EXPERT_BLOCK
    ends = jnp.cumsum(padded)
    dest = ends[e_flat] - padded[e_flat] + rank
    n_blk = -(-n_asg // EXPERT_BLOCK) + N_EXPERTS
    rows = n_blk * EXPERT_BLOCK
    buf_tok = jnp.full((rows,), n_tok, jnp.int32).at[dest].set(tok_flat)
    buf_gate = jnp.zeros((rows,), jnp.float32).at[dest].set(gates.reshape(n_asg))
    blk_expert = jnp.minimum(jnp.searchsorted(ends, jnp.arange(n_blk) * EXPERT_BLOCK, side='right'), N_EXPERTS - 1)
    x_pad = jnp.concatenate([x, jnp.zeros((1, d), x.dtype)], axis=0)
    xb = x_pad[buf_tok].reshape(n_blk, EXPERT_BLOCK, d)

    def expert_block(args):
        xblk, e = args
        gu = xblk @ w_gu[e] + b_gu[e]
        gate = jnp.minimum(gu[:, :D_FF], SWIGLU_LIMIT)
        up = jnp.clip(gu[:, D_FF:], -SWIGLU_LIMIT, SWIGLU_LIMIT)
        glu = gate * jax.nn.sigmoid(gate * SWIGLU_ALPHA)
        return ((up + 1.0) * glu) @ w_down[e] + b_down[e]

    yb = lax.map(expert_block, (xb, blk_expert)).reshape(rows, d)
    y = jnp.zeros((n_tok + 1, d), jnp.float32).at[buf_tok].add(yb.astype(jnp.float32) * buf_gate[:, None])
    return y[:n_tok].astype(x.dtype)


def decoder_block(h, p_l, pos, gla_s0, mla_attend, g_mix, w_in, g_q_a, w_uq, g_kv_a, w_uk, w_uv, w_a2, b_a2, g_gla, w_out, g_ffn, w_router, b_router, w_gu, b_gu, w_down, b_down, g_ple, w_ple, w_ple_gate):
    b, t, d = h.shape
    a = rmsnorm(h, g_mix)
    z = a @ w_in
    c_q, c_kv, k_r, gq, gk, gv, g_out, g_lr, gate_mla, gate_gla = jnp.split(z, IN_OFFSETS, axis=-1)
    q = (rmsnorm(c_q, g_q_a) @ w_uq).reshape(b, t, MLA_HEADS, QK_NOPE + QK_ROPE)
    q_nope = q[..., :QK_NOPE]
    q_rope = apply_rope(q[..., QK_NOPE:], pos)
    c_kv = rmsnorm(c_kv, g_kv_a)
    k_r = apply_rope(k_r[:, :, None, :], pos)[:, :, 0, :]
    q_lat = jnp.einsum('bthn,chn->bthc', q_nope, w_uk)
    o_lat = mla_attend(q_lat, q_rope, c_kv, k_r)
    o_mla = jnp.einsum('bthc,chv->bthv', o_lat, w_uv).reshape(b, t, d)
    gq = gq.reshape(b, t, GLA_HEADS, GLA_DK) * (GLA_DK ** -0.5)
    gk = gk.reshape(b, t, GLA_HEADS, GLA_DK)
    gv = gv.reshape(b, t, GLA_HEADS, GLA_DV)
    log_a = (jax.nn.log_sigmoid((g_lr @ w_a2 + b_a2).astype(jnp.float32)) / GLA_TAU).reshape(b, t, GLA_HEADS, GLA_DK)
    o_g, s_new = gla_chunked(gq, gk, gv, log_a, gla_s0)
    o_gla = rmsnorm(o_g.astype(h.dtype), g_gla).reshape(b, t, d) * jax.nn.silu(g_out)
    mix = jax.nn.sigmoid(gate_mla) * o_mla + jax.nn.sigmoid(gate_gla) * o_gla
    h = h + mix @ w_out
    m = rmsnorm(h, g_ffn)
    h = h + moe(m.reshape(b * t, d), w_router, b_router, w_gu, b_gu, w_down, b_down).reshape(b, t, d)
    u = rmsnorm(h, g_ple)
    h = h + (p_l.astype(h.dtype) @ w_ple) * jax.nn.sigmoid(u @ w_ple_gate)
    return h, c_kv, k_r, s_new


def setup_inputs(seed: int = 0) -> dict:
    key = jax.random.key(seed)
    ks = jax.random.split(key, 32)
    f32 = jnp.float32
    n_pages = PAST_LEN // PAGE_SIZE
    n_used = DEC_BATCH * n_pages
    n_phys = n_used + max(1, n_used // 4)

    def nrm(k, shape, scale):
        return jax.random.normal(k, shape, f32) * scale

    def gain(k, n):
        return 1.0 + 0.05 * jax.random.normal(k, (DEPTH, n), f32)

    page_table = jax.random.permutation(ks[0], n_phys)[:n_used].reshape(DEC_BATCH, n_pages).astype(jnp.int32)
    return {
        'x_prompt': nrm(ks[1], (BATCH, SEQ, D_MODEL), 1.0),
        'x_sample': nrm(ks[2], (DEC_BATCH, DEC_SEQ, D_MODEL), 1.0),
        'cache_kv_latent': nrm(ks[3], (DEPTH, n_phys, PAGE_SIZE, KV_LORA), 1.0),
        'cache_k_rope': nrm(ks[4], (DEPTH, n_phys, PAGE_SIZE, QK_ROPE), 1.0),
        'state_gla': nrm(ks[5], (DEPTH, DEC_BATCH, GLA_HEADS, GLA_DK, GLA_DV), 0.5),
        'page_table': page_table,
        'p_prompt': nrm(ks[6], (DEPTH, BATCH, SEQ, PLE_DIM), 1.0),
        'p_sample': nrm(ks[7], (DEPTH, DEC_BATCH, DEC_SEQ, PLE_DIM), 1.0),
        'g_mix': gain(ks[8], D_MODEL),
        'w_in': nrm(ks[9], (DEPTH, D_MODEL, D_IN), D_MODEL ** -0.5),
        'g_q_a': gain(ks[10], Q_LORA),
        'w_uq': nrm(ks[11], (DEPTH, Q_LORA, MLA_HEADS * (QK_NOPE + QK_ROPE)), Q_LORA ** -0.5),
        'g_kv_a': gain(ks[12], KV_LORA),
        'w_uk': nrm(ks[13], (DEPTH, KV_LORA, MLA_HEADS, QK_NOPE), KV_LORA ** -0.5),
        'w_uv': nrm(ks[14], (DEPTH, KV_LORA, MLA_HEADS, V_HEAD), KV_LORA ** -0.5),
        'w_a2': nrm(ks[15], (DEPTH, GLA_LOWRANK, GLA_HEADS * GLA_DK), GLA_LOWRANK ** -0.5),
        'b_a2': nrm(ks[16], (DEPTH, GLA_HEADS * GLA_DK), 0.1),
        'g_gla': gain(ks[17], GLA_DV),
        'w_out': nrm(ks[18], (DEPTH, D_MODEL, D_MODEL), D_MODEL ** -0.5),
        'g_ffn': gain(ks[19], D_MODEL),
        'w_router': nrm(ks[20], (DEPTH, D_MODEL, N_EXPERTS), D_MODEL ** -0.5),
        'b_router': nrm(ks[21], (DEPTH, N_EXPERTS), 0.01),
        'w_gu': nrm(ks[22], (DEPTH, N_EXPERTS, D_MODEL, 2 * D_FF), D_MODEL ** -0.5),
        'b_gu': nrm(ks[23], (DEPTH, N_EXPERTS, 2 * D_FF), 0.01),
        'w_down': nrm(ks[24], (DEPTH, N_EXPERTS, D_FF, D_MODEL), D_FF ** -0.5),
        'b_down': nrm(ks[25], (DEPTH, N_EXPERTS, D_MODEL), 0.01),
        'g_ple': gain(ks[26], D_MODEL),
        'w_ple': nrm(ks[27], (DEPTH, PLE_DIM, D_MODEL), PLE_DIM ** -0.5),
        'w_ple_gate': nrm(ks[28], (DEPTH, D_MODEL, D_MODEL), D_MODEL ** -0.5),
        'g_final': 1.0 + 0.05 * jax.random.normal(ks[29], (D_MODEL,), f32),
    }


def reference(x_prompt, x_sample, cache_kv_latent, cache_k_rope, state_gla, page_table, p_prompt, p_sample, g_mix, w_in, g_q_a, w_uq, g_kv_a, w_uk, w_uv, w_a2, b_a2, g_gla, w_out, g_ffn, w_router, b_router, w_gu, b_gu, w_down, b_down, g_ple, w_ple, w_ple_gate, g_final):
    layer_params = (g_mix, w_in, g_q_a, w_uq, g_kv_a, w_uk, w_uv, w_a2, b_a2, g_gla, w_out, g_ffn, w_router, b_router, w_gu, b_gu, w_down, b_down, g_ple, w_ple, w_ple_gate)
    pos_p = jnp.arange(x_prompt.shape[1])
    pos_s = PAST_LEN + jnp.arange(x_sample.shape[1])
    h_p, h_s = x_prompt, x_sample
    kv_p, kr_p, st_p, kv_s, kr_s, st_s = [], [], [], [], [], []
    for l in range(DEPTH):
        wl = [w[l] for w in layer_params]
        s0_p = jnp.zeros((x_prompt.shape[0], GLA_HEADS, GLA_DK, GLA_DV), jnp.float32)
        h_p, c_new, r_new, s_new = decoder_block(h_p, p_prompt[l], pos_p, s0_p, mla_attend_prompt, *wl)
        kv_p.append(c_new); kr_p.append(r_new); st_p.append(s_new)
        attend_s = functools.partial(mla_attend_paged, cache_c=cache_kv_latent[l], cache_kr=cache_k_rope[l], page_table=page_table)
        h_s, c_new, r_new, s_new = decoder_block(h_s, p_sample[l], pos_s, state_gla[l], attend_s, *wl)
        kv_s.append(c_new); kr_s.append(r_new); st_s.append(s_new)
    y_prompt = rmsnorm(h_p, g_final)
    y_sample = rmsnorm(h_s, g_final)
    return (y_prompt, y_sample, jnp.stack(kv_p), jnp.stack(kr_p), jnp.stack(st_p), jnp.stack(kv_s), jnp.stack(kr_s), jnp.stack(st_s))
```

```python
import functools
import math

import jax
import jax.numpy as jnp
from jax import lax
from jax.experimental import pallas as pl
from jax.experimental.pallas import tpu as pltpu

F32 = jnp.float32
BF16 = jnp.bfloat16
U32 = jnp.uint32
I32 = jnp.int32

EPS = 1e-6
ROPE_THETA = 10000.0
GLA_TAU = 16.0
GLA_CHUNK = 64
TOP_K = 4
SWIGLU_LIMIT = 7.0
SWIGLU_ALPHA = 1.702
EXPERT_ROW_BLOCK = 128
LANES = 128
MIB = 1024 * 1024
HI = lax.Precision.HIGHEST
NEG_BIG = -1e30

_NT = (((1,), (1,)), ((), ()))
_TN = (((0,), (0,)), ((), ()))


def _cparams(sem, vmem_mib=48):
    return pltpu.CompilerParams(dimension_semantics=sem, vmem_limit_bytes=vmem_mib * MIB)


def _tile(n, cap, mult):
    best = None
    for t in range(mult, min(n, cap) + 1, mult):
        if n % t == 0:
            best = t
    assert best is not None, (n, cap, mult)
    return best


def _rms(x, g):
    return x * lax.rsqrt(jnp.mean(x * x, axis=-1, keepdims=True) + EPS) * g


def _log_sigmoid(x):
    return jnp.minimum(x, 0.0) - jnp.log1p(jnp.exp(-jnp.abs(x)))


def _pack_bf16_pair(hi, lo):
    hb = pltpu.bitcast(hi.astype(BF16).astype(F32), U32)
    lb = pltpu.bitcast(lo.astype(BF16).astype(F32), U32)
    return hb | (lb >> 16)


def _unpack_bf16_pair(w):
    hi = pltpu.bitcast(w & jnp.uint32(0xFFFF0000), F32)
    lo = pltpu.bitcast(w << 16, F32)
    return hi, lo


def _rms_cast_kernel(x_ref, g_ref, o_ref):
    o_ref[...] = _rms(x_ref[...], g_ref[...]).astype(o_ref.dtype)


def _rms_cast(x, g):
    n, d = x.shape
    tm = _tile(n, 512, 16)
    return pl.pallas_call(
        _rms_cast_kernel,
        grid=(n // tm,),
        in_specs=[pl.BlockSpec((tm, d), lambda i: (i, 0)), pl.BlockSpec((1, d), lambda i: (0, 0))],
        out_specs=pl.BlockSpec((tm, d), lambda i: (i, 0)),
        out_shape=jax.ShapeDtypeStruct((n, d), BF16),
        compiler_params=_cparams(("parallel",)),
        name="rms_cast",
    )(x, g.reshape(1, d))


def _mm_kernel(x_ref, w_ref, o_ref):
    o_ref[...] = jnp.dot(x_ref[...], w_ref[...], preferred_element_type=F32).astype(o_ref.dtype)


def _matmul(x, w, out_dtype, name):
    m, k = x.shape
    n = w.shape[1]
    tm = _tile(m, 1024, 16)
    tn = _tile(n, 1280, LANES)
    return pl.pallas_call(
        _mm_kernel,
        grid=(n // tn, m // tm),
        in_specs=[pl.BlockSpec((tm, k), lambda j, i: (i, 0)), pl.BlockSpec((k, tn), lambda j, i: (0, j))],
        out_specs=pl.BlockSpec((tm, tn), lambda j, i: (i, j)),
        out_shape=jax.ShapeDtypeStruct((m, n), out_dtype),
        compiler_params=_cparams(("parallel", "parallel")),
        name=name,
    )(x, w)


def _rope_combine(y2, tab):
    half = y2.shape[1] // 2
    t = y2 * tab
    r = t + pltpu.roll(t, half, axis=1)
    lane = lax.broadcasted_iota(I32, r.shape, 1)
    return jnp.where(lane < half, r, 0.0)


def _q_proj_kernel(cq_ref, g_ref, w_ref, tab_ref, q_ref, *, heads, nope, scale):
    cqn = _rms(cq_ref[...], g_ref[...]).astype(BF16)
    tab = tab_ref[...]
    for h in range(heads):
        y = jnp.dot(cqn, w_ref[h], preferred_element_type=F32)
        q_ref[h, :, :nope] = (y[:, :nope] * scale).astype(BF16)
        q_ref[h, :, nope:] = (_rope_combine(y[:, nope:], tab) * scale).astype(BF16)


def _q_proj(zs, g_q_a, w_uq_p, tab, scale):
    n = zs.shape[0]
    heads, ql, width = w_uq_p.shape
    tm = _tile(n, 512, 16)
    return pl.pallas_call(
        functools.partial(_q_proj_kernel, heads=heads, nope=width // 2, scale=scale),
        grid=(n // tm,),
        in_specs=[
            pl.BlockSpec((tm, ql), lambda i: (i, 0)),
            pl.BlockSpec((1, ql), lambda i: (0, 0)),
            pl.BlockSpec((heads, ql, width), lambda i: (0, 0, 0)),
            pl.BlockSpec((tm, LANES), lambda i: (i, 0)),
        ],
        out_specs=pl.BlockSpec((heads, tm, width), lambda i: (0, i, 0)),
        out_shape=jax.ShapeDtypeStruct((heads, n, width), BF16),
        compiler_params=_cparams(("parallel",)),
        name="q_proj",
    )(zs, g_q_a.reshape(1, ql), w_uq_p, tab)


def _kv_prep_kernel(ckv_ref, krs_ref, g_ref, tab_ref, wuk_ref, wuvt_ref, c_ref, kr_ref, k_ref, vt_ref, *, heads, nope, rope):
    c = _rms(ckv_ref[...], g_ref[...])
    c_ref[...] = c
    cb = c.astype(BF16)
    r = _rope_combine(krs_ref[...], tab_ref[...])
    kr_ref[...] = r[:, :rope]
    rb = r.astype(BF16)
    for h in range(heads):
        k_ref[h, :, :nope] = jnp.dot(cb, wuk_ref[h], preferred_element_type=F32).astype(BF16)
        k_ref[h, :, nope:] = rb
        vt_ref[h] = lax.dot_general(wuvt_ref[h], cb, _NT, preferred_element_type=F32).astype(BF16)


def _latent_kernel(ckv_ref, krs_ref, g_ref, tab_ref, c_ref, kr_ref, *, rope):
    c_ref[...] = _rms(ckv_ref[...], g_ref[...])
    kr_ref[...] = _rope_combine(krs_ref[...], tab_ref[...])[:, :rope]


def _kv_prep(zs, g_kv_a, tab, w_uk_h, w_uv_t, ql, rope, with_kv):
    n = zs.shape[0]
    heads, kvl, nope = w_uk_h.shape
    vh = w_uv_t.shape[1]
    assert ql % kvl == 0 and (ql + kvl) % LANES == 0
    tm = _tile(n, 640, LANES if with_kv else 8)
    in_specs = [
        pl.BlockSpec((tm, kvl), lambda i: (i, ql // kvl)),
        pl.BlockSpec((tm, LANES), lambda i: (i, (ql + kvl) // LANES)),
        pl.BlockSpec((1, kvl), lambda i: (0, 0)),
        pl.BlockSpec((tm, LANES), lambda i: (i, 0)),
    ]
    out_specs = [pl.BlockSpec((tm, kvl), lambda i: (i, 0)), pl.BlockSpec((tm, rope), lambda i: (i, 0))]
    out_shape = [jax.ShapeDtypeStruct((n, kvl), F32), jax.ShapeDtypeStruct((n, rope), F32)]
    args = [zs, zs, g_kv_a.reshape(1, kvl), tab]
    if not with_kv:
        return pl.pallas_call(
            functools.partial(_latent_kernel, rope=rope),
            grid=(n // tm,), in_specs=in_specs, out_specs=out_specs, out_shape=out_shape,
            compiler_params=_cparams(("parallel",)), name="latent",
        )(*args)
    in_specs += [pl.BlockSpec((heads, kvl, nope), lambda i: (0, 0, 0)), pl.BlockSpec((heads, vh, kvl), lambda i: (0, 0, 0))]
    out_specs += [pl.BlockSpec((heads, tm, 2 * nope), lambda i: (0, i, 0)), pl.BlockSpec((heads, vh, tm), lambda i: (0, 0, i))]
    out_shape += [jax.ShapeDtypeStruct((heads, n, 2 * nope), BF16), jax.ShapeDtypeStruct((heads, vh, n), BF16)]
    return pl.pallas_call(
        functools.partial(_kv_prep_kernel, heads=heads, nope=nope, rope=rope),
        grid=(n // tm,), in_specs=in_specs, out_specs=out_specs, out_shape=out_shape,
        compiler_params=_cparams(("parallel",)), name="kv_prep",
    )(*args, w_uk_h, w_uv_t)


def _flash_kernel(qi_ref, ki_ref, q_ref, k_ref, vt_ref, gate_ref, o_ref, m_scr, l_scr, acc_scr, *, heads, vh):
    p = pl.program_id(1)
    qi = qi_ref[p]
    ki = ki_ref[p]
    tk, tq = k_ref.shape[1], q_ref.shape[1]

    @pl.when(ki == 0)
    def _():
        m_scr[...] = jnp.full(m_scr.shape, -jnp.inf, F32)
        l_scr[...] = jnp.zeros(l_scr.shape, F32)
        acc_scr[...] = jnp.zeros(acc_scr.shape, F32)

    def step(h, masked):
        s = lax.dot_general(k_ref[h], q_ref[h], _NT, preferred_element_type=F32)
        if masked:
            kpos = lax.broadcasted_iota(I32, s.shape, 0)
            qpos = lax.broadcasted_iota(I32, s.shape, 1)
            s = jnp.where(kpos <= qpos, s, -jnp.inf)
        m_prev = m_scr[h]
        m_new = jnp.maximum(m_prev, jnp.max(s, axis=0, keepdims=True))
        alpha = jnp.exp(m_prev - m_new)
        pr = jnp.exp(s - m_new)
        l_scr[h] = alpha * l_scr[h] + jnp.sum(pr, axis=0, keepdims=True)
        acc_scr[h] = alpha * acc_scr[h] + jnp.dot(vt_ref[h], pr.astype(BF16), preferred_element_type=F32)
        m_scr[h] = m_new

    @pl.when(ki < qi)
    def _():
        lax.fori_loop(0, heads, lambda h, c: (step(h, False), c)[1], 0)

    @pl.when(ki == qi)
    def _():
        lax.fori_loop(0, heads, lambda h, c: (step(h, True), c)[1], 0)
        for h in range(heads):
            o = (acc_scr[h] / l_scr[h]).T
            g = jax.nn.sigmoid(gate_ref[:, h * vh:(h + 1) * vh].astype(F32))
            o_ref[:, h * vh:(h + 1) * vh] = (o * g).astype(o_ref.dtype)


def _flash_prompt(q, k, vt, zb, gate_off, batch, seq):
    heads, n, width = q.shape
    vh = vt.shape[1]
    d = heads * vh
    tq = _tile(seq, 512, LANES)
    nq = seq // tq
    pairs = [(a, b) for a in range(nq) for b in range(a + 1)]
    qi = jnp.asarray([a for a, _ in pairs], I32)
    ki = jnp.asarray([b for _, b in pairs], I32)
    assert gate_off % d == 0
    grid_spec = pltpu.PrefetchScalarGridSpec(
        num_scalar_prefetch=2,
        grid=(batch, len(pairs)),
        in_specs=[
            pl.BlockSpec((heads, tq, width), lambda b, p, qi, ki: (0, b * nq + qi[p], 0)),
            pl.BlockSpec((heads, tq, width), lambda b, p, qi, ki: (0, b * nq + ki[p], 0)),
            pl.BlockSpec((heads, vh, tq), lambda b, p, qi, ki: (0, 0, b * nq + ki[p])),
            pl.BlockSpec((tq, d), lambda b, p, qi, ki: (b * nq + qi[p], gate_off // d)),
        ],
        out_specs=pl.BlockSpec((tq, d), lambda b, p, qi, ki: (b * nq + qi[p], 0)),
        scratch_shapes=[pltpu.VMEM((heads, 1, tq), F32), pltpu.VMEM((heads, 1, tq), F32), pltpu.VMEM((heads, vh, tq), F32)],
    )
    return pl.pallas_call(
        functools.partial(_flash_kernel, heads=heads, vh=vh),
        grid_spec=grid_spec,
        out_shape=jax.ShapeDtypeStruct((n, d), BF16),
        compiler_params=_cparams(("parallel", "arbitrary")),
        name="flash_prompt",
    )(qi, ki, q, k, vt, zb)


def _absorb_kernel(q_ref, w_ref, o_ref, *, nope):
    o_ref[0] = lax.dot_general(q_ref[0, :, :nope], w_ref[0], _NT, preferred_element_type=F32)


def _absorb_q(qs, w_uk_h):
    heads, ns, width = qs.shape
    _, kvl, nope = w_uk_h.shape
    return pl.pallas_call(
        functools.partial(_absorb_kernel, nope=nope),
        grid=(heads,),
        in_specs=[pl.BlockSpec((1, ns, width), lambda h: (h, 0, 0)), pl.BlockSpec((1, kvl, nope), lambda h: (h, 0, 0))],
        out_specs=pl.BlockSpec((1, ns, kvl), lambda h: (h, 0, 0)),
        out_shape=jax.ShapeDtypeStruct((heads, ns, kvl), F32),
        compiler_params=_cparams(("parallel",)),
        name="absorb_q",
    )(qs, w_uk_h)


def _decode_attn_kernel(pt_ref, ql_ref, qr_ref, cn_ref, krn_ref, cc_hbm, ckr_hbm, o_ref, kc_buf, kr_buf, sem, *, chunk_pages, n_chunks, page):
    s = pl.program_id(0)
    n_seq = pl.num_programs(0)

    def copies(seq, chunk, slot):
        out = []
        for j in range(chunk_pages):
            pg = pt_ref[seq, chunk * chunk_pages + j]
            out.append(pltpu.make_async_copy(cc_hbm.at[pg], kc_buf.at[slot, pl.ds(j * page, page)], sem.at[slot, 0]))
            out.append(pltpu.make_async_copy(ckr_hbm.at[pg], kr_buf.at[slot, pl.ds(j * page, page)], sem.at[slot, 1]))
        return out

    def start(seq, chunk, slot):
        for c in copies(seq, chunk, slot):
            c.start()

    @pl.when(s == 0)
    def _():
        start(0, 0, 0)

    ql = ql_ref[0]
    qr = qr_ref[0]
    cn = cn_ref[0]
    krn = krn_ref[0]
    m = jnp.sum(ql * cn, axis=1, keepdims=True) + jnp.sum(qr * krn, axis=1, keepdims=True)
    l = jnp.ones_like(m)
    acc = jnp.broadcast_to(cn, ql.shape)
    for c in range(n_chunks):
        slot = c % 2
        if c + 1 < n_chunks:
            start(s, c + 1, 1 - slot)
        else:
            @pl.when(s + 1 < n_seq)
            def _():
                start(s + 1, 0, 1 - slot)
        for cp in copies(s, c, slot):
            cp.wait()
        kc = kc_buf[slot]
        kr = kr_buf[slot]
        sc = (lax.dot_general(ql, kc, _NT, preferred_element_type=F32)
              + lax.dot_general(qr, kr, _NT, preferred_element_type=F32))
        m_new = jnp.maximum(m, jnp.max(sc, axis=1, keepdims=True))
        alpha = jnp.exp(m - m_new)
        pr = jnp.exp(sc - m_new)
        l = alpha * l + jnp.sum(pr, axis=1, keepdims=True)
        acc = alpha * acc + jnp.dot(pr, kc, preferred_element_type=F32)
        m = m_new
    o_ref[0] = acc / l


def _decode_attn(page_table, q_lat, q_rope, c_new, kr_new, cache_c, cache_kr):
    ns, heads, kvl = q_lat.shape
    rope = q_rope.shape[-1]
    n_pages = page_table.shape[1]
    page = cache_c.shape[1]
    chunk_pages = _tile(n_pages, 16, 1)
    n_chunks = n_pages // chunk_pages
    assert n_chunks % 2 == 0 or n_chunks == 1
    keys = chunk_pages * page
    grid_spec = pltpu.PrefetchScalarGridSpec(
        num_scalar_prefetch=1,
        grid=(ns,),
        in_specs=[
            pl.BlockSpec((1, heads, kvl), lambda s, pt: (s, 0, 0)),
            pl.BlockSpec((1, heads, rope), lambda s, pt: (s, 0, 0)),
            pl.BlockSpec((1, 1, kvl), lambda s, pt: (s, 0, 0)),
            pl.BlockSpec((1, 1, rope), lambda s, pt: (s, 0, 0)),
            pl.BlockSpec(memory_space=pl.ANY),
            pl.BlockSpec(memory_space=pl.ANY),
        ],
        out_specs=pl.BlockSpec((1, heads, kvl), lambda s, pt: (s, 0, 0)),
        scratch_shapes=[
            pltpu.VMEM((2, keys, kvl), cache_c.dtype),
            pltpu.VMEM((2, keys, rope), cache_kr.dtype),
            pltpu.SemaphoreType.DMA((2, 2)),
        ],
    )
    return pl.pallas_call(
        functools.partial(_decode_attn_kernel, chunk_pages=chunk_pages, n_chunks=n_chunks, page=page),
        grid_spec=grid_spec,
        out_shape=jax.ShapeDtypeStruct((ns, heads, kvl), F32),
        compiler_params=_cparams(("arbitrary",)),
        name="decode_attn",
    )(page_table, q_lat, q_rope, c_new.reshape(ns, 1, kvl), kr_new.reshape(ns, 1, rope), cache_c, cache_kr)


def _unabsorb_kernel(o_ref, w_ref, gate_ref, y_ref):
    y = jnp.dot(o_ref[0].astype(BF16), w_ref[0], preferred_element_type=F32)
    y_ref[...] = (y * jax.nn.sigmoid(gate_ref[...].astype(F32))).astype(y_ref.dtype)


def _unabsorb_o(o_lat_h, w_uv_h, zb, gate_off):
    heads, ns, kvl = o_lat_h.shape
    vh = w_uv_h.shape[-1]
    assert gate_off % vh == 0
    return pl.pallas_call(
        _unabsorb_kernel,
        grid=(heads,),
        in_specs=[
            pl.BlockSpec((1, ns, kvl), lambda h: (h, 0, 0)),
            pl.BlockSpec((1, kvl, vh), lambda h: (h, 0, 0)),
            pl.BlockSpec((ns, vh), lambda h: (0, gate_off // vh + h)),
        ],
        out_specs=pl.BlockSpec((ns, vh), lambda h: (0, h)),
        out_shape=jax.ShapeDtypeStruct((ns, heads * vh), BF16),
        compiler_params=_cparams(("parallel",)),
        name="unabsorb_o",
    )(o_lat_h, w_uv_h, zb)


def _gla_prompt_kernel(q_ref, k_ref, v_ref, go_ref, gg_ref, lr_ref, wa_ref, ba_ref, gn_ref, y_ref, st_ref, st_scr, *, n_chunks, c, scale):
    tb = pl.program_id(2)

    @pl.when(tb == 0)
    def _():
        st_scr[...] = jnp.zeros(st_scr.shape, F32)

    row = lax.broadcasted_iota(I32, (c, c), 0)
    col = lax.broadcasted_iota(I32, (c, c), 1)
    levels = []
    s = c // 2
    while s >= 1:
        levels.append(s)
        s //= 2
    mats = [jnp.where(col <= row, 1.0, 0.0)]
    masks = []
    for s in levels:
        sh = s.bit_length() - 1
        bnd = ((row >> (sh + 1)) << (sh + 1)) + (s - 1)
        mats.append(jnp.where(col <= bnd, 1.0, 0.0))
        masks.append(((row >> (sh + 1)) == (col >> (sh + 1))) & (((row >> sh) & 1) == 1) & (((col >> sh) & 1) == 0))
    cum = jnp.concatenate(mats, axis=0)
    eye = row == col

    def chunk(ci, carry):
        r0 = pl.multiple_of(ci * c, c)
        q = q_ref[pl.ds(r0, c), :].astype(F32) * scale
        k = k_ref[pl.ds(r0, c), :].astype(F32)
        v = v_ref[pl.ds(r0, c), :]
        x = jnp.dot(lr_ref[pl.ds(r0, c), :], wa_ref[...], preferred_element_type=F32, precision=HI) + ba_ref[...]
        la = _log_sigmoid(x) / GLA_TAU
        sums = jnp.dot(cum, la, preferred_element_type=F32, precision=HI)
        bc = sums[:c]
        bl = bc[c - 1:c, :]
        att = jnp.where(eye, jnp.sum(q * k, axis=1, keepdims=True), 0.0)
        for li in range(len(levels)):
            ref = sums[(li + 1) * c:(li + 2) * c]
            qh = (q * jnp.exp(jnp.minimum(bc - ref, 0.0))).astype(BF16)
            kh = (k * jnp.exp(jnp.minimum(ref - bc, 0.0))).astype(BF16)
            a = lax.dot_general(qh, kh, _NT, preferred_element_type=F32)
            att = att + jnp.where(masks[li], a, 0.0)
        st = st_scr[...]
        qt = (q * jnp.exp(bc)).astype(BF16)
        o = (lax.dot_general(qt, st.astype(BF16), _NT, preferred_element_type=F32)
             + jnp.dot(att.astype(BF16), v, preferred_element_type=F32))
        kb = (k * jnp.exp(bl - bc)).astype(BF16)
        st_scr[...] = st * jnp.exp(bl) + lax.dot_general(v, kb, _TN, preferred_element_type=F32)
        on = _rms(o, gn_ref[...])
        go = go_ref[pl.ds(r0, c), :].astype(F32)
        gg = gg_ref[pl.ds(r0, c), :].astype(F32)
        y_ref[pl.ds(r0, c), :] = (on * (go * jax.nn.sigmoid(go)) * jax.nn.sigmoid(gg)).astype(y_ref.dtype)
        return carry

    lax.fori_loop(0, n_chunks, chunk, 0)

    @pl.when(tb == pl.num_programs(2) - 1)
    def _():
        st_ref[0, 0] = st_scr[...].T


def _gla_prompt(zb, zs, offs, w_a2p, b_a2, g_gla, batch, seq, gh, dk, dv):
    n = batch * seq
    c = math.gcd(seq, GLA_CHUNK)
    tb = _tile(seq, 512, c)
    nb = seq // tb
    for key, w in (("gq", dk), ("gk", dk), ("gv", dv), ("g_out", dv), ("gate_gla", dv)):
        assert offs[key] % w == 0
    assert offs["g_lr"] % LANES == 0

    def rows(b, h, t):
        return b * nb + t

    in_specs = [
        pl.BlockSpec((tb, dk), lambda b, h, t: (rows(b, h, t), offs["gq"] // dk + h)),
        pl.BlockSpec((tb, dk), lambda b, h, t: (rows(b, h, t), offs["gk"] // dk + h)),
        pl.BlockSpec((tb, dv), lambda b, h, t: (rows(b, h, t), offs["gv"] // dv + h)),
        pl.BlockSpec((tb, dv), lambda b, h, t: (rows(b, h, t), offs["g_out"] // dv + h)),
        pl.BlockSpec((tb, dv), lambda b, h, t: (rows(b, h, t), offs["gate_gla"] // dv + h)),
        pl.BlockSpec((tb, LANES), lambda b, h, t: (rows(b, h, t), offs["g_lr"] // LANES)),
        pl.BlockSpec((LANES, dk), lambda b, h, t: (0, h)),
        pl.BlockSpec((1, dk), lambda b, h, t: (0, h)),
        pl.BlockSpec((1, dv), lambda b, h, t: (0, 0)),
    ]
    out_specs = [
        pl.BlockSpec((tb, dv), lambda b, h, t: (rows(b, h, t), h)),
        pl.BlockSpec((1, 1, dk, dv), lambda b, h, t: (b, h, 0, 0)),
    ]
    return pl.pallas_call(
        functools.partial(_gla_prompt_kernel, n_chunks=tb // c, c=c, scale=dk ** -0.5),
        grid=(batch, gh, nb),
        in_specs=in_specs,
        out_specs=out_specs,
        out_shape=[jax.ShapeDtypeStruct((n, gh * dv), BF16), jax.ShapeDtypeStruct((batch, gh, dk, dv), F32)],
        scratch_shapes=[pltpu.VMEM((dv, dk), F32)],
        compiler_params=_cparams(("parallel", "parallel", "arbitrary")),
        name="gla_prompt",
    )(zb, zb, zb, zb, zb, zs, w_a2p, b_a2.reshape(1, gh * dk), g_gla.reshape(1, dv))


def _gla_decode_kernel(qk_ref, lrt_ref, wat_ref, bcol_ref, v_ref, go_ref, gg_ref, gn_ref, s0_ref, y_ref, s1_ref, *, sb, scale):
    x = jnp.dot(wat_ref[0], lrt_ref[0], preferred_element_type=F32, precision=HI) + bcol_ref[0]
    a = jnp.exp(_log_sigmoid(x) / GLA_TAU)
    outs = []
    for u in range(sb):
        qc = qk_ref[0, 0, :, u:u + 1] * scale
        kc = qk_ref[0, 0, :, sb + u:sb + u + 1]
        sn = a[:, u:u + 1] * s0_ref[u, 0] + kc * v_ref[u:u + 1, :]
        s1_ref[u, 0] = sn
        outs.append(jnp.sum(qc * sn, axis=0, keepdims=True))
    o = jnp.concatenate(outs, axis=0)
    go = go_ref[...]
    y_ref[...] = _rms(o, gn_ref[...]) * (go * jax.nn.sigmoid(go)) * jax.nn.sigmoid(gg_ref[...])


def _gla_decode(qk_cols, lr_t, wa_t, b_col, gv, g_out, gate_gla, g_gla, state):
    ns, gh, dk, dv = state.shape
    sb = qk_cols.shape[-1] // 2
    return pl.pallas_call(
        functools.partial(_gla_decode_kernel, sb=sb, scale=dk ** -0.5),
        grid=(ns // sb, gh),
        in_specs=[
            pl.BlockSpec((1, 1, dk, 2 * sb), lambda i, h: (i, h, 0, 0)),
            pl.BlockSpec((1, LANES, sb), lambda i, h: (i, 0, 0)),
            pl.BlockSpec((1, dk, LANES), lambda i, h: (h, 0, 0)),
            pl.BlockSpec((1, dk, 1), lambda i, h: (h, 0, 0)),
            pl.BlockSpec((sb, dv), lambda i, h: (i, h)),
            pl.BlockSpec((sb, dv), lambda i, h: (i, h)),
            pl.BlockSpec((sb, dv), lambda i, h: (i, h)),
            pl.BlockSpec((1, dv), lambda i, h: (0, 0)),
            pl.BlockSpec((sb, 1, dk, dv), lambda i, h: (i, h, 0, 0)),
        ],
        out_specs=[pl.BlockSpec((sb, dv), lambda i, h: (i, h)), pl.BlockSpec((sb, 1, dk, dv), lambda i, h: (i, h, 0, 0))],
        out_shape=[jax.ShapeDtypeStruct((ns, gh * dv), F32), jax.ShapeDtypeStruct((ns, gh, dk, dv), F32)],
        compiler_params=_cparams(("parallel", "parallel")),
        name="gla_decode",
    )(qk_cols, lr_t, wa_t, b_col, gv, g_out, gate_gla, g_gla.reshape(1, dv), state)


def _outproj_kernel(ym_ref, yg_ref, h_ref, w_ref, g_ref, wr_ref, br_ref, h1_ref, m_ref, route_ref):
    mix = (ym_ref[...].astype(F32) + yg_ref[...].astype(F32)).astype(BF16)
    h1 = h_ref[...] + jnp.dot(mix, w_ref[...], preferred_element_type=F32)
    h1_ref[...] = h1
    m = _rms(h1, g_ref[...])
    half = m.shape[1] // 2
    m_ref[...] = _pack_bf16_pair(m[:, :half], m[:, half:])
    m_hi = m.astype(BF16)
    m_lo = (m - m_hi.astype(F32)).astype(BF16)
    wr = wr_ref[...]
    w_hi = wr.astype(BF16)
    w_lo = (wr - w_hi.astype(F32)).astype(BF16)
    logits = (jnp.dot(m_hi, w_hi, preferred_element_type=F32) + jnp.dot(m_lo, w_hi, preferred_element_type=F32)
              + jnp.dot(m_hi, w_lo, preferred_element_type=F32) + br_ref[...])
    lane = lax.broadcasted_iota(I32, logits.shape, 1).astype(F32)
    vals, idxs = [], []
    for _ in range(TOP_K):
        mx = jnp.max(logits, axis=1, keepdims=True)
        ix = jnp.min(jnp.where(logits == mx, lane, float(LANES)), axis=1, keepdims=True)
        vals.append(mx)
        idxs.append(ix)
        logits = jnp.where(lane == ix, -jnp.inf, logits)
    ex = [jnp.exp(v - vals[0]) for v in vals]
    den = ex[0]
    for e in ex[1:]:
        den = den + e
    route = jnp.zeros(logits.shape, F32)
    for kk in range(TOP_K):
        route = jnp.where(lane == float(kk), idxs[kk], route)
        route = jnp.where(lane == float(TOP_K + kk), ex[kk] / den, route)
    route_ref[...] = route


def _outproj(y_mla, y_gla, h, w_out_b, g_ffn, w_router_p, b_router_p):
    n, d = h.shape
    tm = _tile(n, 256, 16)
    row = lambda i: (i, 0)
    fixed = lambda i: (0, 0)
    return pl.pallas_call(
        _outproj_kernel,
        grid=(n // tm,),
        in_specs=[
            pl.BlockSpec((tm, d), row), pl.BlockSpec((tm, d), row), pl.BlockSpec((tm, d), row),
            pl.BlockSpec((d, d), fixed), pl.BlockSpec((1, d), fixed),
            pl.BlockSpec((d, LANES), fixed), pl.BlockSpec((1, LANES), fixed),
        ],
        out_specs=[pl.BlockSpec((tm, d), row), pl.BlockSpec((tm, d // 2), row), pl.BlockSpec((tm, LANES), row)],
        out_shape=[jax.ShapeDtypeStruct((n, d), F32), jax.ShapeDtypeStruct((n, d // 2), U32), jax.ShapeDtypeStruct((n, LANES), F32)],
        compiler_params=_cparams(("parallel",)),
        name="outproj",
    )(y_mla, y_gla, h, w_out_b, g_ffn.reshape(1, d), w_router_p, b_router_p)


def _dispatch_kernel(idx_ref, x_ref, o_ref, *, rows):
    base = pl.program_id(0) * rows

    def body(r, carry):
        tok = idx_ref[base + r]
        o_ref[pl.ds(r, 1), :] = x_ref[pl.ds(tok, 1), :]
        return carry

    lax.fori_loop(0, rows, body, 0, unroll=8)


def _dispatch(buf_tok, m_packed):
    rows_total = buf_tok.shape[0]
    n, w = m_packed.shape
    rows = _tile(rows_total, 512, 8)
    grid_spec = pltpu.PrefetchScalarGridSpec(
        num_scalar_prefetch=1,
        grid=(rows_total // rows,),
        in_specs=[pl.BlockSpec(memory_space=pltpu.VMEM)],
        out_specs=pl.BlockSpec((rows, w), lambda i, idx: (i, 0)),
    )
    return pl.pallas_call(
        functools.partial(_dispatch_kernel, rows=rows),
        grid_spec=grid_spec,
        out_shape=jax.ShapeDtypeStruct((rows_total, w), U32),
        compiler_params=_cparams(("arbitrary",), 56),
        name="moe_dispatch",
    )(buf_tok, m_packed)


def _moe_kernel(vt_ref, ve_ref, b0_ref, b1_ref, zf_ref, xs_ref, wg_ref, wu_ref, bg_ref, bu_ref, wd_ref, bd_ref, ys_ref,
                x_scr, a_scr, wg_scr, wu_scr, wd_scr, *, n_up, n_down, tf, rb):
    v = pl.program_id(0)
    j = pl.program_id(1)
    b0 = b0_ref[v]
    b1 = b1_ref[v]
    fill = zf_ref[v] == 1
    active = (b1 > b0) & jnp.logical_not(fill)

    @pl.when(fill & (j == 0))
    def _():
        def blk(b, carry):
            r0 = pl.multiple_of(b * rb, rb)
            for nn in range(n_down):
                ys_ref[nn, 0, pl.ds(r0, rb), :] = jnp.zeros((rb, ys_ref.shape[3]), U32)
            return carry
        lax.fori_loop(b0, b1, blk, 0)

    @pl.when(active & (j == 0))
    def _():
        def blk(b, carry):
            r0 = pl.multiple_of(b * rb, rb)
            hi, lo = _unpack_bf16_pair(xs_ref[pl.ds(r0, rb), :])
            x_scr[pl.ds(r0, rb), :] = jnp.concatenate([hi, lo], axis=1).astype(BF16)
            return carry
        lax.fori_loop(b0, b1, blk, 0)

    @pl.when(active & (j < n_up))
    def _():
        wg_scr[...] = wg_ref[0].astype(BF16)
        wu_scr[...] = wu_ref[0].astype(BF16)

        def blk(b, carry):
            r0 = pl.multiple_of(b * rb, rb)
            x = x_scr[pl.ds(r0, rb), :]
            g = jnp.dot(x, wg_scr[...], preferred_element_type=F32) + bg_ref[0]
            u = jnp.dot(x, wu_scr[...], preferred_element_type=F32) + bu_ref[0]
            gate = jnp.minimum(g, SWIGLU_LIMIT)
            up = jnp.clip(u, -SWIGLU_LIMIT, SWIGLU_LIMIT)
            glu = gate * jax.nn.sigmoid(gate * SWIGLU_ALPHA)
            a_scr[j, pl.ds(r0, rb), :] = ((up + 1.0) * glu).astype(BF16)
            return carry
        lax.fori_loop(b0, b1, blk, 0)

    @pl.when(active & (j >= n_up))
    def _():
        nn = j - n_up
        wd_scr[...] = wd_ref[0].astype(BF16)

        def blk(b, carry):
            r0 = pl.multiple_of(b * rb, rb)
            y = bd_ref[0]
            for f in range(n_up):
                y = y + jnp.dot(a_scr[f, pl.ds(r0, rb), :], wd_scr[f * tf:(f + 1) * tf, :], preferred_element_type=F32)
            half = y.shape[1] // 2
            ys_ref[nn, 0, pl.ds(r0, rb), :] = _pack_bf16_pair(y[:, :half], y[:, half:])
            return carry
        lax.fori_loop(b0, b1, blk, 0)


def _moe_experts(meta, xs, w_gu, b_gu, w_down, b_down, tile_rows):
    v_tile, v_exp, v_b0, v_b1, v_zero = meta
    rows_total, dh = xs.shape
    d = 2 * dh
    e, _, f2 = w_gu.shape
    ff = f2 // 2
    tf = _tile(ff, 256, LANES)
    tn = _tile(d, 256, LANES)
    n_up, n_down = ff // tf, d // tn
    n_tiles = rows_total // tile_rows
    n_visits = v_tile.shape[0]
    up_idx = lambda j, zf: jnp.where(zf == 1, n_up - 1, jnp.minimum(j, n_up - 1))
    dn_idx = lambda j, zf: jnp.where(zf == 1, n_down - 1, jnp.maximum(j - n_up, 0))
    grid_spec = pltpu.PrefetchScalarGridSpec(
        num_scalar_prefetch=5,
        grid=(n_visits, n_up + n_down),
        in_specs=[
            pl.BlockSpec((tile_rows, dh), lambda v, j, vt, ve, b0, b1, zf: (vt[v], 0)),
            pl.BlockSpec((1, d, tf), lambda v, j, vt, ve, b0, b1, zf: (ve[v], 0, up_idx(j, zf[v]))),
            pl.BlockSpec((1, d, tf), lambda v, j, vt, ve, b0, b1, zf: (ve[v], 0, n_up + up_idx(j, zf[v]))),
            pl.BlockSpec((1, 1, tf), lambda v, j, vt, ve, b0, b1, zf: (ve[v], 0, up_idx(j, zf[v]))),
            pl.BlockSpec((1, 1, tf), lambda v, j, vt, ve, b0, b1, zf: (ve[v], 0, n_up + up_idx(j, zf[v]))),
            pl.BlockSpec((1, ff, tn), lambda v, j, vt, ve, b0, b1, zf: (ve[v], 0, dn_idx(j, zf[v]))),
            pl.BlockSpec((1, 1, tn), lambda v, j, vt, ve, b0, b1, zf: (ve[v], 0, dn_idx(j, zf[v]))),
        ],
        out_specs=pl.BlockSpec((n_down, 1, tile_rows, tn // 2), lambda v, j, vt, ve, b0, b1, zf: (0, vt[v], 0, 0)),
        scratch_shapes=[
            pltpu.VMEM((tile_rows, d), BF16),
            pltpu.VMEM((n_up, tile_rows, tf), BF16),
            pltpu.VMEM((d, tf), BF16),
            pltpu.VMEM((d, tf), BF16),
            pltpu.VMEM((ff, tn), BF16),
        ],
    )
    return pl.pallas_call(
        functools.partial(_moe_kernel, n_up=n_up, n_down=n_down, tf=tf, rb=EXPERT_ROW_BLOCK),
        grid_spec=grid_spec,
        out_shape=jax.ShapeDtypeStruct((n_down, n_tiles, tile_rows, tn // 2), U32),
        compiler_params=_cparams(("arbitrary", "arbitrary"), 56),
        name="moe_experts",
    )(v_tile, v_exp, v_b0, v_b1, v_zero, xs, w_gu, w_gu, b_gu.reshape(e, 1, f2), b_gu.reshape(e, 1, f2), w_down, b_down.reshape(e, 1, d))


def _combine_kernel(dest_ref, ys_ref, route_ref, o_ref, stage, *, tm):
    base = pl.program_id(1) * tm

    def body(t, carry):
        for kk in range(TOP_K):
            r = dest_ref[(base + t) * TOP_K + kk]
            stage[kk, pl.ds(t, 1), :] = ys_ref[0, pl.ds(r, 1), :]
        return carry

    lax.fori_loop(0, tm, body, 0, unroll=4)
    acc_hi = acc_lo = None
    for kk in range(TOP_K):
        g = route_ref[:, TOP_K + kk:TOP_K + kk + 1]
        hi, lo = _unpack_bf16_pair(stage[kk])
        acc_hi = g * hi if acc_hi is None else acc_hi + g * hi
        acc_lo = g * lo if acc_lo is None else acc_lo + g * lo
    o_ref[...] = jnp.concatenate([acc_hi, acc_lo], axis=1)


def _combine(dest, ys, route):
    n_down, rows_total, wh = ys.shape
    n = route.shape[0]
    tm = _tile(n, 256, 8)
    grid_spec = pltpu.PrefetchScalarGridSpec(
        num_scalar_prefetch=1,
        grid=(n_down, n // tm),
        in_specs=[
            pl.BlockSpec((1, rows_total, wh), lambda c, i, dest: (c, 0, 0)),
            pl.BlockSpec((tm, LANES), lambda c, i, dest: (i, 0)),
        ],
        out_specs=pl.BlockSpec((tm, 2 * wh), lambda c, i, dest: (i, c)),
        scratch_shapes=[pltpu.VMEM((TOP_K, tm, wh), U32)],
    )
    return pl.pallas_call(
        functools.partial(_combine_kernel, tm=tm),
        grid_spec=grid_spec,
        out_shape=jax.ShapeDtypeStruct((n, n_down * 2 * wh), F32),
        compiler_params=_cparams(("arbitrary", "arbitrary"), 56),
        name="moe_combine",
    )(dest, ys, route)


def _route_meta(top_i, n_experts, tile_rows):
    n_tok = top_i.shape[0]
    n_asg = n_tok * TOP_K
    rb = EXPERT_ROW_BLOCK
    e_flat = top_i.reshape(n_asg)
    tok_flat = (jnp.arange(n_asg, dtype=I32) // TOP_K).astype(I32)
    onehot = jax.nn.one_hot(e_flat, n_experts, dtype=I32)
    rank = jnp.take_along_axis(jnp.cumsum(onehot, axis=0), e_flat[:, None], axis=1)[:, 0] - 1
    counts = onehot.sum(axis=0)
    padded = (counts + rb - 1) // rb * rb
    ends = jnp.cumsum(padded)
    starts = ends - padded
    dest = (starts[e_flat] + rank).astype(I32)
    rows_total = (-(-n_asg // rb) + n_experts) * rb
    rows_total = -(-rows_total // tile_rows) * tile_rows
    buf_tok = jnp.zeros((rows_total,), I32).at[dest].set(tok_flat)
    n_tiles = rows_total // tile_rows
    total = ends[-1].astype(I32)
    bp = jnp.sort(jnp.concatenate([jnp.arange(n_tiles, dtype=I32) * tile_rows, starts.astype(I32), total[None]]))
    lo = bp
    length = jnp.concatenate([bp[1:], jnp.full((1,), rows_total, I32)]) - lo
    valid = length > 0
    order = jnp.argsort(jnp.logical_not(valid), stable=True)
    lo, length = lo[order], length[order]
    n_valid = jnp.sum(valid.astype(I32))
    pos = jnp.arange(lo.shape[0], dtype=I32)
    src = jnp.minimum(pos, n_valid - 1)
    lo_s = lo[src]
    live = pos < n_valid
    v_tile = (lo_s // tile_rows).astype(I32)
    v_exp = jnp.minimum(jnp.searchsorted(ends, jnp.minimum(lo_s, total - 1), side="right"), n_experts - 1).astype(I32)
    v_b0 = jnp.where(live, (lo_s % tile_rows) // rb, 0).astype(I32)
    v_b1 = jnp.where(live, v_b0 + length[src] // rb, 0).astype(I32)
    v_zero = ((lo_s >= total) | jnp.logical_not(live)).astype(I32)
    return buf_tok, dest, (v_tile, v_exp, v_b0, v_b1, v_zero)


def _ple_kernel(h1_ref, moe_ref, p_ref, wp_ref, wg_ref, gp_ref, gf_ref, y_ref, *, final):
    h2 = h1_ref[...] + moe_ref[...]
    u = _rms(h2, gp_ref[...]).astype(BF16)
    gate = jax.nn.sigmoid(jnp.dot(u, wg_ref[...], preferred_element_type=F32))
    pe = jnp.dot(p_ref[...].astype(BF16), wp_ref[...], preferred_element_type=F32)
    y = h2 + pe * gate
    y_ref[...] = _rms(y, gf_ref[...]) if final else y


def _ple_final(h1, moe, row_off, p, w_ple_b, w_ple_gate_b, g_ple, g_final, final):
    n, d = h1.shape
    pd = p.shape[1]
    tm = _tile(n, 256, 8)
    assert row_off % tm == 0
    row = lambda i: (i, 0)
    fixed = lambda i: (0, 0)
    return pl.pallas_call(
        functools.partial(_ple_kernel, final=final),
        grid=(n // tm,),
        in_specs=[
            pl.BlockSpec((tm, d), row), pl.BlockSpec((tm, d), lambda i: (row_off // tm + i, 0)), pl.BlockSpec((tm, pd), row),
            pl.BlockSpec((pd, d), fixed), pl.BlockSpec((d, d), fixed), pl.BlockSpec((1, d), fixed), pl.BlockSpec((1, d), fixed),
        ],
        out_specs=pl.BlockSpec((tm, d), row),
        out_shape=jax.ShapeDtypeStruct((n, d), F32),
        compiler_params=_cparams(("parallel",)),
        name="ple_final",
    )(h1, moe, p, w_ple_b, w_ple_gate_b, g_ple.reshape(1, d), g_final.reshape(1, d))


def _rope_table(pos, rope):
    half = rope // 2
    inv_freq = jnp.power(ROPE_THETA, -jnp.arange(half, dtype=F32) / half)
    ang = pos.astype(F32)[:, None] * inv_freq[None, :]
    cos, sin = jnp.cos(ang), jnp.sin(ang)
    return jnp.concatenate([cos, cos, -sin, sin], axis=1)


def _swap_halves(w):
    half = w.shape[-1] // 2
    return jnp.concatenate([w[..., half:], w[..., :half]], axis=-1)


def kernel(x_prompt, x_sample, cache_kv_latent, cache_k_rope, state_gla, page_table, p_prompt, p_sample, g_mix, w_in, g_q_a, w_uq, g_kv_a, w_uk, w_uv, w_a2, b_a2, g_gla, w_out, g_ffn, w_router, b_router, w_gu, b_gu, w_down, b_down, g_ple, w_ple, w_ple_gate, g_final):
    depth = w_in.shape[0]
    batch, seq, d = x_prompt.shape
    ns, dec_seq, _ = x_sample.shape
    assert dec_seq == 1
    ql, kvl = g_q_a.shape[-1], g_kv_a.shape[-1]
    rope = cache_k_rope.shape[-1]
    heads, nope = w_uk.shape[2], w_uk.shape[3]
    vh = w_uv.shape[3]
    _, _, gh, dk, dv = state_gla.shape
    lr = w_a2.shape[1]
    n_experts = w_router.shape[-1]
    past = page_table.shape[1] * cache_kv_latent.shape[2]
    assert 2 * rope == LANES and nope == LANES and heads * vh == d and lr <= LANES and n_experts <= LANES
    scale = 1.0 / math.sqrt(nope + rope)
    n_p = batch * seq
    tile_rows = 1024

    big_segs = (("gv", gh * dv), ("g_out", gh * dv), ("gate_mla", d), ("gate_gla", d), ("gq", gh * dk), ("gk", gh * dk))
    offs, o = {}, 0
    for key, w in big_segs:
        offs[key] = o
        o += w
    offs["g_lr"] = ql + kvl + 2 * rope
    in_splits = (ql, kvl, rope, gh * dk, gh * dk, gh * dv, gh * dv, lr, d, d)
    in_offsets = tuple(sum(in_splits[:i + 1]) for i in range(len(in_splits) - 1))

    tab_p = jnp.tile(_rope_table(jnp.arange(seq), rope), (batch, 1))
    tab_s = _rope_table(jnp.full((ns,), past, I32), rope)

    h_p = x_prompt.reshape(n_p, d)
    h_s = x_sample.reshape(ns, d)
    kv_p, kr_p, st_p, kv_s, kr_s, st_s = [], [], [], [], [], []
    for l in range(depth):
        seg = dict(zip(("c_q", "c_kv", "k_r", "gq", "gk", "gv", "g_out", "g_lr", "gate_mla", "gate_gla"), jnp.split(w_in[l], in_offsets, axis=1)))
        w_big = jnp.concatenate([seg[k] for k, _ in big_segs], axis=1).astype(BF16)
        w_small = jnp.concatenate([seg["c_q"], seg["c_kv"], seg["k_r"], _swap_halves(seg["k_r"]),
                                   jnp.pad(seg["g_lr"], ((0, 0), (0, LANES - lr)))], axis=1).astype(BF16)
        wq = w_uq[l].reshape(ql, heads, nope + rope)
        w_uq_p = jnp.concatenate([wq[..., :nope], wq[..., nope:], _swap_halves(wq[..., nope:])], axis=-1).transpose(1, 0, 2).astype(BF16)
        w_uk_h = w_uk[l].transpose(1, 0, 2).astype(BF16)
        w_uv_h = w_uv[l].transpose(1, 0, 2).astype(BF16)
        w_uv_t = w_uv[l].transpose(1, 2, 0).astype(BF16)
        w_a2p = jnp.pad(w_a2[l], ((0, LANES - lr), (0, 0)))
        w_out_b = w_out[l].astype(BF16)
        w_router_p = jnp.pad(w_router[l], ((0, 0), (0, LANES - n_experts)))
        b_router_p = jnp.pad(b_router[l], (0, LANES - n_experts), constant_values=NEG_BIG).reshape(1, LANES)
        w_ple_b = w_ple[l].astype(BF16)
        w_ple_gate_b = w_ple_gate[l].astype(BF16)

        def front(h, tab):
            a = _rms_cast(h, g_mix[l])
            zb = _matmul(a, w_big, BF16, "inproj_big")
            zs = _matmul(a, w_small, F32, "inproj_small")
            q = _q_proj(zs, g_q_a[l], w_uq_p, tab, scale)
            return zb, zs, q

        zb, zs, q = front(h_p, tab_p)
        c_kv, k_r, k_heads, v_t = _kv_prep(zs, g_kv_a[l], tab_p, w_uk_h, w_uv_t, ql, rope, True)
        y_mla = _flash_prompt(q, k_heads, v_t, zb, offs["gate_mla"], batch, seq)
        y_gla, s_new = _gla_prompt(zb, zs, offs, w_a2p, b_a2[l], g_gla[l], batch, seq, gh, dk, dv)
        h1_p, m_p, route_p = _outproj(y_mla, y_gla, h_p, w_out_b, g_ffn[l], w_router_p, b_router_p)
        kv_p.append(c_kv.reshape(batch, seq, kvl)); kr_p.append(k_r.reshape(batch, seq, rope)); st_p.append(s_new)

        zb, zs, q = front(h_s, tab_s)
        c_kv, k_r = _kv_prep(zs, g_kv_a[l], tab_s, w_uk_h, w_uv_t, ql, rope, False)
        q_lat = _absorb_q(q, w_uk_h).transpose(1, 0, 2)
        q_rope = q[:, :, nope:nope + rope].astype(F32).transpose(1, 0, 2)
        o_lat = _decode_attn(page_table, q_lat, q_rope, c_kv, k_r, cache_kv_latent[l], cache_k_rope[l])
        y_mla = _unabsorb_o(o_lat.transpose(1, 0, 2), w_uv_h, zb, offs["gate_mla"])
        sb = 8
        zf = zb.astype(F32)
        cols = lambda key, w: zf[:, offs[key]:offs[key] + w]
        to_cols = lambda z: z.reshape(ns // sb, sb, gh, dk).transpose(0, 2, 3, 1)
        qk_cols = jnp.concatenate([to_cols(cols("gq", gh * dk)), to_cols(cols("gk", gh * dk))], axis=-1)
        lr_t = zs[:, offs["g_lr"]:offs["g_lr"] + LANES].reshape(ns // sb, sb, LANES).transpose(0, 2, 1)
        wa_t = w_a2p.reshape(LANES, gh, dk).transpose(1, 2, 0)
        b_col = b_a2[l].reshape(gh, dk, 1)
        y_gla, s_new = _gla_decode(qk_cols, lr_t, wa_t, b_col, cols("gv", gh * dv), cols("g_out", gh * dv),
                                   cols("gate_gla", gh * dv), g_gla[l], state_gla[l])
        h1_s, m_s, route_s = _outproj(y_mla, y_gla, h_s, w_out_b, g_ffn[l], w_router_p, b_router_p)
        kv_s.append(c_kv.reshape(ns, 1, kvl)); kr_s.append(k_r.reshape(ns, 1, rope)); st_s.append(s_new)

        m_all = jnp.concatenate([m_p, m_s], axis=0)
        route = jnp.concatenate([route_p, route_s], axis=0)
        top_i = route[:, :TOP_K].astype(I32)
        buf_tok, dest, meta = _route_meta(top_i, n_experts, tile_rows)
        xs = _dispatch(buf_tok, m_all)
        ys = _moe_experts(meta, xs, w_gu[l], b_gu[l], w_down[l], b_down[l], tile_rows)
        ys = ys.reshape(ys.shape[0], ys.shape[1] * ys.shape[2], ys.shape[3])
        moe = _combine(dest, ys, route)

        last = l == depth - 1
        h_p = _ple_final(h1_p, moe, 0, p_prompt[l].reshape(n_p, -1), w_ple_b, w_ple_gate_b, g_ple[l], g_final, last)
        h_s = _ple_final(h1_s, moe, n_p, p_sample[l].reshape(ns, -1), w_ple_b, w_ple_gate_b, g_ple[l], g_final, last)

    return (h_p.reshape(batch, seq, d), h_s.reshape(ns, 1, d), jnp.stack(kv_p), jnp.stack(kr_p), jnp.stack(st_p),
            jnp.stack(kv_s), jnp.stack(kr_s), jnp.stack(st_s))
```

```python
import functools
import math

import jax
import jax.numpy as jnp
from jax import lax
from jax.experimental import pallas as pl
from jax.experimental.pallas import tpu as pltpu

F32 = jnp.float32
BF16 = jnp.bfloat16
U32 = jnp.uint32
I32 = jnp.int32

EPS = 1e-6
ROPE_THETA = 10000.0
GLA_TAU = 16.0
GLA_CHUNK = 64
TOP_K = 4
SWIGLU_LIMIT = 7.0
SWIGLU_ALPHA = 1.702
EXPERT_ROW_BLOCK = 128
LANES = 128
MIB = 1024 * 1024
SUM_ROWS = 16
LOG2E = 1.4426950408889634
HI = lax.Precision.HIGHEST
NEG_BIG = -1e30

_NT = (((1,), (1,)), ((), ()))
_TN = (((0,), (0,)), ((), ()))


def _cparams(sem, vmem_mib=48):
    return pltpu.CompilerParams(dimension_semantics=sem, vmem_limit_bytes=vmem_mib * MIB)


def _tile(n, cap, mult):
    best = None
    for t in range(mult, min(n, cap) + 1, mult):
        if n % t == 0:
            best = t
    assert best is not None, (n, cap, mult)
    return best


def _rms(x, g):
    return x * lax.rsqrt(jnp.mean(x * x, axis=-1, keepdims=True) + EPS) * g


def _log_sigmoid(x):
    return jnp.minimum(x, 0.0) - jnp.log1p(jnp.exp(-jnp.abs(x)))


def _pack_bf16_pair(hi, lo):
    hb = pltpu.bitcast(hi.astype(BF16).astype(F32), U32)
    lb = pltpu.bitcast(lo.astype(BF16).astype(F32), U32)
    return hb | (lb >> 16)


def _unpack_bf16_pair(w):
    hi = pltpu.bitcast(w & jnp.uint32(0xFFFF0000), F32)
    lo = pltpu.bitcast(w << 16, F32)
    return hi, lo


def _rms_cast_kernel(x_ref, g_ref, o_ref):
    o_ref[...] = _rms(x_ref[...], g_ref[...]).astype(o_ref.dtype)


def _rms_cast(x, g):
    n, d = x.shape
    tm = _tile(n, 512, 16)
    return pl.pallas_call(
        _rms_cast_kernel,
        grid=(n // tm,),
        in_specs=[pl.BlockSpec((tm, d), lambda i: (i, 0)), pl.BlockSpec((1, d), lambda i: (0, 0))],
        out_specs=pl.BlockSpec((tm, d), lambda i: (i, 0)),
        out_shape=jax.ShapeDtypeStruct((n, d), BF16),
        compiler_params=_cparams(("parallel",)),
        name="rms_cast",
    )(x, g.reshape(1, d))


def _mm_kernel(x_ref, w_ref, o_ref):
    o_ref[...] = jnp.dot(x_ref[...], w_ref[...], preferred_element_type=F32).astype(o_ref.dtype)


def _matmul(x, w, out_dtype, name):
    m, k = x.shape
    n = w.shape[1]
    tm = _tile(m, 1024, 16)
    tn = _tile(n, 1280, LANES)
    return pl.pallas_call(
        _mm_kernel,
        grid=(n // tn, m // tm),
        in_specs=[pl.BlockSpec((tm, k), lambda j, i: (i, 0)), pl.BlockSpec((k, tn), lambda j, i: (0, j))],
        out_specs=pl.BlockSpec((tm, tn), lambda j, i: (i, j)),
        out_shape=jax.ShapeDtypeStruct((m, n), out_dtype),
        compiler_params=_cparams(("parallel", "parallel")),
        name=name,
    )(x, w)


def _rope_combine(y2, tab):
    half = y2.shape[1] // 2
    t = y2 * tab
    r = t + pltpu.roll(t, half, axis=1)
    lane = lax.broadcasted_iota(I32, r.shape, 1)
    return jnp.where(lane < half, r, 0.0)


def _q_proj_kernel(cq_ref, g_ref, w_ref, tab_ref, q_ref, *, heads, nope, scale):
    cqn = _rms(cq_ref[...], g_ref[...]).astype(BF16)
    tab = tab_ref[...]
    for h in range(heads):
        y = jnp.dot(cqn, w_ref[h], preferred_element_type=F32)
        q_ref[h, :, :nope] = (y[:, :nope] * scale).astype(BF16)
        q_ref[h, :, nope:] = (_rope_combine(y[:, nope:], tab) * scale).astype(BF16)


def _q_proj(zs, g_q_a, w_uq_p, tab, scale):
    n = zs.shape[0]
    heads, ql, width = w_uq_p.shape
    tm = _tile(n, 512, 16)
    return pl.pallas_call(
        functools.partial(_q_proj_kernel, heads=heads, nope=width // 2, scale=scale),
        grid=(n // tm,),
        in_specs=[
            pl.BlockSpec((tm, ql), lambda i: (i, 0)),
            pl.BlockSpec((1, ql), lambda i: (0, 0)),
            pl.BlockSpec((heads, ql, width), lambda i: (0, 0, 0)),
            pl.BlockSpec((tm, LANES), lambda i: (i, 0)),
        ],
        out_specs=pl.BlockSpec((heads, tm, width), lambda i: (0, i, 0)),
        out_shape=jax.ShapeDtypeStruct((heads, n, width), BF16),
        compiler_params=_cparams(("parallel",)),
        name="q_proj",
    )(zs, g_q_a.reshape(1, ql), w_uq_p, tab)


def _kv_prep_kernel(ckv_ref, krs_ref, g_ref, tab_ref, wuk_ref, wuvt_ref, c_ref, kr_ref, k_ref, vt_ref, *, heads, nope, rope, vh):
    c = _rms(ckv_ref[...], g_ref[...])
    c_ref[...] = c
    cb = c.astype(BF16)
    r = _rope_combine(krs_ref[...], tab_ref[...])
    kr_ref[...] = r[:, :rope]
    rb = r.astype(BF16)
    for h in range(heads):
        k_ref[h, :, :nope] = jnp.dot(cb, wuk_ref[h], preferred_element_type=F32).astype(BF16)
        k_ref[h, :, nope:] = rb
        vt_ref[h, :vh, :] = lax.dot_general(wuvt_ref[h], cb, _NT, preferred_element_type=F32).astype(BF16)
        vt_ref[h, vh:, :] = jnp.ones((SUM_ROWS, cb.shape[0]), BF16)


def _latent_kernel(ckv_ref, krs_ref, g_ref, tab_ref, c_ref, kr_ref, *, rope):
    c_ref[...] = _rms(ckv_ref[...], g_ref[...])
    kr_ref[...] = _rope_combine(krs_ref[...], tab_ref[...])[:, :rope]


def _kv_prep(zs, g_kv_a, tab, w_uk_h, w_uv_t, ql, rope, with_kv):
    n = zs.shape[0]
    heads, kvl, nope = w_uk_h.shape
    vh = w_uv_t.shape[1]
    assert ql % kvl == 0 and (ql + kvl) % LANES == 0
    tm = _tile(n, 640, LANES if with_kv else 8)
    in_specs = [
        pl.BlockSpec((tm, kvl), lambda i: (i, ql // kvl)),
        pl.BlockSpec((tm, LANES), lambda i: (i, (ql + kvl) // LANES)),
        pl.BlockSpec((1, kvl), lambda i: (0, 0)),
        pl.BlockSpec((tm, LANES), lambda i: (i, 0)),
    ]
    out_specs = [pl.BlockSpec((tm, kvl), lambda i: (i, 0)), pl.BlockSpec((tm, rope), lambda i: (i, 0))]
    out_shape = [jax.ShapeDtypeStruct((n, kvl), F32), jax.ShapeDtypeStruct((n, rope), F32)]
    args = [zs, zs, g_kv_a.reshape(1, kvl), tab]
    if not with_kv:
        return pl.pallas_call(
            functools.partial(_latent_kernel, rope=rope),
            grid=(n // tm,), in_specs=in_specs, out_specs=out_specs, out_shape=out_shape,
            compiler_params=_cparams(("parallel",)), name="latent",
        )(*args)
    in_specs += [pl.BlockSpec((heads, kvl, nope), lambda i: (0, 0, 0)), pl.BlockSpec((heads, vh, kvl), lambda i: (0, 0, 0))]
    out_specs += [pl.BlockSpec((heads, tm, 2 * nope), lambda i: (0, i, 0)), pl.BlockSpec((heads, vh + SUM_ROWS, tm), lambda i: (0, 0, i))]
    out_shape += [jax.ShapeDtypeStruct((heads, n, 2 * nope), BF16), jax.ShapeDtypeStruct((heads, vh + SUM_ROWS, n), BF16)]
    return pl.pallas_call(
        functools.partial(_kv_prep_kernel, heads=heads, nope=nope, rope=rope, vh=vh),
        grid=(n // tm,), in_specs=in_specs, out_specs=out_specs, out_shape=out_shape,
        compiler_params=_cparams(("parallel",)), name="kv_prep",
    )(*args, w_uk_h, w_uv_t)


def _flash_kernel(qi_ref, ki_ref, q_ref, k_ref, vt_ref, gate_ref, o_ref, m_scr, acc_scr, s_scr, *, heads, vh):
    p = pl.program_id(1)
    qi = qi_ref[p]
    ki = ki_ref[p]

    @pl.when(ki == 0)
    def _():
        m_scr[...] = jnp.full(m_scr.shape, -jnp.inf, F32)
        acc_scr[...] = jnp.zeros(acc_scr.shape, F32)

    def scores(h, slot):
        s_scr[slot] = lax.dot_general(k_ref[h], q_ref[h], _NT, preferred_element_type=F32)

    def update(h, slot, masked):
        s = s_scr[slot]
        if masked:
            kpos = lax.broadcasted_iota(I32, s.shape, 0)
            qpos = lax.broadcasted_iota(I32, s.shape, 1)
            s = jnp.where(kpos <= qpos, s, -jnp.inf)
        m_prev = m_scr[h]
        m_new = jnp.maximum(m_prev, jnp.max(s, axis=0, keepdims=True))
        pr = jnp.exp2(s - m_new).astype(BF16)
        acc_scr[h] = jnp.exp2(m_prev - m_new) * acc_scr[h] + jnp.dot(vt_ref[h], pr, preferred_element_type=F32)
        m_scr[h] = m_new

    def run(masked):
        scores(0, 0)

        def pair(i, carry):
            h0 = 2 * i
            scores(h0 + 1, 1)
            update(h0, 0, masked)
            scores(jnp.minimum(h0 + 2, heads - 1), 0)
            update(h0 + 1, 1, masked)
            return carry

        lax.fori_loop(0, heads // 2, pair, 0)

    @pl.when(ki < qi)
    def _():
        run(False)

    @pl.when(ki == qi)
    def _():
        run(True)
        for h in range(heads):
            acc = acc_scr[h]
            o = (acc[:vh] / acc[vh:vh + 1]).T
            g = jax.nn.sigmoid(gate_ref[:, h * vh:(h + 1) * vh].astype(F32))
            o_ref[:, h * vh:(h + 1) * vh] = (o * g).astype(o_ref.dtype)


def _flash_prompt(q, k, vt, zb, gate_off, batch, seq):
    heads, n, width = q.shape
    vr = vt.shape[1]
    vh = vr - SUM_ROWS
    d = heads * vh
    assert heads % 2 == 0
    tq = _tile(seq, 512, LANES)
    nq = seq // tq
    pairs = [(a, b) for a in range(nq) for b in range(a + 1)]
    qi = jnp.asarray([a for a, _ in pairs], I32)
    ki = jnp.asarray([b for _, b in pairs], I32)
    assert gate_off % d == 0
    grid_spec = pltpu.PrefetchScalarGridSpec(
        num_scalar_prefetch=2,
        grid=(batch, len(pairs)),
        in_specs=[
            pl.BlockSpec((heads, tq, width), lambda b, p, qi, ki: (0, b * nq + qi[p], 0)),
            pl.BlockSpec((heads, tq, width), lambda b, p, qi, ki: (0, b * nq + ki[p], 0)),
            pl.BlockSpec((heads, vr, tq), lambda b, p, qi, ki: (0, 0, b * nq + ki[p])),
            pl.BlockSpec((tq, d), lambda b, p, qi, ki: (b * nq + qi[p], gate_off // d)),
        ],
        out_specs=pl.BlockSpec((tq, d), lambda b, p, qi, ki: (b * nq + qi[p], 0)),
        scratch_shapes=[pltpu.VMEM((heads, 1, tq), F32), pltpu.VMEM((heads, vr, tq), F32), pltpu.VMEM((2, tq, tq), F32)],
    )
    return pl.pallas_call(
        functools.partial(_flash_kernel, heads=heads, vh=vh),
        grid_spec=grid_spec,
        out_shape=jax.ShapeDtypeStruct((n, d), BF16),
        compiler_params=_cparams(("parallel", "arbitrary")),
        name="flash_prompt",
    )(qi, ki, q, k, vt, zb)


def _absorb_kernel(q_ref, w_ref, o_ref, *, nope):
    o_ref[0] = lax.dot_general(q_ref[0, :, :nope], w_ref[0], _NT, preferred_element_type=F32)


def _absorb_q(qs, w_uk_h):
    heads, ns, width = qs.shape
    _, kvl, nope = w_uk_h.shape
    return pl.pallas_call(
        functools.partial(_absorb_kernel, nope=nope),
        grid=(heads,),
        in_specs=[pl.BlockSpec((1, ns, width), lambda h: (h, 0, 0)), pl.BlockSpec((1, kvl, nope), lambda h: (h, 0, 0))],
        out_specs=pl.BlockSpec((1, ns, kvl), lambda h: (h, 0, 0)),
        out_shape=jax.ShapeDtypeStruct((heads, ns, kvl), F32),
        compiler_params=_cparams(("parallel",)),
        name="absorb_q",
    )(qs, w_uk_h)


def _decode_attn_kernel(pt_ref, ql_ref, qr_ref, cn_ref, krn_ref, cc_hbm, ckr_hbm, o_ref, kc_buf, kr_buf, sem, *, chunk_pages, n_chunks, page):
    s = pl.program_id(0)
    n_seq = pl.num_programs(0)

    def copies(seq, chunk, slot):
        out = []
        for j in range(chunk_pages):
            pg = pt_ref[seq, chunk * chunk_pages + j]
            out.append(pltpu.make_async_copy(cc_hbm.at[pg], kc_buf.at[slot, pl.ds(j * page, page)], sem.at[slot, 0]))
            out.append(pltpu.make_async_copy(ckr_hbm.at[pg], kr_buf.at[slot, :, pl.ds(j * page, page)], sem.at[slot, 1]))
        return out

    def start(seq, chunk, slot):
        for c in copies(seq, chunk, slot):
            c.start()

    @pl.when(s == 0)
    def _():
        start(0, 0, 0)

    ql = ql_ref[0]
    qr = qr_ref[0]
    cn = cn_ref[0]
    krn = krn_ref[0]
    m = jnp.sum(ql * cn, axis=1, keepdims=True) + jnp.sum(qr * krn, axis=1, keepdims=True)
    l = jnp.ones_like(m)
    acc = jnp.broadcast_to(cn, ql.shape)
    for c in range(n_chunks):
        slot = c % 2
        if c + 1 < n_chunks:
            start(s, c + 1, 1 - slot)
        else:
            @pl.when(s + 1 < n_seq)
            def _():
                start(s + 1, 0, 1 - slot)
        for cp in copies(s, c, slot):
            cp.wait()
        kc = kc_buf[slot]
        kr_t = kr_buf[slot]
        sc = (lax.dot_general(ql, kc, _NT, preferred_element_type=F32)
              + jnp.dot(qr, kr_t, preferred_element_type=F32))
        m_new = jnp.maximum(m, jnp.max(sc, axis=1, keepdims=True))
        alpha = jnp.exp2(m - m_new)
        pr = jnp.exp2(sc - m_new)
        l = alpha * l + jnp.sum(pr, axis=1, keepdims=True)
        acc = alpha * acc + jnp.dot(pr, kc, preferred_element_type=F32)
        m = m_new
    o_ref[0] = acc / l


def _decode_attn(page_table, q_lat, q_rope, c_new, kr_new, cache_c, cache_kr_t):
    ns, heads, kvl = q_lat.shape
    rope = q_rope.shape[-1]
    n_pages = page_table.shape[1]
    page = cache_c.shape[1]
    assert cache_kr_t.shape[1:] == (rope, page)
    chunk_pages = _tile(n_pages, 16, 1)
    n_chunks = n_pages // chunk_pages
    assert n_chunks % 2 == 0 or n_chunks == 1
    keys = chunk_pages * page
    grid_spec = pltpu.PrefetchScalarGridSpec(
        num_scalar_prefetch=1,
        grid=(ns,),
        in_specs=[
            pl.BlockSpec((1, heads, kvl), lambda s, pt: (s, 0, 0)),
            pl.BlockSpec((1, heads, rope), lambda s, pt: (s, 0, 0)),
            pl.BlockSpec((1, 1, kvl), lambda s, pt: (s, 0, 0)),
            pl.BlockSpec((1, 1, rope), lambda s, pt: (s, 0, 0)),
            pl.BlockSpec(memory_space=pl.ANY),
            pl.BlockSpec(memory_space=pl.ANY),
        ],
        out_specs=pl.BlockSpec((1, heads, kvl), lambda s, pt: (s, 0, 0)),
        scratch_shapes=[
            pltpu.VMEM((2, keys, kvl), cache_c.dtype),
            pltpu.VMEM((2, rope, keys), cache_kr_t.dtype),
            pltpu.SemaphoreType.DMA((2, 2)),
        ],
    )
    return pl.pallas_call(
        functools.partial(_decode_attn_kernel, chunk_pages=chunk_pages, n_chunks=n_chunks, page=page),
        grid_spec=grid_spec,
        out_shape=jax.ShapeDtypeStruct((ns, heads, kvl), F32),
        compiler_params=_cparams(("arbitrary",)),
        name="decode_attn",
    )(page_table, q_lat, q_rope, c_new.reshape(ns, 1, kvl), kr_new.reshape(ns, 1, rope), cache_c, cache_kr_t)


def _unabsorb_kernel(o_ref, w_ref, gate_ref, y_ref):
    y = jnp.dot(o_ref[0].astype(BF16), w_ref[0], preferred_element_type=F32)
    y_ref[...] = (y * jax.nn.sigmoid(gate_ref[...].astype(F32))).astype(y_ref.dtype)


def _unabsorb_o(o_lat_h, w_uv_h, zb, gate_off):
    heads, ns, kvl = o_lat_h.shape
    vh = w_uv_h.shape[-1]
    assert gate_off % vh == 0
    return pl.pallas_call(
        _unabsorb_kernel,
        grid=(heads,),
        in_specs=[
            pl.BlockSpec((1, ns, kvl), lambda h: (h, 0, 0)),
            pl.BlockSpec((1, kvl, vh), lambda h: (h, 0, 0)),
            pl.BlockSpec((ns, vh), lambda h: (0, gate_off // vh + h)),
        ],
        out_specs=pl.BlockSpec((ns, vh), lambda h: (0, h)),
        out_shape=jax.ShapeDtypeStruct((ns, heads * vh), BF16),
        compiler_params=_cparams(("parallel",)),
        name="unabsorb_o",
    )(o_lat_h, w_uv_h, zb)


def _segment_mid(bc, s):
    c, dk = bc.shape
    if 2 * s >= 8:
        n = c // (2 * s)
        mid = bc.reshape(n, 2 * s, dk)[:, s - 1:s, :]
        return jnp.broadcast_to(mid, (n, 2 * s, dk)).reshape(c, dk)
    r = lax.broadcasted_iota(I32, bc.shape, 0) & (2 * s - 1)
    prev1 = pltpu.roll(bc, 1, 0)
    if s == 1:
        return jnp.where(r == 0, bc, prev1)
    assert s == 2
    return jnp.where(r == 0, pltpu.roll(bc, c - 1, 0), jnp.where(r == 1, bc, jnp.where(r == 2, prev1, pltpu.roll(bc, 2, 0))))


def _gla_prompt_kernel(q_ref, k_ref, v_ref, go_ref, gg_ref, lr_ref, wa_ref, ba_ref, gn_ref, y_ref, st_ref, st_scr, la_scr, *, n_chunks, c, scale):
    tb = pl.program_id(2)

    @pl.when(tb == 0)
    def _():
        st_scr[...] = jnp.zeros(st_scr.shape, F32)

    x = jnp.dot(lr_ref[...], wa_ref[...], preferred_element_type=F32, precision=HI) + ba_ref[...]
    la_scr[...] = _log_sigmoid(x) / GLA_TAU

    row = lax.broadcasted_iota(I32, (c, c), 0)
    col = lax.broadcasted_iota(I32, (c, c), 1)
    levels = []
    s = c // 2
    while s >= 1:
        levels.append(s)
        s //= 2
    tril = jnp.where(col <= row, 1.0, 0.0)
    masks = []
    for s in levels:
        sh = s.bit_length() - 1
        masks.append(((row >> (sh + 1)) == (col >> (sh + 1))) & (((row >> sh) & 1) == 1) & (((col >> sh) & 1) == 0))
    eye = row == col

    def chunk(ci, carry):
        r0 = pl.multiple_of(ci * c, c)
        q = q_ref[pl.ds(r0, c), :].astype(F32) * scale
        k = k_ref[pl.ds(r0, c), :].astype(F32)
        v = v_ref[pl.ds(r0, c), :]
        bc = jnp.dot(tril, la_scr[pl.ds(r0, c), :], preferred_element_type=F32, precision=HI)
        bl = bc[c - 1:c, :]
        att = jnp.where(eye, jnp.sum(q * k, axis=1, keepdims=True), 0.0)
        for li, s in enumerate(levels):
            ref = _segment_mid(bc, s)
            qh = (q * jnp.exp(jnp.minimum(bc - ref, 0.0))).astype(BF16)
            kh = (k * jnp.exp(jnp.minimum(ref - bc, 0.0))).astype(BF16)
            a = lax.dot_general(qh, kh, _NT, preferred_element_type=F32)
            att = att + jnp.where(masks[li], a, 0.0)
        st = st_scr[...]
        qt = (q * jnp.exp(bc)).astype(BF16)
        o = (lax.dot_general(qt, st.astype(BF16), _NT, preferred_element_type=F32)
             + jnp.dot(att.astype(BF16), v, preferred_element_type=F32))
        kb = (k * jnp.exp(bl - bc)).astype(BF16)
        st_scr[...] = st * jnp.exp(bl) + lax.dot_general(v, kb, _TN, preferred_element_type=F32)
        on = _rms(o, gn_ref[...])
        go = go_ref[pl.ds(r0, c), :].astype(F32)
        gg = gg_ref[pl.ds(r0, c), :].astype(F32)
        y_ref[pl.ds(r0, c), :] = (on * (go * jax.nn.sigmoid(go)) * jax.nn.sigmoid(gg)).astype(y_ref.dtype)
        return carry

    lax.fori_loop(0, n_chunks, chunk, 0, unroll=2 if n_chunks % 2 == 0 else 1)

    @pl.when(tb == pl.num_programs(2) - 1)
    def _():
        st_ref[0, 0] = st_scr[...].T


def _gla_prompt(zb, zs, offs, w_a2p, b_a2, g_gla, batch, seq, gh, dk, dv):
    n = batch * seq
    c = math.gcd(seq, GLA_CHUNK)
    tb = _tile(seq, 512, c)
    nb = seq // tb
    for key, w in (("gq", dk), ("gk", dk), ("gv", dv), ("g_out", dv), ("gate_gla", dv)):
        assert offs[key] % w == 0
    assert offs["g_lr"] % LANES == 0

    def rows(b, h, t):
        return b * nb + t

    in_specs = [
        pl.BlockSpec((tb, dk), lambda b, h, t: (rows(b, h, t), offs["gq"] // dk + h)),
        pl.BlockSpec((tb, dk), lambda b, h, t: (rows(b, h, t), offs["gk"] // dk + h)),
        pl.BlockSpec((tb, dv), lambda b, h, t: (rows(b, h, t), offs["gv"] // dv + h)),
        pl.BlockSpec((tb, dv), lambda b, h, t: (rows(b, h, t), offs["g_out"] // dv + h)),
        pl.BlockSpec((tb, dv), lambda b, h, t: (rows(b, h, t), offs["gate_gla"] // dv + h)),
        pl.BlockSpec((tb, LANES), lambda b, h, t: (rows(b, h, t), offs["g_lr"] // LANES)),
        pl.BlockSpec((LANES, dk), lambda b, h, t: (0, h)),
        pl.BlockSpec((1, dk), lambda b, h, t: (0, h)),
        pl.BlockSpec((1, dv), lambda b, h, t: (0, 0)),
    ]
    out_specs = [
        pl.BlockSpec((tb, dv), lambda b, h, t: (rows(b, h, t), h)),
        pl.BlockSpec((1, 1, dk, dv), lambda b, h, t: (b, h, 0, 0)),
    ]
    return pl.pallas_call(
        functools.partial(_gla_prompt_kernel, n_chunks=tb // c, c=c, scale=dk ** -0.5),
        grid=(batch, gh, nb),
        in_specs=in_specs,
        out_specs=out_specs,
        out_shape=[jax.ShapeDtypeStruct((n, gh * dv), BF16), jax.ShapeDtypeStruct((batch, gh, dk, dv), F32)],
        scratch_shapes=[pltpu.VMEM((dv, dk), F32), pltpu.VMEM((tb, dk), F32)],
        compiler_params=_cparams(("parallel", "parallel", "arbitrary")),
        name="gla_prompt",
    )(zb, zb, zb, zb, zb, zs, w_a2p, b_a2.reshape(1, gh * dk), g_gla.reshape(1, dv))


def _gla_decode_kernel(qk_ref, lrt_ref, wat_ref, bcol_ref, v_ref, go_ref, gg_ref, gn_ref, s0_ref, y_ref, s1_ref, *, sb, scale):
    x = jnp.dot(wat_ref[0], lrt_ref[0], preferred_element_type=F32, precision=HI) + bcol_ref[0]
    a = jnp.exp(_log_sigmoid(x) / GLA_TAU)
    outs = []
    for u in range(sb):
        qc = qk_ref[0, 0, :, u:u + 1] * scale
        kc = qk_ref[0, 0, :, sb + u:sb + u + 1]
        sn = a[:, u:u + 1] * s0_ref[u, 0] + kc * v_ref[u:u + 1, :]
        s1_ref[u, 0] = sn
        outs.append(jnp.sum(qc * sn, axis=0, keepdims=True))
    o = jnp.concatenate(outs, axis=0)
    go = go_ref[...]
    y_ref[...] = _rms(o, gn_ref[...]) * (go * jax.nn.sigmoid(go)) * jax.nn.sigmoid(gg_ref[...])


def _gla_decode(qk_cols, lr_t, wa_t, b_col, gv, g_out, gate_gla, g_gla, state):
    ns, gh, dk, dv = state.shape
    sb = qk_cols.shape[-1] // 2
    return pl.pallas_call(
        functools.partial(_gla_decode_kernel, sb=sb, scale=dk ** -0.5),
        grid=(ns // sb, gh),
        in_specs=[
            pl.BlockSpec((1, 1, dk, 2 * sb), lambda i, h: (i, h, 0, 0)),
            pl.BlockSpec((1, LANES, sb), lambda i, h: (i, 0, 0)),
            pl.BlockSpec((1, dk, LANES), lambda i, h: (h, 0, 0)),
            pl.BlockSpec((1, dk, 1), lambda i, h: (h, 0, 0)),
            pl.BlockSpec((sb, dv), lambda i, h: (i, h)),
            pl.BlockSpec((sb, dv), lambda i, h: (i, h)),
            pl.BlockSpec((sb, dv), lambda i, h: (i, h)),
            pl.BlockSpec((1, dv), lambda i, h: (0, 0)),
            pl.BlockSpec((sb, 1, dk, dv), lambda i, h: (i, h, 0, 0)),
        ],
        out_specs=[pl.BlockSpec((sb, dv), lambda i, h: (i, h)), pl.BlockSpec((sb, 1, dk, dv), lambda i, h: (i, h, 0, 0))],
        out_shape=[jax.ShapeDtypeStruct((ns, gh * dv), F32), jax.ShapeDtypeStruct((ns, gh, dk, dv), F32)],
        compiler_params=_cparams(("parallel", "parallel")),
        name="gla_decode",
    )(qk_cols, lr_t, wa_t, b_col, gv, g_out, gate_gla, g_gla.reshape(1, dv), state)


def _outproj_kernel(ym_ref, yg_ref, h_ref, w_ref, g_ref, wr_ref, br_ref, h1_ref, m_ref, route_ref):
    mix = (ym_ref[...].astype(F32) + yg_ref[...].astype(F32)).astype(BF16)
    h1 = h_ref[...] + jnp.dot(mix, w_ref[...], preferred_element_type=F32)
    h1_ref[...] = h1
    m = _rms(h1, g_ref[...])
    half = m.shape[1] // 2
    m_ref[...] = _pack_bf16_pair(m[:, :half], m[:, half:])
    m_hi = m.astype(BF16)
    m_lo = (m - m_hi.astype(F32)).astype(BF16)
    wr = wr_ref[...]
    w_hi = wr.astype(BF16)
    w_lo = (wr - w_hi.astype(F32)).astype(BF16)
    logits = (jnp.dot(m_hi, w_hi, preferred_element_type=F32) + jnp.dot(m_lo, w_hi, preferred_element_type=F32)
              + jnp.dot(m_hi, w_lo, preferred_element_type=F32) + br_ref[...])
    lane = lax.broadcasted_iota(I32, logits.shape, 1).astype(F32)
    vals, idxs = [], []
    for _ in range(TOP_K):
        mx = jnp.max(logits, axis=1, keepdims=True)
        ix = jnp.min(jnp.where(logits == mx, lane, float(LANES)), axis=1, keepdims=True)
        vals.append(mx)
        idxs.append(ix)
        logits = jnp.where(lane == ix, -jnp.inf, logits)
    ex = [jnp.exp(v - vals[0]) for v in vals]
    den = ex[0]
    for e in ex[1:]:
        den = den + e
    route = jnp.zeros(logits.shape, F32)
    for kk in range(TOP_K):
        route = jnp.where(lane == float(kk), idxs[kk], route)
        route = jnp.where(lane == float(TOP_K + kk), ex[kk] / den, route)
    route_ref[...] = route


def _outproj(y_mla, y_gla, h, w_out_b, g_ffn, w_router_p, b_router_p):
    n, d = h.shape
    tm = _tile(n, 256, 16)
    row = lambda i: (i, 0)
    fixed = lambda i: (0, 0)
    return pl.pallas_call(
        _outproj_kernel,
        grid=(n // tm,),
        in_specs=[
            pl.BlockSpec((tm, d), row), pl.BlockSpec((tm, d), row), pl.BlockSpec((tm, d), row),
            pl.BlockSpec((d, d), fixed), pl.BlockSpec((1, d), fixed),
            pl.BlockSpec((d, LANES), fixed), pl.BlockSpec((1, LANES), fixed),
        ],
        out_specs=[pl.BlockSpec((tm, d), row), pl.BlockSpec((tm, d // 2), row), pl.BlockSpec((tm, LANES), row)],
        out_shape=[jax.ShapeDtypeStruct((n, d), F32), jax.ShapeDtypeStruct((n, d // 2), U32), jax.ShapeDtypeStruct((n, LANES), F32)],
        compiler_params=_cparams(("parallel",)),
        name="outproj",
    )(y_mla, y_gla, h, w_out_b, g_ffn.reshape(1, d), w_router_p, b_router_p)


def _dispatch_kernel(idx_ref, x_ref, o_ref, g_scr, *, rows):
    base = pl.program_id(0) * rows

    def body(r, carry):
        tok = idx_ref[base + r]
        g_scr[pl.ds(r, 1), :] = x_ref[pl.ds(tok, 1), :]
        return carry

    lax.fori_loop(0, rows, body, 0, unroll=8)
    hi, lo = _unpack_bf16_pair(g_scr[...])
    o_ref[...] = jnp.concatenate([hi, lo], axis=1).astype(BF16)


def _dispatch(buf_tok, m_packed):
    rows_total = buf_tok.shape[0]
    n, w = m_packed.shape
    rows = _tile(rows_total, 512, 16)
    grid_spec = pltpu.PrefetchScalarGridSpec(
        num_scalar_prefetch=1,
        grid=(rows_total // rows,),
        in_specs=[pl.BlockSpec(memory_space=pltpu.VMEM)],
        out_specs=pl.BlockSpec((rows, 2 * w), lambda i, idx: (i, 0)),
        scratch_shapes=[pltpu.VMEM((rows, w), U32)],
    )
    return pl.pallas_call(
        functools.partial(_dispatch_kernel, rows=rows),
        grid_spec=grid_spec,
        out_shape=jax.ShapeDtypeStruct((rows_total, 2 * w), BF16),
        compiler_params=_cparams(("arbitrary",), 56),
        name="moe_dispatch",
    )(buf_tok, m_packed)


def _for_row_chunks(b0, b1, rb, max_blocks, fn):
    n = b1 - b0
    size = max_blocks
    while size >= 1:
        start = b0 + (n & ~(2 * size - 1))

        @pl.when((n & size) != 0)
        def _(start=start, size=size):
            fn(pl.multiple_of(start * rb, rb), size * rb)

        size //= 2


def _moe_kernel(vt_ref, ve_ref, b0_ref, b1_ref, zf_ref, xs_ref, wg_ref, wu_ref, bg_ref, bu_ref, wd_ref, bd_ref, ys_ref,
                a_scr, *, n_up, n_down, tf, rb):
    v = pl.program_id(0)
    j = pl.program_id(1)
    b0 = b0_ref[v]
    b1 = b1_ref[v]
    fill = zf_ref[v] == 1
    max_blocks = xs_ref.shape[0] // rb

    @pl.when(fill & (j == 0))
    def _():
        def blk(b, carry):
            r0 = pl.multiple_of(b * rb, rb)
            for nn in range(n_down):
                ys_ref[nn, 0, pl.ds(r0, rb), :] = jnp.zeros((rb, ys_ref.shape[3]), U32)
            return carry
        lax.fori_loop(b0, b1, blk, 0)

    @pl.when(jnp.logical_not(fill) & (j < n_up))
    def _():
        def up_proj(r0, rows):
            x = xs_ref[pl.ds(r0, rows), :]
            g = jnp.dot(x, wg_ref[0].astype(BF16), preferred_element_type=F32) + bg_ref[0]
            u = jnp.dot(x, wu_ref[0].astype(BF16), preferred_element_type=F32) + bu_ref[0]
            gate = jnp.minimum(g, SWIGLU_LIMIT)
            up = jnp.clip(u, -SWIGLU_LIMIT, SWIGLU_LIMIT)
            glu = gate * jax.nn.sigmoid(gate * SWIGLU_ALPHA)
            a_scr[j, pl.ds(r0, rows), :] = ((up + 1.0) * glu).astype(BF16)
        _for_row_chunks(b0, b1, rb, max_blocks, up_proj)

    @pl.when(jnp.logical_not(fill) & (j >= n_up))
    def _():
        nn = j - n_up

        def down_proj(r0, rows):
            y = bd_ref[0]
            for f in range(n_up):
                w = wd_ref[0, f * tf:(f + 1) * tf, :].astype(BF16)
                y = y + jnp.dot(a_scr[f, pl.ds(r0, rows), :], w, preferred_element_type=F32)
            half = y.shape[1] // 2
            ys_ref[nn, 0, pl.ds(r0, rows), :] = _pack_bf16_pair(y[:, :half], y[:, half:])
        _for_row_chunks(b0, b1, rb, max_blocks, down_proj)


def _moe_experts(meta, xs, w_gu, b_gu, w_down, b_down, tile_rows):
    v_tile, v_exp, v_b0, v_b1, v_zero = meta
    rows_total, d = xs.shape
    e, _, f2 = w_gu.shape
    ff = f2 // 2
    tf = _tile(ff, 256, LANES)
    tn = _tile(d, 256, LANES)
    n_up, n_down = ff // tf, d // tn
    n_tiles = rows_total // tile_rows
    n_visits = v_tile.shape[0]
    up_idx = lambda j, zf: jnp.where(zf == 1, n_up - 1, jnp.minimum(j, n_up - 1))
    dn_idx = lambda j, zf: jnp.where(zf == 1, n_down - 1, jnp.maximum(j - n_up, 0))
    grid_spec = pltpu.PrefetchScalarGridSpec(
        num_scalar_prefetch=5,
        grid=(n_visits, n_up + n_down),
        in_specs=[
            pl.BlockSpec((tile_rows, d), lambda v, j, vt, ve, b0, b1, zf: (vt[v], 0)),
            pl.BlockSpec((1, d, tf), lambda v, j, vt, ve, b0, b1, zf: (ve[v], 0, up_idx(j, zf[v]))),
            pl.BlockSpec((1, d, tf), lambda v, j, vt, ve, b0, b1, zf: (ve[v], 0, n_up + up_idx(j, zf[v]))),
            pl.BlockSpec((1, 1, tf), lambda v, j, vt, ve, b0, b1, zf: (ve[v], 0, up_idx(j, zf[v]))),
            pl.BlockSpec((1, 1, tf), lambda v, j, vt, ve, b0, b1, zf: (ve[v], 0, n_up + up_idx(j, zf[v]))),
            pl.BlockSpec((1, ff, tn), lambda v, j, vt, ve, b0, b1, zf: (ve[v], 0, dn_idx(j, zf[v]))),
            pl.BlockSpec((1, 1, tn), lambda v, j, vt, ve, b0, b1, zf: (ve[v], 0, dn_idx(j, zf[v]))),
        ],
        out_specs=pl.BlockSpec((n_down, 1, tile_rows, tn // 2), lambda v, j, vt, ve, b0, b1, zf: (0, vt[v], 0, 0)),
        scratch_shapes=[pltpu.VMEM((n_up, tile_rows, tf), BF16)],
    )
    return pl.pallas_call(
        functools.partial(_moe_kernel, n_up=n_up, n_down=n_down, tf=tf, rb=EXPERT_ROW_BLOCK),
        grid_spec=grid_spec,
        out_shape=jax.ShapeDtypeStruct((n_down, n_tiles, tile_rows, tn // 2), U32),
        compiler_params=_cparams(("arbitrary", "arbitrary"), 56),
        name="moe_experts",
    )(v_tile, v_exp, v_b0, v_b1, v_zero, xs, w_gu, w_gu, b_gu.reshape(e, 1, f2), b_gu.reshape(e, 1, f2), w_down, b_down.reshape(e, 1, d))


def _combine_kernel(dest_ref, ys_ref, route_ref, o_ref, stage, *, tm):
    base = pl.program_id(1) * tm

    def body(t, carry):
        for kk in range(TOP_K):
            r = dest_ref[(base + t) * TOP_K + kk]
            stage[kk, pl.ds(t, 1), :] = ys_ref[0, pl.ds(r, 1), :]
        return carry

    lax.fori_loop(0, tm, body, 0, unroll=4)
    acc_hi = acc_lo = None
    for kk in range(TOP_K):
        g = route_ref[:, TOP_K + kk:TOP_K + kk + 1]
        hi, lo = _unpack_bf16_pair(stage[kk])
        acc_hi = g * hi if acc_hi is None else acc_hi + g * hi
        acc_lo = g * lo if acc_lo is None else acc_lo + g * lo
    o_ref[...] = jnp.concatenate([acc_hi, acc_lo], axis=1)


def _combine(dest, ys, route):
    n_down, rows_total, wh = ys.shape
    n = route.shape[0]
    tm = _tile(n, 256, 8)
    grid_spec = pltpu.PrefetchScalarGridSpec(
        num_scalar_prefetch=1,
        grid=(n_down, n // tm),
        in_specs=[
            pl.BlockSpec((1, rows_total, wh), lambda c, i, dest: (c, 0, 0)),
            pl.BlockSpec((tm, LANES), lambda c, i, dest: (i, 0)),
        ],
        out_specs=pl.BlockSpec((tm, 2 * wh), lambda c, i, dest: (i, c)),
        scratch_shapes=[pltpu.VMEM((TOP_K, tm, wh), U32)],
    )
    return pl.pallas_call(
        functools.partial(_combine_kernel, tm=tm),
        grid_spec=grid_spec,
        out_shape=jax.ShapeDtypeStruct((n, n_down * 2 * wh), F32),
        compiler_params=_cparams(("arbitrary", "arbitrary"), 56),
        name="moe_combine",
    )(dest, ys, route)


def _route_meta(top_i, n_experts, tile_rows):
    n_tok = top_i.shape[0]
    n_asg = n_tok * TOP_K
    rb = EXPERT_ROW_BLOCK
    e_flat = top_i.reshape(n_asg)
    tok_flat = (jnp.arange(n_asg, dtype=I32) // TOP_K).astype(I32)
    onehot = jax.nn.one_hot(e_flat, n_experts, dtype=I32)
    rank = jnp.take_along_axis(jnp.cumsum(onehot, axis=0), e_flat[:, None], axis=1)[:, 0] - 1
    counts = onehot.sum(axis=0)
    padded = (counts + rb - 1) // rb * rb
    ends = jnp.cumsum(padded)
    starts = ends - padded
    dest = (starts[e_flat] + rank).astype(I32)
    rows_total = (-(-n_asg // rb) + n_experts) * rb
    rows_total = -(-rows_total // tile_rows) * tile_rows
    buf_tok = jnp.zeros((rows_total,), I32).at[dest].set(tok_flat)
    n_tiles = rows_total // tile_rows
    total = ends[-1].astype(I32)
    bp = jnp.sort(jnp.concatenate([jnp.arange(n_tiles, dtype=I32) * tile_rows, starts.astype(I32), total[None]]))
    lo = bp
    length = jnp.concatenate([bp[1:], jnp.full((1,), rows_total, I32)]) - lo
    valid = length > 0
    order = jnp.argsort(jnp.logical_not(valid), stable=True)
    lo, length = lo[order], length[order]
    n_valid = jnp.sum(valid.astype(I32))
    pos = jnp.arange(lo.shape[0], dtype=I32)
    src = jnp.minimum(pos, n_valid - 1)
    lo_s = lo[src]
    live = pos < n_valid
    v_tile = (lo_s // tile_rows).astype(I32)
    v_exp = jnp.minimum(jnp.searchsorted(ends, jnp.minimum(lo_s, total - 1), side="right"), n_experts - 1).astype(I32)
    v_b0 = jnp.where(live, (lo_s % tile_rows) // rb, 0).astype(I32)
    v_b1 = jnp.where(live, v_b0 + length[src] // rb, 0).astype(I32)
    v_zero = ((lo_s >= total) | jnp.logical_not(live)).astype(I32)
    return buf_tok, dest, (v_tile, v_exp, v_b0, v_b1, v_zero)


def _ple_kernel(h1_ref, moe_ref, p_ref, wp_ref, wg_ref, gp_ref, gf_ref, y_ref, *, final):
    h2 = h1_ref[...] + moe_ref[...]
    u = _rms(h2, gp_ref[...]).astype(BF16)
    gate = jax.nn.sigmoid(jnp.dot(u, wg_ref[...], preferred_element_type=F32))
    pe = jnp.dot(p_ref[...].astype(BF16), wp_ref[...], preferred_element_type=F32)
    y = h2 + pe * gate
    y_ref[...] = _rms(y, gf_ref[...]) if final else y


def _ple_final(h1, moe, row_off, p, w_ple_b, w_ple_gate_b, g_ple, g_final, final):
    n, d = h1.shape
    pd = p.shape[1]
    tm = _tile(n, 256, 8)
    assert row_off % tm == 0
    row = lambda i: (i, 0)
    fixed = lambda i: (0, 0)
    return pl.pallas_call(
        functools.partial(_ple_kernel, final=final),
        grid=(n // tm,),
        in_specs=[
            pl.BlockSpec((tm, d), row), pl.BlockSpec((tm, d), lambda i: (row_off // tm + i, 0)), pl.BlockSpec((tm, pd), row),
            pl.BlockSpec((pd, d), fixed), pl.BlockSpec((d, d), fixed), pl.BlockSpec((1, d), fixed), pl.BlockSpec((1, d), fixed),
        ],
        out_specs=pl.BlockSpec((tm, d), row),
        out_shape=jax.ShapeDtypeStruct((n, d), F32),
        compiler_params=_cparams(("parallel",)),
        name="ple_final",
    )(h1, moe, p, w_ple_b, w_ple_gate_b, g_ple.reshape(1, d), g_final.reshape(1, d))


def _rope_table(pos, rope):
    half = rope // 2
    inv_freq = jnp.power(ROPE_THETA, -jnp.arange(half, dtype=F32) / half)
    ang = pos.astype(F32)[:, None] * inv_freq[None, :]
    cos, sin = jnp.cos(ang), jnp.sin(ang)
    return jnp.concatenate([cos, cos, -sin, sin], axis=1)


def _swap_halves(w):
    half = w.shape[-1] // 2
    return jnp.concatenate([w[..., half:], w[..., :half]], axis=-1)


def kernel(x_prompt, x_sample, cache_kv_latent, cache_k_rope, state_gla, page_table, p_prompt, p_sample, g_mix, w_in, g_q_a, w_uq, g_kv_a, w_uk, w_uv, w_a2, b_a2, g_gla, w_out, g_ffn, w_router, b_router, w_gu, b_gu, w_down, b_down, g_ple, w_ple, w_ple_gate, g_final):
    depth = w_in.shape[0]
    batch, seq, d = x_prompt.shape
    ns, dec_seq, _ = x_sample.shape
    assert dec_seq == 1
    ql, kvl = g_q_a.shape[-1], g_kv_a.shape[-1]
    rope = cache_k_rope.shape[-1]
    heads, nope = w_uk.shape[2], w_uk.shape[3]
    vh = w_uv.shape[3]
    _, _, gh, dk, dv = state_gla.shape
    lr = w_a2.shape[1]
    n_experts = w_router.shape[-1]
    past = page_table.shape[1] * cache_kv_latent.shape[2]
    assert 2 * rope == LANES and nope == LANES and heads * vh == d and lr <= LANES and n_experts <= LANES
    scale = LOG2E / math.sqrt(nope + rope)
    n_p = batch * seq
    tile_rows = 1024

    big_segs = (("gv", gh * dv), ("g_out", gh * dv), ("gate_mla", d), ("gate_gla", d), ("gq", gh * dk), ("gk", gh * dk))
    offs, o = {}, 0
    for key, w in big_segs:
        offs[key] = o
        o += w
    offs["g_lr"] = ql + kvl + 2 * rope
    in_splits = (ql, kvl, rope, gh * dk, gh * dk, gh * dv, gh * dv, lr, d, d)
    in_offsets = tuple(sum(in_splits[:i + 1]) for i in range(len(in_splits) - 1))

    tab_p = jnp.tile(_rope_table(jnp.arange(seq), rope), (batch, 1))
    tab_s = _rope_table(jnp.full((ns,), past, I32), rope)

    h_p = x_prompt.reshape(n_p, d)
    h_s = x_sample.reshape(ns, d)
    kv_p, kr_p, st_p, kv_s, kr_s, st_s = [], [], [], [], [], []
    for l in range(depth):
        seg = dict(zip(("c_q", "c_kv", "k_r", "gq", "gk", "gv", "g_out", "g_lr", "gate_mla", "gate_gla"), jnp.split(w_in[l], in_offsets, axis=1)))
        w_big = jnp.concatenate([seg[k] for k, _ in big_segs], axis=1).astype(BF16)
        w_small = jnp.concatenate([seg["c_q"], seg["c_kv"], seg["k_r"], _swap_halves(seg["k_r"]),
                                   jnp.pad(seg["g_lr"], ((0, 0), (0, LANES - lr)))], axis=1).astype(BF16)
        wq = w_uq[l].reshape(ql, heads, nope + rope)
        w_uq_p = jnp.concatenate([wq[..., :nope], wq[..., nope:], _swap_halves(wq[..., nope:])], axis=-1).transpose(1, 0, 2).astype(BF16)
        w_uk_h = w_uk[l].transpose(1, 0, 2).astype(BF16)
        w_uv_h = w_uv[l].transpose(1, 0, 2).astype(BF16)
        w_uv_t = w_uv[l].transpose(1, 2, 0).astype(BF16)
        w_a2p = jnp.pad(w_a2[l], ((0, LANES - lr), (0, 0)))
        w_out_b = w_out[l].astype(BF16)
        w_router_p = jnp.pad(w_router[l], ((0, 0), (0, LANES - n_experts)))
        b_router_p = jnp.pad(b_router[l], (0, LANES - n_experts), constant_values=NEG_BIG).reshape(1, LANES)
        w_ple_b = w_ple[l].astype(BF16)
        w_ple_gate_b = w_ple_gate[l].astype(BF16)

        def front(h, tab):
            a = _rms_cast(h, g_mix[l])
            zb = _matmul(a, w_big, BF16, "inproj_big")
            zs = _matmul(a, w_small, F32, "inproj_small")
            q = _q_proj(zs, g_q_a[l], w_uq_p, tab, scale)
            return zb, zs, q

        zb, zs, q = front(h_p, tab_p)
        c_kv, k_r, k_heads, v_t = _kv_prep(zs, g_kv_a[l], tab_p, w_uk_h, w_uv_t, ql, rope, True)
        y_mla = _flash_prompt(q, k_heads, v_t, zb, offs["gate_mla"], batch, seq)
        y_gla, s_new = _gla_prompt(zb, zs, offs, w_a2p, b_a2[l], g_gla[l], batch, seq, gh, dk, dv)
        h1_p, m_p, route_p = _outproj(y_mla, y_gla, h_p, w_out_b, g_ffn[l], w_router_p, b_router_p)
        kv_p.append(c_kv.reshape(batch, seq, kvl)); kr_p.append(k_r.reshape(batch, seq, rope)); st_p.append(s_new)

        zb, zs, q = front(h_s, tab_s)
        c_kv, k_r = _kv_prep(zs, g_kv_a[l], tab_s, w_uk_h, w_uv_t, ql, rope, False)
        q_lat = _absorb_q(q, w_uk_h).transpose(1, 0, 2)
        q_rope = q[:, :, nope:nope + rope].astype(F32).transpose(1, 0, 2)
        o_lat = _decode_attn(page_table, q_lat, q_rope, c_kv, k_r, cache_kv_latent[l], cache_k_rope[l].transpose(0, 2, 1))
        y_mla = _unabsorb_o(o_lat.transpose(1, 0, 2), w_uv_h, zb, offs["gate_mla"])
        sb = 8
        zf = zb.astype(F32)
        cols = lambda key, w: zf[:, offs[key]:offs[key] + w]
        to_cols = lambda z: z.reshape(ns // sb, sb, gh, dk).transpose(0, 2, 3, 1)
        qk_cols = jnp.concatenate([to_cols(cols("gq", gh * dk)), to_cols(cols("gk", gh * dk))], axis=-1)
        lr_t = zs[:, offs["g_lr"]:offs["g_lr"] + LANES].reshape(ns // sb, sb, LANES).transpose(0, 2, 1)
        wa_t = w_a2p.reshape(LANES, gh, dk).transpose(1, 2, 0)
        b_col = b_a2[l].reshape(gh, dk, 1)
        y_gla, s_new = _gla_decode(qk_cols, lr_t, wa_t, b_col, cols("gv", gh * dv), cols("g_out", gh * dv),
                                   cols("gate_gla", gh * dv), g_gla[l], state_gla[l])
        h1_s, m_s, route_s = _outproj(y_mla, y_gla, h_s, w_out_b, g_ffn[l], w_router_p, b_router_p)
        kv_s.append(c_kv.reshape(ns, 1, kvl)); kr_s.append(k_r.reshape(ns, 1, rope)); st_s.append(s_new)

        m_all = jnp.concatenate([m_p, m_s], axis=0)
        route = jnp.concatenate([route_p, route_s], axis=0)
        top_i = route[:, :TOP_K].astype(I32)
        buf_tok, dest, meta = _route_meta(top_i, n_experts, tile_rows)
        xs = _dispatch(buf_tok, m_all)
        ys = _moe_experts(meta, xs, w_gu[l], b_gu[l], w_down[l], b_down[l], tile_rows)
        ys = ys.reshape(ys.shape[0], ys.shape[1] * ys.shape[2], ys.shape[3])
        moe = _combine(dest, ys, route)

        last = l == depth - 1
        h_p = _ple_final(h1_p, moe, 0, p_prompt[l].reshape(n_p, -1), w_ple_b, w_ple_gate_b, g_ple[l], g_final, last)
        h_s = _ple_final(h1_s, moe, n_p, p_sample[l].reshape(ns, -1), w_ple_b, w_ple_gate_b, g_ple[l], g_final, last)

    return (h_p.reshape(batch, seq, d), h_s.reshape(ns, 1, d), jnp.stack(kv_p), jnp.stack(kr_p), jnp.stack(st_p),
            jnp.stack(kv_s), jnp.stack(kr_s), jnp.stack(st_s))
```

```python
import functools
import math

import jax
import jax.numpy as jnp
from jax import lax
from jax.experimental import pallas as pl
from jax.experimental.pallas import tpu as pltpu

F32 = jnp.float32
BF16 = jnp.bfloat16
U32 = jnp.uint32
I32 = jnp.int32

EPS = 1e-6
ROPE_THETA = 10000.0
GLA_TAU = 16.0
GLA_CHUNK = 64
TOP_K = 4
SWIGLU_LIMIT = 7.0
SWIGLU_ALPHA = 1.702
EXPERT_ROW_BLOCK = 128
MOE_SLAB = 512
LANES = 128
MIB = 1024 * 1024
SUM_ROWS = 16
LOG2E = 1.4426950408889634
HI = lax.Precision.HIGHEST
NEG_BIG = -1e30

_NT = (((1,), (1,)), ((), ()))
_TN = (((0,), (0,)), ((), ()))


def _cparams(sem, vmem_mib=48):
    return pltpu.CompilerParams(dimension_semantics=sem, vmem_limit_bytes=vmem_mib * MIB)


def _tile(n, cap, mult):
    best = None
    for t in range(mult, min(n, cap) + 1, mult):
        if n % t == 0:
            best = t
    assert best is not None, (n, cap, mult)
    return best


def _rms(x, g):
    return x * lax.rsqrt(jnp.mean(x * x, axis=-1, keepdims=True) + EPS) * g


def _log_sigmoid(x):
    return jnp.minimum(x, 0.0) - jnp.log1p(jnp.exp(-jnp.abs(x)))


def _pack_bf16_pair(hi, lo):
    hb = pltpu.bitcast(hi.astype(BF16).astype(F32), U32)
    lb = pltpu.bitcast(lo.astype(BF16).astype(F32), U32)
    return hb | (lb >> 16)


def _unpack_bf16_pair(w):
    hi = pltpu.bitcast(w & jnp.uint32(0xFFFF0000), F32)
    lo = pltpu.bitcast(w << 16, F32)
    return hi, lo


def _rms_cast_kernel(x_ref, g_ref, o_ref):
    o_ref[...] = _rms(x_ref[...], g_ref[...]).astype(o_ref.dtype)


def _rms_cast(x, g):
    n, d = x.shape
    tm = _tile(n, 512, 16)
    return pl.pallas_call(
        _rms_cast_kernel,
        grid=(n // tm,),
        in_specs=[pl.BlockSpec((tm, d), lambda i: (i, 0)), pl.BlockSpec((1, d), lambda i: (0, 0))],
        out_specs=pl.BlockSpec((tm, d), lambda i: (i, 0)),
        out_shape=jax.ShapeDtypeStruct((n, d), BF16),
        compiler_params=_cparams(("parallel",)),
        name="rms_cast",
    )(x, g.reshape(1, d))


def _mm_kernel(x_ref, w_ref, o_ref):
    o_ref[...] = jnp.dot(x_ref[...], w_ref[...], preferred_element_type=F32).astype(o_ref.dtype)


def _matmul(x, w, out_dtype, name):
    m, k = x.shape
    n = w.shape[1]
    tm = _tile(m, 1024, 16)
    tn = _tile(n, 1280, LANES)
    return pl.pallas_call(
        _mm_kernel,
        grid=(n // tn, m // tm),
        in_specs=[pl.BlockSpec((tm, k), lambda j, i: (i, 0)), pl.BlockSpec((k, tn), lambda j, i: (0, j))],
        out_specs=pl.BlockSpec((tm, tn), lambda j, i: (i, j)),
        out_shape=jax.ShapeDtypeStruct((m, n), out_dtype),
        compiler_params=_cparams(("parallel", "parallel")),
        name=name,
    )(x, w)


def _rope_combine(y2, tab):
    half = y2.shape[1] // 2
    t = y2 * tab
    r = t + pltpu.roll(t, half, axis=1)
    lane = lax.broadcasted_iota(I32, r.shape, 1)
    return jnp.where(lane < half, r, 0.0)


def _q_proj_kernel(cq_ref, g_ref, w_ref, tab_ref, q_ref, *, heads, nope, scale):
    cqn = _rms(cq_ref[...], g_ref[...]).astype(BF16)
    tab = tab_ref[...]
    for h in range(heads):
        y = jnp.dot(cqn, w_ref[h], preferred_element_type=F32)
        q_ref[h, :, :nope] = (y[:, :nope] * scale).astype(BF16)
        q_ref[h, :, nope:] = (_rope_combine(y[:, nope:], tab) * scale).astype(BF16)


def _q_proj(zs, g_q_a, w_uq_p, tab, scale):
    n = zs.shape[0]
    heads, ql, width = w_uq_p.shape
    tm = _tile(n, 512, 16)
    return pl.pallas_call(
        functools.partial(_q_proj_kernel, heads=heads, nope=width // 2, scale=scale),
        grid=(n // tm,),
        in_specs=[
            pl.BlockSpec((tm, ql), lambda i: (i, 0)),
            pl.BlockSpec((1, ql), lambda i: (0, 0)),
            pl.BlockSpec((heads, ql, width), lambda i: (0, 0, 0)),
            pl.BlockSpec((tm, LANES), lambda i: (i, 0)),
        ],
        out_specs=pl.BlockSpec((heads, tm, width), lambda i: (0, i, 0)),
        out_shape=jax.ShapeDtypeStruct((heads, n, width), BF16),
        compiler_params=_cparams(("parallel",)),
        name="q_proj",
    )(zs, g_q_a.reshape(1, ql), w_uq_p, tab)


def _kv_prep_kernel(ckv_ref, krs_ref, g_ref, tab_ref, wuk_ref, wuvt_ref, c_ref, kr_ref, k_ref, vt_ref, *, heads, nope, rope, vh):
    c = _rms(ckv_ref[...], g_ref[...])
    c_ref[...] = c
    cb = c.astype(BF16)
    r = _rope_combine(krs_ref[...], tab_ref[...])
    kr_ref[...] = r[:, :rope]
    rb = r.astype(BF16)
    kn = jnp.dot(cb, wuk_ref[...], preferred_element_type=F32).astype(BF16)
    vt = lax.dot_general(wuvt_ref[...], cb, _NT, preferred_element_type=F32).astype(BF16)
    for h in range(heads):
        k_ref[h, :, :nope] = kn[:, h * nope:(h + 1) * nope]
        k_ref[h, :, nope:] = rb
        vt_ref[h, :vh, :] = vt[h * vh:(h + 1) * vh, :]
        vt_ref[h, vh:, :] = jnp.ones((SUM_ROWS, cb.shape[0]), BF16)


def _latent_kernel(ckv_ref, krs_ref, g_ref, tab_ref, c_ref, kr_ref, *, rope):
    c_ref[...] = _rms(ckv_ref[...], g_ref[...])
    kr_ref[...] = _rope_combine(krs_ref[...], tab_ref[...])[:, :rope]


def _kv_prep(zs, g_kv_a, tab, w_uk_f, w_uv_tf, heads, ql, rope, with_kv):
    n = zs.shape[0]
    kvl = w_uk_f.shape[0]
    nope = w_uk_f.shape[1] // heads
    vh = w_uv_tf.shape[0] // heads
    assert ql % kvl == 0 and (ql + kvl) % LANES == 0
    tm = _tile(n, 640, LANES if with_kv else 8)
    in_specs = [
        pl.BlockSpec((tm, kvl), lambda i: (i, ql // kvl)),
        pl.BlockSpec((tm, LANES), lambda i: (i, (ql + kvl) // LANES)),
        pl.BlockSpec((1, kvl), lambda i: (0, 0)),
        pl.BlockSpec((tm, LANES), lambda i: (i, 0)),
    ]
    out_specs = [pl.BlockSpec((tm, kvl), lambda i: (i, 0)), pl.BlockSpec((tm, rope), lambda i: (i, 0))]
    out_shape = [jax.ShapeDtypeStruct((n, kvl), F32), jax.ShapeDtypeStruct((n, rope), F32)]
    args = [zs, zs, g_kv_a.reshape(1, kvl), tab]
    if not with_kv:
        return pl.pallas_call(
            functools.partial(_latent_kernel, rope=rope),
            grid=(n // tm,), in_specs=in_specs, out_specs=out_specs, out_shape=out_shape,
            compiler_params=_cparams(("parallel",)), name="latent",
        )(*args)
    in_specs += [pl.BlockSpec((kvl, heads * nope), lambda i: (0, 0)), pl.BlockSpec((heads * vh, kvl), lambda i: (0, 0))]
    out_specs += [pl.BlockSpec((heads, tm, 2 * nope), lambda i: (0, i, 0)), pl.BlockSpec((heads, vh + SUM_ROWS, tm), lambda i: (0, 0, i))]
    out_shape += [jax.ShapeDtypeStruct((heads, n, 2 * nope), BF16), jax.ShapeDtypeStruct((heads, vh + SUM_ROWS, n), BF16)]
    return pl.pallas_call(
        functools.partial(_kv_prep_kernel, heads=heads, nope=nope, rope=rope, vh=vh),
        grid=(n // tm,), in_specs=in_specs, out_specs=out_specs, out_shape=out_shape,
        compiler_params=_cparams(("parallel",)), name="kv_prep",
    )(*args, w_uk_f, w_uv_tf)


def _flash_kernel(qi_ref, ki_ref, q_ref, k_ref, vt_ref, gate_ref, o_ref, m_scr, acc_scr, s_scr, *, heads, vh):
    p = pl.program_id(1)
    qi = qi_ref[p]
    ki = ki_ref[p]

    @pl.when(ki == 0)
    def _():
        m_scr[...] = jnp.full(m_scr.shape, -jnp.inf, F32)
        acc_scr[...] = jnp.zeros(acc_scr.shape, F32)

    def scores(h, slot):
        s_scr[slot] = lax.dot_general(k_ref[h], q_ref[h], _NT, preferred_element_type=F32)

    def update(h, slot, masked):
        s = s_scr[slot]
        if masked:
            kpos = lax.broadcasted_iota(I32, s.shape, 0)
            qpos = lax.broadcasted_iota(I32, s.shape, 1)
            s = jnp.where(kpos <= qpos, s, -jnp.inf)
        m_prev = m_scr[h]
        m_new = jnp.maximum(m_prev, jnp.max(s, axis=0, keepdims=True))
        pr = jnp.exp2(s - m_new).astype(BF16)
        acc_scr[h] = jnp.exp2(m_prev - m_new) * acc_scr[h] + jnp.dot(vt_ref[h], pr, preferred_element_type=F32)
        m_scr[h] = m_new

    def run(masked):
        scores(0, 0)

        def pair(i, carry):
            h0 = 2 * i
            scores(h0 + 1, 1)
            update(h0, 0, masked)
            scores(jnp.minimum(h0 + 2, heads - 1), 0)
            update(h0 + 1, 1, masked)
            return carry

        lax.fori_loop(0, heads // 2, pair, 0)

    @pl.when(ki < qi)
    def _():
        run(False)

    @pl.when(ki == qi)
    def _():
        run(True)
        for h in range(heads):
            acc = acc_scr[h]
            o = (acc[:vh] / acc[vh:vh + 1]).T
            g = jax.nn.sigmoid(gate_ref[:, h * vh:(h + 1) * vh].astype(F32))
            o_ref[:, h * vh:(h + 1) * vh] = (o * g).astype(o_ref.dtype)


def _flash_prompt(q, k, vt, zb, gate_off, batch, seq):
    heads, n, width = q.shape
    vr = vt.shape[1]
    vh = vr - SUM_ROWS
    d = heads * vh
    assert heads % 2 == 0
    tq = _tile(seq, 512, LANES)
    nq = seq // tq
    pairs = [(a, b) for a in range(nq) for b in range(a + 1)]
    qi = jnp.asarray([a for a, _ in pairs], I32)
    ki = jnp.asarray([b for _, b in pairs], I32)
    assert gate_off % d == 0
    grid_spec = pltpu.PrefetchScalarGridSpec(
        num_scalar_prefetch=2,
        grid=(batch, len(pairs)),
        in_specs=[
            pl.BlockSpec((heads, tq, width), lambda b, p, qi, ki: (0, b * nq + qi[p], 0)),
            pl.BlockSpec((heads, tq, width), lambda b, p, qi, ki: (0, b * nq + ki[p], 0)),
            pl.BlockSpec((heads, vr, tq), lambda b, p, qi, ki: (0, 0, b * nq + ki[p])),
            pl.BlockSpec((tq, d), lambda b, p, qi, ki: (b * nq + qi[p], gate_off // d)),
        ],
        out_specs=pl.BlockSpec((tq, d), lambda b, p, qi, ki: (b * nq + qi[p], 0)),
        scratch_shapes=[pltpu.VMEM((heads, 1, tq), F32), pltpu.VMEM((heads, vr, tq), F32), pltpu.VMEM((2, tq, tq), F32)],
    )
    return pl.pallas_call(
        functools.partial(_flash_kernel, heads=heads, vh=vh),
        grid_spec=grid_spec,
        out_shape=jax.ShapeDtypeStruct((n, d), BF16),
        compiler_params=_cparams(("parallel", "arbitrary")),
        name="flash_prompt",
    )(qi, ki, q, k, vt, zb)


def _absorb_kernel(q_ref, w_ref, o_ref, *, nope):
    o_ref[0] = lax.dot_general(q_ref[0, :, :nope], w_ref[0], _NT, preferred_element_type=F32)


def _absorb_q(qs, w_uk_h):
    heads, ns, width = qs.shape
    _, kvl, nope = w_uk_h.shape
    return pl.pallas_call(
        functools.partial(_absorb_kernel, nope=nope),
        grid=(heads,),
        in_specs=[pl.BlockSpec((1, ns, width), lambda h: (h, 0, 0)), pl.BlockSpec((1, kvl, nope), lambda h: (h, 0, 0))],
        out_specs=pl.BlockSpec((1, ns, kvl), lambda h: (h, 0, 0)),
        out_shape=jax.ShapeDtypeStruct((heads, ns, kvl), F32),
        compiler_params=_cparams(("parallel",)),
        name="absorb_q",
    )(qs, w_uk_h)


def _decode_attn_kernel(pt_ref, ql_ref, qr_ref, cn_ref, krn_ref, cc_hbm, ckr_hbm, o_ref, kc_buf, kr_buf, sem, *, chunk_pages, n_chunks, n_slots, page):
    s = pl.program_id(0)
    n_seq = pl.num_programs(0)

    def copies(seq, chunk, slot):
        out = []
        for j in range(chunk_pages):
            pg = pt_ref[seq, chunk * chunk_pages + j]
            out.append(pltpu.make_async_copy(cc_hbm.at[pg], kc_buf.at[slot, pl.ds(j * page, page)], sem.at[slot, 0]))
            out.append(pltpu.make_async_copy(ckr_hbm.at[pg], kr_buf.at[slot, :, pl.ds(j * page, page)], sem.at[slot, 1]))
        return out

    def start(seq, chunk, slot):
        for c in copies(seq, chunk, slot):
            c.start()

    ahead = n_slots - 1

    @pl.when(s == 0)
    def _():
        for c in range(ahead):
            start(0, c, c)

    ql = ql_ref[0]
    qr = qr_ref[0]
    cn = cn_ref[0]
    krn = krn_ref[0]
    m = jnp.sum(ql * cn, axis=1, keepdims=True) + jnp.sum(qr * krn, axis=1, keepdims=True)
    l = jnp.ones_like(m)
    acc = jnp.broadcast_to(cn, ql.shape)
    for c in range(n_chunks):
        slot = c % n_slots
        nxt = c + ahead
        if nxt < n_chunks:
            start(s, nxt, nxt % n_slots)
        else:
            @pl.when(s + 1 < n_seq)
            def _(nxt=nxt):
                start(s + 1, nxt - n_chunks, nxt % n_slots)
        for cp in copies(s, c, slot):
            cp.wait()
        kc = kc_buf[slot]
        kr_t = kr_buf[slot]
        sc = (lax.dot_general(ql, kc, _NT, preferred_element_type=F32)
              + jnp.dot(qr, kr_t, preferred_element_type=F32))
        m_new = jnp.maximum(m, jnp.max(sc, axis=1, keepdims=True))
        alpha = jnp.exp2(m - m_new)
        pr = jnp.exp2(sc - m_new)
        l = alpha * l + jnp.sum(pr, axis=1, keepdims=True)
        acc = alpha * acc + jnp.dot(pr, kc, preferred_element_type=F32)
        m = m_new
    o_ref[0] = acc / l


def _decode_attn(page_table, q_lat, q_rope, c_new, kr_new, cache_c, cache_kr_t):
    ns, heads, kvl = q_lat.shape
    rope = q_rope.shape[-1]
    n_pages = page_table.shape[1]
    page = cache_c.shape[1]
    assert cache_kr_t.shape[1:] == (rope, page)
    chunk_pages = _tile(n_pages, 16, 1)
    n_chunks = n_pages // chunk_pages
    assert n_chunks % 2 == 0
    n_slots = 4 if n_chunks % 4 == 0 else 2
    keys = chunk_pages * page
    grid_spec = pltpu.PrefetchScalarGridSpec(
        num_scalar_prefetch=1,
        grid=(ns,),
        in_specs=[
            pl.BlockSpec((1, heads, kvl), lambda s, pt: (s, 0, 0)),
            pl.BlockSpec((1, heads, rope), lambda s, pt: (s, 0, 0)),
            pl.BlockSpec((1, 1, kvl), lambda s, pt: (s, 0, 0)),
            pl.BlockSpec((1, 1, rope), lambda s, pt: (s, 0, 0)),
            pl.BlockSpec(memory_space=pl.ANY),
            pl.BlockSpec(memory_space=pl.ANY),
        ],
        out_specs=pl.BlockSpec((1, heads, kvl), lambda s, pt: (s, 0, 0)),
        scratch_shapes=[
            pltpu.VMEM((n_slots, keys, kvl), cache_c.dtype),
            pltpu.VMEM((n_slots, rope, keys), cache_kr_t.dtype),
            pltpu.SemaphoreType.DMA((n_slots, 2)),
        ],
    )
    return pl.pallas_call(
        functools.partial(_decode_attn_kernel, chunk_pages=chunk_pages, n_chunks=n_chunks, n_slots=n_slots, page=page),
        grid_spec=grid_spec,
        out_shape=jax.ShapeDtypeStruct((ns, heads, kvl), F32),
        compiler_params=_cparams(("arbitrary",)),
        name="decode_attn",
    )(page_table, q_lat, q_rope, c_new.reshape(ns, 1, kvl), kr_new.reshape(ns, 1, rope), cache_c, cache_kr_t)


def _unabsorb_kernel(o_ref, w_ref, gate_ref, y_ref):
    y = jnp.dot(o_ref[0].astype(BF16), w_ref[0], preferred_element_type=F32)
    y_ref[...] = (y * jax.nn.sigmoid(gate_ref[...].astype(F32))).astype(y_ref.dtype)


def _unabsorb_o(o_lat_h, w_uv_h, zb, gate_off):
    heads, ns, kvl = o_lat_h.shape
    vh = w_uv_h.shape[-1]
    assert gate_off % vh == 0
    return pl.pallas_call(
        _unabsorb_kernel,
        grid=(heads,),
        in_specs=[
            pl.BlockSpec((1, ns, kvl), lambda h: (h, 0, 0)),
            pl.BlockSpec((1, kvl, vh), lambda h: (h, 0, 0)),
            pl.BlockSpec((ns, vh), lambda h: (0, gate_off // vh + h)),
        ],
        out_specs=pl.BlockSpec((ns, vh), lambda h: (0, h)),
        out_shape=jax.ShapeDtypeStruct((ns, heads * vh), BF16),
        compiler_params=_cparams(("parallel",)),
        name="unabsorb_o",
    )(o_lat_h, w_uv_h, zb)


def _segment_mid(bc, s):
    c, dk = bc.shape
    if 2 * s >= 8:
        n = c // (2 * s)
        mid = bc.reshape(n, 2 * s, dk)[:, s - 1:s, :]
        return jnp.broadcast_to(mid, (n, 2 * s, dk)).reshape(c, dk)
    r = lax.broadcasted_iota(I32, bc.shape, 0) & (2 * s - 1)
    prev1 = pltpu.roll(bc, 1, 0)
    if s == 1:
        return jnp.where(r == 0, bc, prev1)
    assert s == 2
    return jnp.where(r == 0, pltpu.roll(bc, c - 1, 0), jnp.where(r == 1, bc, jnp.where(r == 2, prev1, pltpu.roll(bc, 2, 0))))


def _gla_prompt_kernel(q_ref, k_ref, v_ref, go_ref, gg_ref, lr_ref, wa_ref, ba_ref, gn_ref, y_ref, st_ref, st_scr, la_scr, *, n_chunks, c, scale):
    tb = pl.program_id(2)

    @pl.when(tb == 0)
    def _():
        st_scr[...] = jnp.zeros(st_scr.shape, F32)

    x = jnp.dot(lr_ref[...], wa_ref[...], preferred_element_type=F32, precision=HI) + ba_ref[...]
    la_scr[...] = _log_sigmoid(x) / GLA_TAU

    row = lax.broadcasted_iota(I32, (c, c), 0)
    col = lax.broadcasted_iota(I32, (c, c), 1)
    levels = []
    s = c // 2
    while s >= 1:
        levels.append(s)
        s //= 2
    tril = jnp.where(col <= row, 1.0, 0.0)
    masks = []
    for s in levels:
        sh = s.bit_length() - 1
        masks.append(((row >> (sh + 1)) == (col >> (sh + 1))) & (((row >> sh) & 1) == 1) & (((col >> sh) & 1) == 0))
    eye = row == col

    def chunk(ci, carry):
        r0 = pl.multiple_of(ci * c, c)
        q = q_ref[pl.ds(r0, c), :].astype(F32) * scale
        k = k_ref[pl.ds(r0, c), :].astype(F32)
        v = v_ref[pl.ds(r0, c), :]
        bc = jnp.dot(tril, la_scr[pl.ds(r0, c), :], preferred_element_type=F32, precision=HI)
        bl = bc[c - 1:c, :]
        att = jnp.where(eye, jnp.sum(q * k, axis=1, keepdims=True), 0.0)
        for li, s in enumerate(levels):
            ref = _segment_mid(bc, s)
            qh = (q * jnp.exp(jnp.minimum(bc - ref, 0.0))).astype(BF16)
            kh = (k * jnp.exp(jnp.minimum(ref - bc, 0.0))).astype(BF16)
            a = lax.dot_general(qh, kh, _NT, preferred_element_type=F32)
            att = att + jnp.where(masks[li], a, 0.0)
        st = st_scr[...]
        qt = (q * jnp.exp(bc)).astype(BF16)
        o = (lax.dot_general(qt, st.astype(BF16), _NT, preferred_element_type=F32)
             + jnp.dot(att.astype(BF16), v, preferred_element_type=F32))
        kb = (k * jnp.exp(bl - bc)).astype(BF16)
        st_scr[...] = st * jnp.exp(bl) + lax.dot_general(v, kb, _TN, preferred_element_type=F32)
        on = _rms(o, gn_ref[...])
        go = go_ref[pl.ds(r0, c), :].astype(F32)
        gg = gg_ref[pl.ds(r0, c), :].astype(F32)
        y_ref[pl.ds(r0, c), :] = (on * (go * jax.nn.sigmoid(go)) * jax.nn.sigmoid(gg)).astype(y_ref.dtype)
        return carry

    lax.fori_loop(0, n_chunks, chunk, 0, unroll=2 if n_chunks % 2 == 0 else 1)

    @pl.when(tb == pl.num_programs(2) - 1)
    def _():
        st_ref[0, 0] = st_scr[...].T


def _gla_prompt(zb, zs, offs, w_a2p, b_a2, g_gla, batch, seq, gh, dk, dv):
    n = batch * seq
    c = math.gcd(seq, GLA_CHUNK)
    tb = _tile(seq, 512, c)
    nb = seq // tb
    for key, w in (("gq", dk), ("gk", dk), ("gv", dv), ("g_out", dv), ("gate_gla", dv)):
        assert offs[key] % w == 0
    assert offs["g_lr"] % LANES == 0

    def rows(b, h, t):
        return b * nb + t

    in_specs = [
        pl.BlockSpec((tb, dk), lambda b, h, t: (rows(b, h, t), offs["gq"] // dk + h)),
        pl.BlockSpec((tb, dk), lambda b, h, t: (rows(b, h, t), offs["gk"] // dk + h)),
        pl.BlockSpec((tb, dv), lambda b, h, t: (rows(b, h, t), offs["gv"] // dv + h)),
        pl.BlockSpec((tb, dv), lambda b, h, t: (rows(b, h, t), offs["g_out"] // dv + h)),
        pl.BlockSpec((tb, dv), lambda b, h, t: (rows(b, h, t), offs["gate_gla"] // dv + h)),
        pl.BlockSpec((tb, LANES), lambda b, h, t: (rows(b, h, t), offs["g_lr"] // LANES)),
        pl.BlockSpec((LANES, dk), lambda b, h, t: (0, h)),
        pl.BlockSpec((1, dk), lambda b, h, t: (0, h)),
        pl.BlockSpec((1, dv), lambda b, h, t: (0, 0)),
    ]
    out_specs = [
        pl.BlockSpec((tb, dv), lambda b, h, t: (rows(b, h, t), h)),
        pl.BlockSpec((1, 1, dk, dv), lambda b, h, t: (b, h, 0, 0)),
    ]
    return pl.pallas_call(
        functools.partial(_gla_prompt_kernel, n_chunks=tb // c, c=c, scale=dk ** -0.5),
        grid=(batch, gh, nb),
        in_specs=in_specs,
        out_specs=out_specs,
        out_shape=[jax.ShapeDtypeStruct((n, gh * dv), BF16), jax.ShapeDtypeStruct((batch, gh, dk, dv), F32)],
        scratch_shapes=[pltpu.VMEM((dv, dk), F32), pltpu.VMEM((tb, dk), F32)],
        compiler_params=_cparams(("parallel", "parallel", "arbitrary")),
        name="gla_prompt",
    )(zb, zb, zb, zb, zb, zs, w_a2p, b_a2.reshape(1, gh * dk), g_gla.reshape(1, dv))


def _gla_decode_kernel(qk_ref, lrt_ref, wat_ref, bcol_ref, v_ref, go_ref, gg_ref, gn_ref, s0_ref, y_ref, s1_ref, *, sb, scale):
    x = jnp.dot(wat_ref[0], lrt_ref[0], preferred_element_type=F32, precision=HI) + bcol_ref[0]
    a = jnp.exp(_log_sigmoid(x) / GLA_TAU)
    outs = []
    for u in range(sb):
        qc = qk_ref[0, 0, :, u:u + 1] * scale
        kc = qk_ref[0, 0, :, sb + u:sb + u + 1]
        sn = a[:, u:u + 1] * s0_ref[u, 0] + kc * v_ref[u:u + 1, :]
        s1_ref[u, 0] = sn
        outs.append(jnp.sum(qc * sn, axis=0, keepdims=True))
    o = jnp.concatenate(outs, axis=0)
    go = go_ref[...]
    y_ref[...] = _rms(o, gn_ref[...]) * (go * jax.nn.sigmoid(go)) * jax.nn.sigmoid(gg_ref[...])


def _gla_decode(qk_cols, lr_t, wa_t, b_col, gv, g_out, gate_gla, g_gla, state):
    ns, gh, dk, dv = state.shape
    sb = qk_cols.shape[-1] // 2
    return pl.pallas_call(
        functools.partial(_gla_decode_kernel, sb=sb, scale=dk ** -0.5),
        grid=(ns // sb, gh),
        in_specs=[
            pl.BlockSpec((1, 1, dk, 2 * sb), lambda i, h: (i, h, 0, 0)),
            pl.BlockSpec((1, LANES, sb), lambda i, h: (i, 0, 0)),
            pl.BlockSpec((1, dk, LANES), lambda i, h: (h, 0, 0)),
            pl.BlockSpec((1, dk, 1), lambda i, h: (h, 0, 0)),
            pl.BlockSpec((sb, dv), lambda i, h: (i, h)),
            pl.BlockSpec((sb, dv), lambda i, h: (i, h)),
            pl.BlockSpec((sb, dv), lambda i, h: (i, h)),
            pl.BlockSpec((1, dv), lambda i, h: (0, 0)),
            pl.BlockSpec((sb, 1, dk, dv), lambda i, h: (i, h, 0, 0)),
        ],
        out_specs=[pl.BlockSpec((sb, dv), lambda i, h: (i, h)), pl.BlockSpec((sb, 1, dk, dv), lambda i, h: (i, h, 0, 0))],
        out_shape=[jax.ShapeDtypeStruct((ns, gh * dv), F32), jax.ShapeDtypeStruct((ns, gh, dk, dv), F32)],
        compiler_params=_cparams(("parallel", "parallel")),
        name="gla_decode",
    )(qk_cols, lr_t, wa_t, b_col, gv, g_out, gate_gla, g_gla.reshape(1, dv), state)


def _outproj_kernel(ym_ref, yg_ref, h_ref, w_ref, g_ref, wr_ref, br_ref, h1_ref, m_ref, route_ref):
    mix = (ym_ref[...].astype(F32) + yg_ref[...].astype(F32)).astype(BF16)
    h1 = h_ref[...] + jnp.dot(mix, w_ref[...], preferred_element_type=F32)
    h1_ref[...] = h1
    m = _rms(h1, g_ref[...])
    half = m.shape[1] // 2
    m_ref[...] = _pack_bf16_pair(m[:, :half], m[:, half:])
    m_hi = m.astype(BF16)
    m_lo = (m - m_hi.astype(F32)).astype(BF16)
    wr = wr_ref[...]
    w_hi = wr.astype(BF16)
    w_lo = (wr - w_hi.astype(F32)).astype(BF16)
    logits = (jnp.dot(m_hi, w_hi, preferred_element_type=F32) + jnp.dot(m_lo, w_hi, preferred_element_type=F32)
              + jnp.dot(m_hi, w_lo, preferred_element_type=F32) + br_ref[...])
    lane = lax.broadcasted_iota(I32, logits.shape, 1).astype(F32)
    vals, idxs = [], []
    for _ in range(TOP_K):
        mx = jnp.max(logits, axis=1, keepdims=True)
        ix = jnp.min(jnp.where(logits == mx, lane, float(LANES)), axis=1, keepdims=True)
        vals.append(mx)
        idxs.append(ix)
        logits = jnp.where(lane == ix, -jnp.inf, logits)
    ex = [jnp.exp(v - vals[0]) for v in vals]
    den = ex[0]
    for e in ex[1:]:
        den = den + e
    route = jnp.zeros(logits.shape, F32)
    for kk in range(TOP_K):
        route = jnp.where(lane == float(kk), idxs[kk], route)
        route = jnp.where(lane == float(TOP_K + kk), ex[kk] / den, route)
    route_ref[...] = route


def _outproj(y_mla, y_gla, h, w_out_b, g_ffn, w_router_p, b_router_p):
    n, d = h.shape
    tm = _tile(n, 256, 16)
    row = lambda i: (i, 0)
    fixed = lambda i: (0, 0)
    return pl.pallas_call(
        _outproj_kernel,
        grid=(n // tm,),
        in_specs=[
            pl.BlockSpec((tm, d), row), pl.BlockSpec((tm, d), row), pl.BlockSpec((tm, d), row),
            pl.BlockSpec((d, d), fixed), pl.BlockSpec((1, d), fixed),
            pl.BlockSpec((d, LANES), fixed), pl.BlockSpec((1, LANES), fixed),
        ],
        out_specs=[pl.BlockSpec((tm, d), row), pl.BlockSpec((tm, d // 2), row), pl.BlockSpec((tm, LANES), row)],
        out_shape=[jax.ShapeDtypeStruct((n, d), F32), jax.ShapeDtypeStruct((n, d // 2), U32), jax.ShapeDtypeStruct((n, LANES), F32)],
        compiler_params=_cparams(("parallel",)),
        name="outproj",
    )(y_mla, y_gla, h, w_out_b, g_ffn.reshape(1, d), w_router_p, b_router_p)


def _dispatch_kernel(idx_ref, x_ref, o_ref, g_scr, *, rows):
    base = pl.program_id(0) * rows

    def body(r, carry):
        tok = idx_ref[base + r]
        g_scr[pl.ds(r, 1), :] = x_ref[pl.ds(tok, 1), :]
        return carry

    lax.fori_loop(0, rows, body, 0, unroll=8)
    hi, lo = _unpack_bf16_pair(g_scr[...])
    o_ref[...] = jnp.concatenate([hi, lo], axis=1).astype(BF16)


def _dispatch(buf_tok, m_packed):
    rows_total = buf_tok.shape[0]
    n, w = m_packed.shape
    rows = _tile(rows_total, 512, 16)
    grid_spec = pltpu.PrefetchScalarGridSpec(
        num_scalar_prefetch=1,
        grid=(rows_total // rows,),
        in_specs=[pl.BlockSpec(memory_space=pltpu.VMEM)],
        out_specs=pl.BlockSpec((rows, 2 * w), lambda i, idx: (i, 0)),
        scratch_shapes=[pltpu.VMEM((rows, w), U32)],
    )
    return pl.pallas_call(
        functools.partial(_dispatch_kernel, rows=rows),
        grid_spec=grid_spec,
        out_shape=jax.ShapeDtypeStruct((rows_total, 2 * w), BF16),
        compiler_params=_cparams(("arbitrary",), 56),
        name="moe_dispatch",
    )(buf_tok, m_packed)


def _for_row_chunks(b0, b1, rb, max_blocks, fn):
    n = b1 - b0
    size = max_blocks
    while size >= 1:
        start = b0 + (n & ~(2 * size - 1))

        @pl.when((n & size) != 0)
        def _(start=start, size=size):
            fn(pl.multiple_of(start * rb, rb), size * rb)

        size //= 2


def _moe_kernel(vt_ref, ve_ref, b0_ref, b1_ref, zf_ref, xs_ref, wg_ref, wu_ref, bg_ref, bu_ref, wd_ref, bd_ref, ys_ref,
                y_scr, *, n_ff, n_slab, dn, rb):
    v = pl.program_id(0)
    j = pl.program_id(1)
    b0 = b0_ref[v]
    b1 = b1_ref[v]
    fill = zf_ref[v] == 1
    work = jnp.logical_not(fill)
    max_blocks = xs_ref.shape[0] // rb
    d = y_scr.shape[1]
    slab = d // n_slab

    @pl.when(fill & (j == 0))
    def _():
        def blk(b, carry):
            r0 = pl.multiple_of(b * rb, rb)
            for s in range(n_slab):
                ys_ref[s, 0, pl.ds(r0, rb), :] = jnp.zeros((rb, ys_ref.shape[3]), U32)
            return carry
        lax.fori_loop(b0, b1, blk, 0)

    @pl.when(work & (j == 0))
    def _():
        def init(r0, rows):
            y_scr[pl.ds(r0, rows), :] = jnp.broadcast_to(bd_ref[0], (rows, d))
        _for_row_chunks(b0, b1, rb, max_blocks, init)

    @pl.when(work)
    def _():
        def mlp(r0, rows):
            x = xs_ref[pl.ds(r0, rows), :]
            g = jnp.dot(x, wg_ref[0].astype(BF16), preferred_element_type=F32) + bg_ref[0]
            u = jnp.dot(x, wu_ref[0].astype(BF16), preferred_element_type=F32) + bu_ref[0]
            gate = jnp.minimum(g, SWIGLU_LIMIT)
            up = jnp.clip(u, -SWIGLU_LIMIT, SWIGLU_LIMIT)
            glu = gate * jax.nn.sigmoid(gate * SWIGLU_ALPHA)
            a = ((up + 1.0) * glu).astype(BF16)
            for c0 in range(0, d, dn):
                w = wd_ref[0, :, c0:c0 + dn].astype(BF16)
                y_scr[pl.ds(r0, rows), c0:c0 + dn] += jnp.dot(a, w, preferred_element_type=F32)
        _for_row_chunks(b0, b1, rb, max_blocks, mlp)

    @pl.when(work & (j == n_ff - 1))
    def _():
        def emit(r0, rows):
            for s in range(n_slab):
                y = y_scr[pl.ds(r0, rows), s * slab:(s + 1) * slab]
                ys_ref[s, 0, pl.ds(r0, rows), :] = _pack_bf16_pair(y[:, :slab // 2], y[:, slab // 2:])
        _for_row_chunks(b0, b1, rb, max_blocks, emit)


def _moe_experts(meta, xs, w_gu, b_gu, w_down, b_down, tile_rows):
    v_tile, v_exp, v_b0, v_b1, v_zero = meta
    rows_total, d = xs.shape
    e, _, f2 = w_gu.shape
    ff = f2 // 2
    tf = _tile(ff, 256, LANES)
    n_ff = ff // tf
    n_slab = d // _tile(d, MOE_SLAB, 2 * LANES)
    n_tiles = rows_total // tile_rows
    n_visits = v_tile.shape[0]
    ff_idx = lambda j, zf: jnp.where(zf == 1, n_ff - 1, j)
    weights = {}
    grid_spec = pltpu.PrefetchScalarGridSpec(
        num_scalar_prefetch=5,
        grid=(n_visits, n_ff),
        in_specs=[
            pl.BlockSpec((tile_rows, d), lambda v, j, vt, ve, b0, b1, zf: (vt[v], 0)),
            pl.BlockSpec((1, d, tf), lambda v, j, vt, ve, b0, b1, zf: (ve[v], 0, ff_idx(j, zf[v])), **weights),
            pl.BlockSpec((1, d, tf), lambda v, j, vt, ve, b0, b1, zf: (ve[v], 0, n_ff + ff_idx(j, zf[v])), **weights),
            pl.BlockSpec((1, 1, tf), lambda v, j, vt, ve, b0, b1, zf: (ve[v], 0, ff_idx(j, zf[v]))),
            pl.BlockSpec((1, 1, tf), lambda v, j, vt, ve, b0, b1, zf: (ve[v], 0, n_ff + ff_idx(j, zf[v]))),
            pl.BlockSpec((1, tf, d), lambda v, j, vt, ve, b0, b1, zf: (ve[v], ff_idx(j, zf[v]), 0), **weights),
            pl.BlockSpec((1, 1, d), lambda v, j, vt, ve, b0, b1, zf: (ve[v], 0, 0)),
        ],
        out_specs=pl.BlockSpec((n_slab, 1, tile_rows, d // n_slab // 2), lambda v, j, vt, ve, b0, b1, zf: (0, vt[v], 0, 0)),
        scratch_shapes=[pltpu.VMEM((tile_rows, d), F32)],
    )
    return pl.pallas_call(
        functools.partial(_moe_kernel, n_ff=n_ff, n_slab=n_slab, dn=_tile(d, 512, LANES), rb=EXPERT_ROW_BLOCK),
        grid_spec=grid_spec,
        out_shape=jax.ShapeDtypeStruct((n_slab, n_tiles, tile_rows, d // n_slab // 2), U32),
        compiler_params=_cparams(("arbitrary", "arbitrary"), 56),
        name="moe_experts",
    )(v_tile, v_exp, v_b0, v_b1, v_zero, xs, w_gu, w_gu, b_gu.reshape(e, 1, f2), b_gu.reshape(e, 1, f2), w_down, b_down.reshape(e, 1, d))


def _combine_kernel(dest_ref, ys_ref, route_ref, o_ref, stage, *, tm):
    base = pl.program_id(1) * tm

    def body(t, carry):
        for kk in range(TOP_K):
            r = dest_ref[(base + t) * TOP_K + kk]
            stage[kk, pl.ds(t, 1), :] = ys_ref[0, pl.ds(r, 1), :]
        return carry

    lax.fori_loop(0, tm, body, 0, unroll=4)
    acc_hi = acc_lo = None
    for kk in range(TOP_K):
        g = route_ref[:, TOP_K + kk:TOP_K + kk + 1]
        hi, lo = _unpack_bf16_pair(stage[kk])
        acc_hi = g * hi if acc_hi is None else acc_hi + g * hi
        acc_lo = g * lo if acc_lo is None else acc_lo + g * lo
    o_ref[...] = jnp.concatenate([acc_hi, acc_lo], axis=1)


def _combine(dest, ys, route):
    n_down, rows_total, wh = ys.shape
    n = route.shape[0]
    tm = _tile(n, 256, 8)
    grid_spec = pltpu.PrefetchScalarGridSpec(
        num_scalar_prefetch=1,
        grid=(n_down, n // tm),
        in_specs=[
            pl.BlockSpec((1, rows_total, wh), lambda c, i, dest: (c, 0, 0), pipeline_mode=pl.Buffered(1)),
            pl.BlockSpec((tm, LANES), lambda c, i, dest: (i, 0)),
        ],
        out_specs=pl.BlockSpec((tm, 2 * wh), lambda c, i, dest: (i, c)),
        scratch_shapes=[pltpu.VMEM((TOP_K, tm, wh), U32)],
    )
    return pl.pallas_call(
        functools.partial(_combine_kernel, tm=tm),
        grid_spec=grid_spec,
        out_shape=jax.ShapeDtypeStruct((n, n_down * 2 * wh), F32),
        compiler_params=_cparams(("arbitrary", "arbitrary"), 56),
        name="moe_combine",
    )(dest, ys, route)


def _route_meta(top_i, n_experts, tile_rows):
    n_tok = top_i.shape[0]
    n_asg = n_tok * TOP_K
    rb = EXPERT_ROW_BLOCK
    e_flat = top_i.reshape(n_asg)
    tok_flat = (jnp.arange(n_asg, dtype=I32) // TOP_K).astype(I32)
    onehot = jax.nn.one_hot(e_flat, n_experts, dtype=I32)
    rank = jnp.take_along_axis(jnp.cumsum(onehot, axis=0), e_flat[:, None], axis=1)[:, 0] - 1
    counts = onehot.sum(axis=0)
    padded = (counts + rb - 1) // rb * rb
    ends = jnp.cumsum(padded)
    starts = ends - padded
    dest = (starts[e_flat] + rank).astype(I32)
    rows_total = (-(-n_asg // rb) + n_experts) * rb
    rows_total = -(-rows_total // tile_rows) * tile_rows
    buf_tok = jnp.zeros((rows_total,), I32).at[dest].set(tok_flat, unique_indices=True, mode="promise_in_bounds")
    n_tiles = rows_total // tile_rows
    total = ends[-1].astype(I32)
    bp = jnp.sort(jnp.concatenate([jnp.arange(n_tiles, dtype=I32) * tile_rows, starts.astype(I32), total[None]]))
    lo = bp
    length = jnp.concatenate([bp[1:], jnp.full((1,), rows_total, I32)]) - lo
    valid = length > 0
    order = jnp.argsort(jnp.logical_not(valid), stable=True)
    lo, length = lo[order], length[order]
    n_valid = jnp.sum(valid.astype(I32))
    pos = jnp.arange(lo.shape[0], dtype=I32)
    src = jnp.minimum(pos, n_valid - 1)
    lo_s = lo[src]
    live = pos < n_valid
    v_tile = (lo_s // tile_rows).astype(I32)
    v_exp = jnp.minimum(jnp.searchsorted(ends, jnp.minimum(lo_s, total - 1), side="right"), n_experts - 1).astype(I32)
    v_b0 = jnp.where(live, (lo_s % tile_rows) // rb, 0).astype(I32)
    v_b1 = jnp.where(live, v_b0 + length[src] // rb, 0).astype(I32)
    v_zero = ((lo_s >= total) | jnp.logical_not(live)).astype(I32)
    return buf_tok, dest, (v_tile, v_exp, v_b0, v_b1, v_zero)


def _ple_kernel(h1_ref, moe_ref, p_ref, wp_ref, wg_ref, gp_ref, gf_ref, y_ref, *, final):
    h2 = h1_ref[...] + moe_ref[...]
    u = _rms(h2, gp_ref[...]).astype(BF16)
    gate = jax.nn.sigmoid(jnp.dot(u, wg_ref[...], preferred_element_type=F32))
    pe = jnp.dot(p_ref[...].astype(BF16), wp_ref[...], preferred_element_type=F32)
    y = h2 + pe * gate
    y_ref[...] = _rms(y, gf_ref[...]) if final else y


def _ple_final(h1, moe, row_off, p, w_ple_b, w_ple_gate_b, g_ple, g_final, final):
    n, d = h1.shape
    pd = p.shape[1]
    tm = _tile(n, 256, 8)
    assert row_off % tm == 0
    row = lambda i: (i, 0)
    fixed = lambda i: (0, 0)
    return pl.pallas_call(
        functools.partial(_ple_kernel, final=final),
        grid=(n // tm,),
        in_specs=[
            pl.BlockSpec((tm, d), row), pl.BlockSpec((tm, d), lambda i: (row_off // tm + i, 0)), pl.BlockSpec((tm, pd), row),
            pl.BlockSpec((pd, d), fixed), pl.BlockSpec((d, d), fixed), pl.BlockSpec((1, d), fixed), pl.BlockSpec((1, d), fixed),
        ],
        out_specs=pl.BlockSpec((tm, d), row),
        out_shape=jax.ShapeDtypeStruct((n, d), F32),
        compiler_params=_cparams(("parallel",)),
        name="ple_final",
    )(h1, moe, p, w_ple_b, w_ple_gate_b, g_ple.reshape(1, d), g_final.reshape(1, d))


def _rope_table(pos, rope):
    half = rope // 2
    inv_freq = jnp.power(ROPE_THETA, -jnp.arange(half, dtype=F32) / half)
    ang = pos.astype(F32)[:, None] * inv_freq[None, :]
    cos, sin = jnp.cos(ang), jnp.sin(ang)
    return jnp.concatenate([cos, cos, -sin, sin], axis=1)


def _swap_halves(w):
    half = w.shape[-1] // 2
    return jnp.concatenate([w[..., half:], w[..., :half]], axis=-1)


def kernel(x_prompt, x_sample, cache_kv_latent, cache_k_rope, state_gla, page_table, p_prompt, p_sample, g_mix, w_in, g_q_a, w_uq, g_kv_a, w_uk, w_uv, w_a2, b_a2, g_gla, w_out, g_ffn, w_router, b_router, w_gu, b_gu, w_down, b_down, g_ple, w_ple, w_ple_gate, g_final):
    depth = w_in.shape[0]
    batch, seq, d = x_prompt.shape
    ns, dec_seq, _ = x_sample.shape
    assert dec_seq == 1
    ql, kvl = g_q_a.shape[-1], g_kv_a.shape[-1]
    rope = cache_k_rope.shape[-1]
    heads, nope = w_uk.shape[2], w_uk.shape[3]
    vh = w_uv.shape[3]
    _, _, gh, dk, dv = state_gla.shape
    lr = w_a2.shape[1]
    n_experts = w_router.shape[-1]
    past = page_table.shape[1] * cache_kv_latent.shape[2]
    assert 2 * rope == LANES and nope == LANES and heads * vh == d and lr <= LANES and n_experts <= LANES
    scale = LOG2E / math.sqrt(nope + rope)
    n_p = batch * seq
    tile_rows = 1024

    big_segs = (("gv", gh * dv), ("g_out", gh * dv), ("gate_mla", d), ("gate_gla", d), ("gq", gh * dk), ("gk", gh * dk))
    offs, o = {}, 0
    for key, w in big_segs:
        offs[key] = o
        o += w
    offs["g_lr"] = ql + kvl + 2 * rope
    in_splits = (ql, kvl, rope, gh * dk, gh * dk, gh * dv, gh * dv, lr, d, d)
    in_offsets = tuple(sum(in_splits[:i + 1]) for i in range(len(in_splits) - 1))

    tab_p = jnp.tile(_rope_table(jnp.arange(seq), rope), (batch, 1))
    tab_s = _rope_table(jnp.full((ns,), past, I32), rope)

    h_p = x_prompt.reshape(n_p, d)
    h_s = x_sample.reshape(ns, d)
    kv_p, kr_p, st_p, kv_s, kr_s, st_s = [], [], [], [], [], []
    for l in range(depth):
        seg = dict(zip(("c_q", "c_kv", "k_r", "gq", "gk", "gv", "g_out", "g_lr", "gate_mla", "gate_gla"), jnp.split(w_in[l], in_offsets, axis=1)))
        w_big = jnp.concatenate([seg[k] for k, _ in big_segs], axis=1).astype(BF16)
        w_small = jnp.concatenate([seg["c_q"], seg["c_kv"], seg["k_r"], _swap_halves(seg["k_r"]),
                                   jnp.pad(seg["g_lr"], ((0, 0), (0, LANES - lr)))], axis=1).astype(BF16)
        wq = w_uq[l].reshape(ql, heads, nope + rope)
        w_uq_p = jnp.concatenate([wq[..., :nope], wq[..., nope:], _swap_halves(wq[..., nope:])], axis=-1).transpose(1, 0, 2).astype(BF16)
        w_uk_h = w_uk[l].transpose(1, 0, 2).astype(BF16)
        w_uv_h = w_uv[l].transpose(1, 0, 2).astype(BF16)
        w_uk_f = w_uk[l].reshape(kvl, heads * nope).astype(BF16)
        w_uv_tf = w_uv[l].reshape(kvl, heads * vh).T.astype(BF16)
        w_a2p = jnp.pad(w_a2[l], ((0, LANES - lr), (0, 0)))
        w_out_b = w_out[l].astype(BF16)
        w_router_p = jnp.pad(w_router[l], ((0, 0), (0, LANES - n_experts)))
        b_router_p = jnp.pad(b_router[l], (0, LANES - n_experts), constant_values=NEG_BIG).reshape(1, LANES)
        w_ple_b = w_ple[l].astype(BF16)
        w_ple_gate_b = w_ple_gate[l].astype(BF16)

        def front(h, tab):
            a = _rms_cast(h, g_mix[l])
            zb = _matmul(a, w_big, BF16, "inproj_big")
            zs = _matmul(a, w_small, F32, "inproj_small")
            q = _q_proj(zs, g_q_a[l], w_uq_p, tab, scale)
            return zb, zs, q

        zb, zs, q = front(h_p, tab_p)
        c_kv, k_r, k_heads, v_t = _kv_prep(zs, g_kv_a[l], tab_p, w_uk_f, w_uv_tf, heads, ql, rope, True)
        y_mla = _flash_prompt(q, k_heads, v_t, zb, offs["gate_mla"], batch, seq)
        y_gla, s_new = _gla_prompt(zb, zs, offs, w_a2p, b_a2[l], g_gla[l], batch, seq, gh, dk, dv)
        h1_p, m_p, route_p = _outproj(y_mla, y_gla, h_p, w_out_b, g_ffn[l], w_router_p, b_router_p)
        kv_p.append(c_kv.reshape(batch, seq, kvl)); kr_p.append(k_r.reshape(batch, seq, rope)); st_p.append(s_new)

        zb, zs, q = front(h_s, tab_s)
        c_kv, k_r = _kv_prep(zs, g_kv_a[l], tab_s, w_uk_f, w_uv_tf, heads, ql, rope, False)
        q_lat = _absorb_q(q, w_uk_h).transpose(1, 0, 2)
        q_rope = q[:, :, nope:nope + rope].astype(F32).transpose(1, 0, 2)
        o_lat = _decode_attn(page_table, q_lat, q_rope, c_kv, k_r, cache_kv_latent[l], cache_k_rope[l].transpose(0, 2, 1))
        y_mla = _unabsorb_o(o_lat.transpose(1, 0, 2), w_uv_h, zb, offs["gate_mla"])
        sb = 8
        zf = zb.astype(F32)
        cols = lambda key, w: zf[:, offs[key]:offs[key] + w]
        to_cols = lambda z: z.reshape(ns // sb, sb, gh, dk).transpose(0, 2, 3, 1)
        qk_cols = jnp.concatenate([to_cols(cols("gq", gh * dk)), to_cols(cols("gk", gh * dk))], axis=-1)
        lr_t = zs[:, offs["g_lr"]:offs["g_lr"] + LANES].reshape(ns // sb, sb, LANES).transpose(0, 2, 1)
        wa_t = w_a2p.reshape(LANES, gh, dk).transpose(1, 2, 0)
        b_col = b_a2[l].reshape(gh, dk, 1)
        y_gla, s_new = _gla_decode(qk_cols, lr_t, wa_t, b_col, cols("gv", gh * dv), cols("g_out", gh * dv),
                                   cols("gate_gla", gh * dv), g_gla[l], state_gla[l])
        h1_s, m_s, route_s = _outproj(y_mla, y_gla, h_s, w_out_b, g_ffn[l], w_router_p, b_router_p)
        kv_s.append(c_kv.reshape(ns, 1, kvl)); kr_s.append(k_r.reshape(ns, 1, rope)); st_s.append(s_new)

        m_all = jnp.concatenate([m_p, m_s], axis=0)
        route = jnp.concatenate([route_p, route_s], axis=0)
        top_i = route[:, :TOP_K].astype(I32)
        buf_tok, dest, meta = _route_meta(top_i, n_experts, tile_rows)
        xs = _dispatch(buf_tok, m_all)
        ys = _moe_experts(meta, xs, w_gu[l], b_gu[l], w_down[l], b_down[l], tile_rows)
        ys = ys.reshape(ys.shape[0], ys.shape[1] * ys.shape[2], ys.shape[3])
        moe = _combine(dest, ys, route)

        last = l == depth - 1
        h_p = _ple_final(h1_p, moe, 0, p_prompt[l].reshape(n_p, -1), w_ple_b, w_ple_gate_b, g_ple[l], g_final, last)
        h_s = _ple_final(h1_s, moe, n_p, p_sample[l].reshape(ns, -1), w_ple_b, w_ple_gate_b, g_ple[l], g_final, last)

    return (h_p.reshape(batch, seq, d), h_s.reshape(ns, 1, d), jnp.stack(kv_p), jnp.stack(kr_p), jnp.stack(st_p),
            jnp.stack(kv_s), jnp.stack(kr_s), jnp.stack(st_s))
```

```python
import functools
import math

import jax
import jax.numpy as jnp
from jax import lax
from jax.experimental import pallas as pl
from jax.experimental.pallas import tpu as pltpu

F32 = jnp.float32
BF16 = jnp.bfloat16
U32 = jnp.uint32
I32 = jnp.int32

EPS = 1e-6
ROPE_THETA = 10000.0
GLA_TAU = 16.0
GLA_CHUNK = 64
TOP_K = 4
SWIGLU_LIMIT = 7.0
SWIGLU_ALPHA = 1.702
EXPERT_ROW_BLOCK = 128
MOE_SLAB = 256
MOE_TILE_ROWS = 1536
LANES = 128
MIB = 1024 * 1024
SUM_ROWS = 16
LOG2E = 1.4426950408889634
HI = lax.Precision.HIGHEST
NEG_BIG = -1e30

_NT = (((1,), (1,)), ((), ()))
_TN = (((0,), (0,)), ((), ()))


def _cparams(sem, vmem_mib=48):
    return pltpu.CompilerParams(dimension_semantics=sem, vmem_limit_bytes=vmem_mib * MIB)


def _tile(n, cap, mult):
    best = None
    for t in range(mult, min(n, cap) + 1, mult):
        if n % t == 0:
            best = t
    assert best is not None, (n, cap, mult)
    return best


def _rms(x, g):
    return x * lax.rsqrt(jnp.mean(x * x, axis=-1, keepdims=True) + EPS) * g


def _log_sigmoid(x):
    return jnp.minimum(x, 0.0) - jnp.log1p(jnp.exp(-jnp.abs(x)))


def _pack_bf16_pair(hi, lo):
    hb = pltpu.bitcast(hi.astype(BF16).astype(F32), U32)
    lb = pltpu.bitcast(lo.astype(BF16).astype(F32), U32)
    return hb | (lb >> 16)


def _unpack_bf16_pair(w):
    hi = pltpu.bitcast(w & jnp.uint32(0xFFFF0000), F32)
    lo = pltpu.bitcast(w << 16, F32)
    return hi, lo


def _rms_cast_kernel(x_ref, g_ref, o_ref):
    o_ref[...] = _rms(x_ref[...], g_ref[...]).astype(o_ref.dtype)


def _rms_cast(x, g):
    n, d = x.shape
    tm = _tile(n, 512, 16)
    return pl.pallas_call(
        _rms_cast_kernel,
        grid=(n // tm,),
        in_specs=[pl.BlockSpec((tm, d), lambda i: (i, 0)), pl.BlockSpec((1, d), lambda i: (0, 0))],
        out_specs=pl.BlockSpec((tm, d), lambda i: (i, 0)),
        out_shape=jax.ShapeDtypeStruct((n, d), BF16),
        compiler_params=_cparams(("parallel",)),
        name="rms_cast",
    )(x, g.reshape(1, d))


def _mm_kernel(x_ref, w_ref, o_ref):
    o_ref[...] = jnp.dot(x_ref[...], w_ref[...], preferred_element_type=F32).astype(o_ref.dtype)


def _matmul(x, w, out_dtype, name):
    m, k = x.shape
    n = w.shape[1]
    tm = _tile(m, 1024, 16)
    tn = _tile(n, 1280, LANES)
    return pl.pallas_call(
        _mm_kernel,
        grid=(n // tn, m // tm),
        in_specs=[pl.BlockSpec((tm, k), lambda j, i: (i, 0)), pl.BlockSpec((k, tn), lambda j, i: (0, j))],
        out_specs=pl.BlockSpec((tm, tn), lambda j, i: (i, j)),
        out_shape=jax.ShapeDtypeStruct((m, n), out_dtype),
        compiler_params=_cparams(("parallel", "parallel")),
        name=name,
    )(x, w)


def _rope_combine(y2, tab):
    half = y2.shape[1] // 2
    t = y2 * tab
    r = t + pltpu.roll(t, half, axis=1)
    lane = lax.broadcasted_iota(I32, r.shape, 1)
    return jnp.where(lane < half, r, 0.0)


def _q_proj_kernel(cq_ref, g_ref, w_ref, tab_ref, q_ref, *, heads, nope, scale):
    cqn = _rms(cq_ref[...], g_ref[...]).astype(BF16)
    tab = tab_ref[...]
    for h in range(heads):
        y = jnp.dot(cqn, w_ref[h], preferred_element_type=F32)
        q_ref[h, :, :nope] = (y[:, :nope] * scale).astype(BF16)
        q_ref[h, :, nope:] = (_rope_combine(y[:, nope:], tab) * scale).astype(BF16)


def _q_proj(zs, g_q_a, w_uq_p, tab, scale):
    n = zs.shape[0]
    heads, ql, width = w_uq_p.shape
    tm = _tile(n, 512, 16)
    return pl.pallas_call(
        functools.partial(_q_proj_kernel, heads=heads, nope=width // 2, scale=scale),
        grid=(n // tm,),
        in_specs=[
            pl.BlockSpec((tm, ql), lambda i: (i, 0)),
            pl.BlockSpec((1, ql), lambda i: (0, 0)),
            pl.BlockSpec((heads, ql, width), lambda i: (0, 0, 0)),
            pl.BlockSpec((tm, LANES), lambda i: (i, 0)),
        ],
        out_specs=pl.BlockSpec((heads, tm, width), lambda i: (0, i, 0)),
        out_shape=jax.ShapeDtypeStruct((heads, n, width), BF16),
        compiler_params=_cparams(("parallel",)),
        name="q_proj",
    )(zs, g_q_a.reshape(1, ql), w_uq_p, tab)


def _kv_prep_kernel(ckv_ref, krs_ref, g_ref, tab_ref, wuk_ref, wuvt_ref, c_ref, kr_ref, k_ref, vt_ref, *, heads, nope, rope, vh):
    c = _rms(ckv_ref[...], g_ref[...])
    c_ref[...] = c
    cb = c.astype(BF16)
    r = _rope_combine(krs_ref[...], tab_ref[...])
    kr_ref[...] = r[:, :rope]
    rb = r.astype(BF16)
    kn = jnp.dot(cb, wuk_ref[...], preferred_element_type=F32).astype(BF16)
    vt = lax.dot_general(wuvt_ref[...], cb, _NT, preferred_element_type=F32).astype(BF16)
    for h in range(heads):
        k_ref[h, :, :nope] = kn[:, h * nope:(h + 1) * nope]
        k_ref[h, :, nope:] = rb
        vt_ref[h, :vh, :] = vt[h * vh:(h + 1) * vh, :]
        vt_ref[h, vh:, :] = jnp.ones((SUM_ROWS, cb.shape[0]), BF16)


def _latent_kernel(ckv_ref, krs_ref, g_ref, tab_ref, c_ref, kr_ref, *, rope):
    c_ref[...] = _rms(ckv_ref[...], g_ref[...])
    kr_ref[...] = _rope_combine(krs_ref[...], tab_ref[...])[:, :rope]


def _kv_prep(zs, g_kv_a, tab, w_uk_f, w_uv_tf, heads, ql, rope, with_kv):
    n = zs.shape[0]
    kvl = w_uk_f.shape[0]
    nope = w_uk_f.shape[1] // heads
    vh = w_uv_tf.shape[0] // heads
    assert ql % kvl == 0 and (ql + kvl) % LANES == 0
    tm = _tile(n, 640, LANES if with_kv else 8)
    in_specs = [
        pl.BlockSpec((tm, kvl), lambda i: (i, ql // kvl)),
        pl.BlockSpec((tm, LANES), lambda i: (i, (ql + kvl) // LANES)),
        pl.BlockSpec((1, kvl), lambda i: (0, 0)),
        pl.BlockSpec((tm, LANES), lambda i: (i, 0)),
    ]
    out_specs = [pl.BlockSpec((tm, kvl), lambda i: (i, 0)), pl.BlockSpec((tm, rope), lambda i: (i, 0))]
    out_shape = [jax.ShapeDtypeStruct((n, kvl), F32), jax.ShapeDtypeStruct((n, rope), F32)]
    args = [zs, zs, g_kv_a.reshape(1, kvl), tab]
    if not with_kv:
        return pl.pallas_call(
            functools.partial(_latent_kernel, rope=rope),
            grid=(n // tm,), in_specs=in_specs, out_specs=out_specs, out_shape=out_shape,
            compiler_params=_cparams(("parallel",)), name="latent",
        )(*args)
    in_specs += [pl.BlockSpec((kvl, heads * nope), lambda i: (0, 0)), pl.BlockSpec((heads * vh, kvl), lambda i: (0, 0))]
    out_specs += [pl.BlockSpec((heads, tm, 2 * nope), lambda i: (0, i, 0)), pl.BlockSpec((heads, vh + SUM_ROWS, tm), lambda i: (0, 0, i))]
    out_shape += [jax.ShapeDtypeStruct((heads, n, 2 * nope), BF16), jax.ShapeDtypeStruct((heads, vh + SUM_ROWS, n), BF16)]
    return pl.pallas_call(
        functools.partial(_kv_prep_kernel, heads=heads, nope=nope, rope=rope, vh=vh),
        grid=(n // tm,), in_specs=in_specs, out_specs=out_specs, out_shape=out_shape,
        compiler_params=_cparams(("parallel",)), name="kv_prep",
    )(*args, w_uk_f, w_uv_tf)


def _flash_kernel(qi_ref, ki_ref, q_ref, k_ref, vt_ref, gate_ref, o_ref, m_scr, acc_scr, s_scr, *, heads, vh):
    p = pl.program_id(1)
    qi = qi_ref[p]
    ki = ki_ref[p]

    @pl.when(ki == 0)
    def _():
        m_scr[...] = jnp.full(m_scr.shape, -jnp.inf, F32)
        acc_scr[...] = jnp.zeros(acc_scr.shape, F32)

    def scores(h, slot):
        s_scr[slot] = lax.dot_general(k_ref[h], q_ref[h], _NT, preferred_element_type=F32)

    def update(h, slot, masked):
        s = s_scr[slot]
        if masked:
            kpos = lax.broadcasted_iota(I32, s.shape, 0)
            qpos = lax.broadcasted_iota(I32, s.shape, 1)
            s = jnp.where(kpos <= qpos, s, -jnp.inf)
        m_prev = m_scr[h]
        m_new = jnp.maximum(m_prev, jnp.max(s, axis=0, keepdims=True))
        pr = jnp.exp2(s - m_new).astype(BF16)
        acc_scr[h] = jnp.exp2(m_prev - m_new) * acc_scr[h] + jnp.dot(vt_ref[h], pr, preferred_element_type=F32)
        m_scr[h] = m_new

    def run(masked):
        scores(0, 0)

        def pair(i, carry):
            h0 = 2 * i
            scores(h0 + 1, 1)
            update(h0, 0, masked)
            scores(jnp.minimum(h0 + 2, heads - 1), 0)
            update(h0 + 1, 1, masked)
            return carry

        lax.fori_loop(0, heads // 2, pair, 0)

    @pl.when(ki < qi)
    def _():
        run(False)

    @pl.when(ki == qi)
    def _():
        run(True)
        for h in range(heads):
            acc = acc_scr[h]
            o = (acc[:vh] / acc[vh:vh + 1]).T
            g = jax.nn.sigmoid(gate_ref[:, h * vh:(h + 1) * vh].astype(F32))
            o_ref[:, h * vh:(h + 1) * vh] = (o * g).astype(o_ref.dtype)


def _flash_prompt(q, k, vt, zb, gate_off, batch, seq):
    heads, n, width = q.shape
    vr = vt.shape[1]
    vh = vr - SUM_ROWS
    d = heads * vh
    assert heads % 2 == 0
    tq = _tile(seq, 512, LANES)
    nq = seq // tq
    pairs = [(a, b) for a in range(nq) for b in range(a + 1)]
    qi = jnp.asarray([a for a, _ in pairs], I32)
    ki = jnp.asarray([b for _, b in pairs], I32)
    assert gate_off % d == 0
    grid_spec = pltpu.PrefetchScalarGridSpec(
        num_scalar_prefetch=2,
        grid=(batch, len(pairs)),
        in_specs=[
            pl.BlockSpec((heads, tq, width), lambda b, p, qi, ki: (0, b * nq + qi[p], 0)),
            pl.BlockSpec((heads, tq, width), lambda b, p, qi, ki: (0, b * nq + ki[p], 0)),
            pl.BlockSpec((heads, vr, tq), lambda b, p, qi, ki: (0, 0, b * nq + ki[p])),
            pl.BlockSpec((tq, d), lambda b, p, qi, ki: (b * nq + qi[p], gate_off // d)),
        ],
        out_specs=pl.BlockSpec((tq, d), lambda b, p, qi, ki: (b * nq + qi[p], 0)),
        scratch_shapes=[pltpu.VMEM((heads, 1, tq), F32), pltpu.VMEM((heads, vr, tq), F32), pltpu.VMEM((2, tq, tq), F32)],
    )
    return pl.pallas_call(
        functools.partial(_flash_kernel, heads=heads, vh=vh),
        grid_spec=grid_spec,
        out_shape=jax.ShapeDtypeStruct((n, d), BF16),
        compiler_params=_cparams(("parallel", "arbitrary")),
        name="flash_prompt",
    )(qi, ki, q, k, vt, zb)


def _absorb_kernel(q_ref, w_ref, o_ref, *, nope):
    o_ref[0] = lax.dot_general(q_ref[0, :, :nope], w_ref[0], _NT, preferred_element_type=F32)


def _absorb_q(qs, w_uk_h):
    heads, ns, width = qs.shape
    _, kvl, nope = w_uk_h.shape
    return pl.pallas_call(
        functools.partial(_absorb_kernel, nope=nope),
        grid=(heads,),
        in_specs=[pl.BlockSpec((1, ns, width), lambda h: (h, 0, 0)), pl.BlockSpec((1, kvl, nope), lambda h: (h, 0, 0))],
        out_specs=pl.BlockSpec((1, ns, kvl), lambda h: (h, 0, 0)),
        out_shape=jax.ShapeDtypeStruct((heads, ns, kvl), F32),
        compiler_params=_cparams(("parallel",)),
        name="absorb_q",
    )(qs, w_uk_h)


def _decode_attn_kernel(pt_ref, ql_ref, qr_ref, cn_ref, krn_ref, cc_hbm, ckr_hbm, o_ref, kc_buf, kr_buf, sem, *, chunk_pages, n_chunks, n_slots, page, scale):
    s = pl.program_id(0)
    n_seq = pl.num_programs(0)

    def copies(seq, chunk, slot):
        out = []
        for j in range(chunk_pages):
            pg = pt_ref[seq, chunk * chunk_pages + j]
            out.append(pltpu.make_async_copy(cc_hbm.at[pg], kc_buf.at[slot, pl.ds(j * page, page)], sem.at[slot, 0]))
            out.append(pltpu.make_async_copy(ckr_hbm.at[pg], kr_buf.at[slot, :, pl.ds(j * page, page)], sem.at[slot, 1]))
        return out

    def start(seq, chunk, slot):
        for c in copies(seq, chunk, slot):
            c.start()

    ahead = n_slots - 1

    @pl.when(s == 0)
    def _():
        for c in range(ahead):
            start(0, c, c)

    ql = ql_ref[0]
    qr = qr_ref[0]
    cn = cn_ref[0]
    krn = krn_ref[0]
    m = (jnp.sum(ql * cn, axis=1, keepdims=True) + jnp.sum(qr * krn, axis=1, keepdims=True)) * scale
    l = jnp.ones_like(m)
    acc = jnp.broadcast_to(cn, ql.shape)
    for c in range(n_chunks):
        slot = c % n_slots
        nxt = c + ahead
        if nxt < n_chunks:
            start(s, nxt, nxt % n_slots)
        else:
            @pl.when(s + 1 < n_seq)
            def _(nxt=nxt):
                start(s + 1, nxt - n_chunks, nxt % n_slots)
        for cp in copies(s, c, slot):
            cp.wait()
        kc = kc_buf[slot]
        kr_t = kr_buf[slot]
        sc = (lax.dot_general(ql, kc, _NT, preferred_element_type=F32)
              + jnp.dot(qr, kr_t, preferred_element_type=F32)) * scale
        m_new = jnp.maximum(m, jnp.max(sc, axis=1, keepdims=True))
        alpha = jnp.exp2(m - m_new)
        pr = jnp.exp2(sc - m_new)
        l = alpha * l + jnp.sum(pr, axis=1, keepdims=True)
        acc = alpha * acc + jnp.dot(pr, kc, preferred_element_type=F32)
        m = m_new
    o_ref[0] = acc / l


def _decode_attn(page_table, q_lat, q_rope, c_new, kr_new, cache_c, cache_kr_t, scale):
    ns, heads, kvl = q_lat.shape
    rope = q_rope.shape[-1]
    n_pages = page_table.shape[1]
    page = cache_c.shape[1]
    assert cache_kr_t.shape[1:] == (rope, page)
    chunk_pages = _tile(n_pages, 16, 1)
    n_chunks = n_pages // chunk_pages
    assert n_chunks % 2 == 0
    n_slots = 4 if n_chunks % 4 == 0 else 2
    keys = chunk_pages * page
    grid_spec = pltpu.PrefetchScalarGridSpec(
        num_scalar_prefetch=1,
        grid=(ns,),
        in_specs=[
            pl.BlockSpec((1, heads, kvl), lambda s, pt: (s, 0, 0)),
            pl.BlockSpec((1, heads, rope), lambda s, pt: (s, 0, 0)),
            pl.BlockSpec((1, 1, kvl), lambda s, pt: (s, 0, 0)),
            pl.BlockSpec((1, 1, rope), lambda s, pt: (s, 0, 0)),
            pl.BlockSpec(memory_space=pl.ANY),
            pl.BlockSpec(memory_space=pl.ANY),
        ],
        out_specs=pl.BlockSpec((1, heads, kvl), lambda s, pt: (s, 0, 0)),
        scratch_shapes=[
            pltpu.VMEM((n_slots, keys, kvl), cache_c.dtype),
            pltpu.VMEM((n_slots, rope, keys), cache_kr_t.dtype),
            pltpu.SemaphoreType.DMA((n_slots, 2)),
        ],
    )
    return pl.pallas_call(
        functools.partial(_decode_attn_kernel, chunk_pages=chunk_pages, n_chunks=n_chunks, n_slots=n_slots, page=page, scale=scale),
        grid_spec=grid_spec,
        out_shape=jax.ShapeDtypeStruct((ns, heads, kvl), F32),
        compiler_params=_cparams(("arbitrary",)),
        name="decode_attn",
    )(page_table, q_lat, q_rope, c_new.reshape(ns, 1, kvl), kr_new.reshape(ns, 1, rope), cache_c, cache_kr_t)


def _unabsorb_kernel(o_ref, w_ref, gate_ref, y_ref):
    y = jnp.dot(o_ref[0].astype(BF16), w_ref[0], preferred_element_type=F32)
    y_ref[...] = (y * jax.nn.sigmoid(gate_ref[...].astype(F32))).astype(y_ref.dtype)


def _unabsorb_o(o_lat_h, w_uv_h, zb, gate_off):
    heads, ns, kvl = o_lat_h.shape
    vh = w_uv_h.shape[-1]
    assert gate_off % vh == 0
    return pl.pallas_call(
        _unabsorb_kernel,
        grid=(heads,),
        in_specs=[
            pl.BlockSpec((1, ns, kvl), lambda h: (h, 0, 0)),
            pl.BlockSpec((1, kvl, vh), lambda h: (h, 0, 0)),
            pl.BlockSpec((ns, vh), lambda h: (0, gate_off // vh + h)),
        ],
        out_specs=pl.BlockSpec((ns, vh), lambda h: (0, h)),
        out_shape=jax.ShapeDtypeStruct((ns, heads * vh), F32),
        compiler_params=_cparams(("parallel",)),
        name="unabsorb_o",
    )(o_lat_h, w_uv_h, zb)


def _segment_mid(bc, s):
    c, dk = bc.shape
    if 2 * s >= 8:
        n = c // (2 * s)
        mid = bc.reshape(n, 2 * s, dk)[:, s - 1:s, :]
        return jnp.broadcast_to(mid, (n, 2 * s, dk)).reshape(c, dk)
    r = lax.broadcasted_iota(I32, bc.shape, 0) & (2 * s - 1)
    prev1 = pltpu.roll(bc, 1, 0)
    if s == 1:
        return jnp.where(r == 0, bc, prev1)
    assert s == 2
    return jnp.where(r == 0, pltpu.roll(bc, c - 1, 0), jnp.where(r == 1, bc, jnp.where(r == 2, prev1, pltpu.roll(bc, 2, 0))))


def _gla_prompt_kernel(q_ref, k_ref, v_ref, go_ref, gg_ref, lr_ref, wa_ref, ba_ref, gn_ref, y_ref, st_ref, st_scr, la_scr, *, n_chunks, c, scale):
    tb = pl.program_id(2)

    @pl.when(tb == 0)
    def _():
        st_scr[...] = jnp.zeros(st_scr.shape, F32)

    x = jnp.dot(lr_ref[...], wa_ref[...], preferred_element_type=F32, precision=HI) + ba_ref[...]
    la_scr[...] = _log_sigmoid(x) / GLA_TAU

    row = lax.broadcasted_iota(I32, (c, c), 0)
    col = lax.broadcasted_iota(I32, (c, c), 1)
    levels = []
    s = c // 2
    while s >= 1:
        levels.append(s)
        s //= 2
    tril = jnp.where(col <= row, 1.0, 0.0)
    masks = []
    for s in levels:
        sh = s.bit_length() - 1
        masks.append(((row >> (sh + 1)) == (col >> (sh + 1))) & (((row >> sh) & 1) == 1) & (((col >> sh) & 1) == 0))
    eye = row == col

    def chunk(ci, carry):
        r0 = pl.multiple_of(ci * c, c)
        q = q_ref[pl.ds(r0, c), :].astype(F32) * scale
        k = k_ref[pl.ds(r0, c), :].astype(F32)
        v = v_ref[pl.ds(r0, c), :]
        bc = jnp.dot(tril, la_scr[pl.ds(r0, c), :], preferred_element_type=F32, precision=HI)
        bl = bc[c - 1:c, :]
        att = jnp.where(eye, jnp.sum(q * k, axis=1, keepdims=True), 0.0)
        for li, s in enumerate(levels):
            ref = _segment_mid(bc, s)
            qh = (q * jnp.exp(jnp.minimum(bc - ref, 0.0))).astype(BF16)
            kh = (k * jnp.exp(jnp.minimum(ref - bc, 0.0))).astype(BF16)
            a = lax.dot_general(qh, kh, _NT, preferred_element_type=F32)
            att = att + jnp.where(masks[li], a, 0.0)
        st = st_scr[...]
        qt = (q * jnp.exp(bc)).astype(BF16)
        o = (lax.dot_general(qt, st.astype(BF16), _NT, preferred_element_type=F32)
             + jnp.dot(att.astype(BF16), v, preferred_element_type=F32))
        kb = (k * jnp.exp(bl - bc)).astype(BF16)
        st_scr[...] = st * jnp.exp(bl) + lax.dot_general(v, kb, _TN, preferred_element_type=F32)
        on = _rms(o, gn_ref[...])
        go = go_ref[pl.ds(r0, c), :].astype(F32)
        gg = gg_ref[pl.ds(r0, c), :].astype(F32)
        y_ref[pl.ds(r0, c), :] = (on * (go * jax.nn.sigmoid(go)) * jax.nn.sigmoid(gg)).astype(y_ref.dtype)
        return carry

    lax.fori_loop(0, n_chunks, chunk, 0, unroll=2 if n_chunks % 2 == 0 else 1)

    @pl.when(tb == pl.num_programs(2) - 1)
    def _():
        st_ref[0, 0] = st_scr[...].T


def _gla_prompt(zb, zs, offs, w_a2p, b_a2, g_gla, batch, seq, gh, dk, dv):
    n = batch * seq
    c = math.gcd(seq, GLA_CHUNK)
    tb = _tile(seq, 512, c)
    nb = seq // tb
    for key, w in (("gq", dk), ("gk", dk), ("gv", dv), ("g_out", dv), ("gate_gla", dv)):
        assert offs[key] % w == 0
    assert offs["g_lr"] % LANES == 0

    def rows(b, h, t):
        return b * nb + t

    in_specs = [
        pl.BlockSpec((tb, dk), lambda b, h, t: (rows(b, h, t), offs["gq"] // dk + h)),
        pl.BlockSpec((tb, dk), lambda b, h, t: (rows(b, h, t), offs["gk"] // dk + h)),
        pl.BlockSpec((tb, dv), lambda b, h, t: (rows(b, h, t), offs["gv"] // dv + h)),
        pl.BlockSpec((tb, dv), lambda b, h, t: (rows(b, h, t), offs["g_out"] // dv + h)),
        pl.BlockSpec((tb, dv), lambda b, h, t: (rows(b, h, t), offs["gate_gla"] // dv + h)),
        pl.BlockSpec((tb, LANES), lambda b, h, t: (rows(b, h, t), offs["g_lr"] // LANES)),
        pl.BlockSpec((LANES, dk), lambda b, h, t: (0, h)),
        pl.BlockSpec((1, dk), lambda b, h, t: (0, h)),
        pl.BlockSpec((1, dv), lambda b, h, t: (0, 0)),
    ]
    out_specs = [
        pl.BlockSpec((tb, dv), lambda b, h, t: (rows(b, h, t), h)),
        pl.BlockSpec((1, 1, dk, dv), lambda b, h, t: (b, h, 0, 0)),
    ]
    return pl.pallas_call(
        functools.partial(_gla_prompt_kernel, n_chunks=tb // c, c=c, scale=dk ** -0.5),
        grid=(batch, gh, nb),
        in_specs=in_specs,
        out_specs=out_specs,
        out_shape=[jax.ShapeDtypeStruct((n, gh * dv), BF16), jax.ShapeDtypeStruct((batch, gh, dk, dv), F32)],
        scratch_shapes=[pltpu.VMEM((dv, dk), F32), pltpu.VMEM((tb, dk), F32)],
        compiler_params=_cparams(("parallel", "parallel", "arbitrary")),
        name="gla_prompt",
    )(zb, zb, zb, zb, zb, zs, w_a2p, b_a2.reshape(1, gh * dk), g_gla.reshape(1, dv))


def _gla_decode_kernel(qk_ref, lrt_ref, wat_ref, bcol_ref, v_ref, go_ref, gg_ref, gn_ref, s0_ref, y_ref, s1_ref, *, sb, scale):
    x = jnp.dot(wat_ref[0], lrt_ref[0], preferred_element_type=F32, precision=HI) + bcol_ref[0]
    a = jnp.exp(_log_sigmoid(x) / GLA_TAU)
    outs = []
    for u in range(sb):
        qc = qk_ref[0, 0, :, u:u + 1] * scale
        kc = qk_ref[0, 0, :, sb + u:sb + u + 1]
        sn = a[:, u:u + 1] * s0_ref[u, 0] + kc * v_ref[u:u + 1, :]
        s1_ref[u, 0] = sn
        outs.append(jnp.sum(qc * sn, axis=0, keepdims=True))
    o = jnp.concatenate(outs, axis=0)
    go = go_ref[...]
    y_ref[...] = _rms(o, gn_ref[...]) * (go * jax.nn.sigmoid(go)) * jax.nn.sigmoid(gg_ref[...])


def _gla_decode(qk_cols, lr_t, wa_t, b_col, gv, g_out, gate_gla, g_gla, state):
    ns, gh, dk, dv = state.shape
    sb = qk_cols.shape[-1] // 2
    return pl.pallas_call(
        functools.partial(_gla_decode_kernel, sb=sb, scale=dk ** -0.5),
        grid=(ns // sb, gh),
        in_specs=[
            pl.BlockSpec((1, 1, dk, 2 * sb), lambda i, h: (i, h, 0, 0)),
            pl.BlockSpec((1, LANES, sb), lambda i, h: (i, 0, 0)),
            pl.BlockSpec((1, dk, LANES), lambda i, h: (h, 0, 0)),
            pl.BlockSpec((1, dk, 1), lambda i, h: (h, 0, 0)),
            pl.BlockSpec((sb, dv), lambda i, h: (i, h)),
            pl.BlockSpec((sb, dv), lambda i, h: (i, h)),
            pl.BlockSpec((sb, dv), lambda i, h: (i, h)),
            pl.BlockSpec((1, dv), lambda i, h: (0, 0)),
            pl.BlockSpec((sb, 1, dk, dv), lambda i, h: (i, h, 0, 0)),
        ],
        out_specs=[pl.BlockSpec((sb, dv), lambda i, h: (i, h)), pl.BlockSpec((sb, 1, dk, dv), lambda i, h: (i, h, 0, 0))],
        out_shape=[jax.ShapeDtypeStruct((ns, gh * dv), F32), jax.ShapeDtypeStruct((ns, gh, dk, dv), F32)],
        compiler_params=_cparams(("parallel", "parallel")),
        name="gla_decode",
    )(qk_cols, lr_t, wa_t, b_col, gv, g_out, gate_gla, g_gla.reshape(1, dv), state)


def _outproj_kernel(ym_ref, yg_ref, h_ref, w_ref, g_ref, wr_ref, br_ref, h1_ref, m_ref, route_ref):
    mix = (ym_ref[...].astype(F32) + yg_ref[...].astype(F32)).astype(BF16)
    h1 = h_ref[...] + jnp.dot(mix, w_ref[...], preferred_element_type=F32)
    h1_ref[...] = h1
    m = _rms(h1, g_ref[...])
    half = m.shape[1] // 2
    m_ref[...] = _pack_bf16_pair(m[:, :half], m[:, half:])
    logits = jnp.dot(m.astype(BF16), wr_ref[...].astype(BF16), preferred_element_type=F32) + br_ref[...]
    lane = lax.broadcasted_iota(I32, logits.shape, 1).astype(F32)
    vals, idxs = [], []
    for _ in range(TOP_K):
        mx = jnp.max(logits, axis=1, keepdims=True)
        ix = jnp.min(jnp.where(logits == mx, lane, float(LANES)), axis=1, keepdims=True)
        vals.append(mx)
        idxs.append(ix)
        logits = jnp.where(lane == ix, -jnp.inf, logits)
    ex = [jnp.exp(v - vals[0]) for v in vals]
    den = ex[0]
    for e in ex[1:]:
        den = den + e
    route = jnp.zeros(logits.shape, F32)
    for kk in range(TOP_K):
        route = jnp.where(lane == float(kk), idxs[kk], route)
        route = jnp.where(lane == float(TOP_K + kk), ex[kk] / den, route)
    route_ref[...] = route


def _outproj(y_mla, y_gla, h, w_out_b, g_ffn, w_router_p, b_router_p):
    n, d = h.shape
    tm = _tile(n, 256, 16)
    row = lambda i: (i, 0)
    fixed = lambda i: (0, 0)
    return pl.pallas_call(
        _outproj_kernel,
        grid=(n // tm,),
        in_specs=[
            pl.BlockSpec((tm, d), row), pl.BlockSpec((tm, d), row), pl.BlockSpec((tm, d), row),
            pl.BlockSpec((d, d), fixed), pl.BlockSpec((1, d), fixed),
            pl.BlockSpec((d, LANES), fixed), pl.BlockSpec((1, LANES), fixed),
        ],
        out_specs=[pl.BlockSpec((tm, d), row), pl.BlockSpec((tm, d // 2), row), pl.BlockSpec((tm, LANES), row)],
        out_shape=[jax.ShapeDtypeStruct((n, d), F32), jax.ShapeDtypeStruct((n, d // 2), U32), jax.ShapeDtypeStruct((n, LANES), F32)],
        compiler_params=_cparams(("parallel",)),
        name="outproj",
    )(y_mla, y_gla, h, w_out_b, g_ffn.reshape(1, d), w_router_p, b_router_p)


def _dispatch_kernel(idx_ref, x_ref, o_ref, g_scr, *, rows):
    base = pl.program_id(0) * rows

    def body(r, carry):
        tok = idx_ref[base + r]
        g_scr[pl.ds(r, 1), :] = x_ref[pl.ds(tok, 1), :]
        return carry

    lax.fori_loop(0, rows, body, 0, unroll=8)
    hi, lo = _unpack_bf16_pair(g_scr[...])
    o_ref[...] = jnp.concatenate([hi, lo], axis=1).astype(BF16)


def _dispatch(buf_tok, m_packed):
    rows_total = buf_tok.shape[0]
    n, w = m_packed.shape
    rows = _tile(rows_total, 512, 16)
    grid_spec = pltpu.PrefetchScalarGridSpec(
        num_scalar_prefetch=1,
        grid=(rows_total // rows,),
        in_specs=[pl.BlockSpec(memory_space=pltpu.VMEM)],
        out_specs=pl.BlockSpec((rows, 2 * w), lambda i, idx: (i, 0)),
        scratch_shapes=[pltpu.VMEM((rows, w), U32)],
    )
    return pl.pallas_call(
        functools.partial(_dispatch_kernel, rows=rows),
        grid_spec=grid_spec,
        out_shape=jax.ShapeDtypeStruct((rows_total, 2 * w), BF16),
        compiler_params=_cparams(("arbitrary",), 56),
        name="moe_dispatch",
    )(buf_tok, m_packed)


def _for_row_chunks(b0, b1, rb, max_blocks, fn):
    n = b1 - b0
    size = 1 << (max_blocks.bit_length() - 1)
    while size >= 1:
        start = b0 + (n & ~(2 * size - 1))

        @pl.when((n & size) != 0)
        def _(start=start, size=size):
            fn(pl.multiple_of(start * rb, rb), size * rb)

        size //= 2


def _moe_kernel(vt_ref, ve_ref, b0_ref, b1_ref, zf_ref, xs_ref, wg_ref, wu_ref, bg_ref, bu_ref, wd_ref, bd_ref, ys_ref,
                y_scr, *, n_ff, n_slab, dn, rb):
    v = pl.program_id(0)
    j = pl.program_id(1)
    b0 = b0_ref[v]
    b1 = b1_ref[v]
    fill = zf_ref[v] == 1
    work = jnp.logical_not(fill)
    max_blocks = xs_ref.shape[0] // rb
    d = y_scr.shape[1]
    slab = d // n_slab

    @pl.when(fill & (j == 0))
    def _():
        def blk(b, carry):
            r0 = pl.multiple_of(b * rb, rb)
            for s in range(n_slab):
                ys_ref[s, 0, pl.ds(r0, rb), :] = jnp.zeros((rb, ys_ref.shape[3]), U32)
            return carry
        lax.fori_loop(b0, b1, blk, 0)

    @pl.when(work & (j == 0))
    def _():
        def init(r0, rows):
            y_scr[pl.ds(r0, rows), :] = jnp.broadcast_to(bd_ref[0], (rows, d))
        _for_row_chunks(b0, b1, rb, max_blocks, init)

    @pl.when(work)
    def _():
        def mlp(r0, rows):
            x = xs_ref[pl.ds(r0, rows), :]
            g = jnp.dot(x, wg_ref[0].astype(BF16), preferred_element_type=F32) + bg_ref[0]
            u = jnp.dot(x, wu_ref[0].astype(BF16), preferred_element_type=F32) + bu_ref[0]
            gate = jnp.minimum(g, SWIGLU_LIMIT)
            up = jnp.clip(u, -SWIGLU_LIMIT, SWIGLU_LIMIT)
            glu = gate * jax.nn.sigmoid(gate * SWIGLU_ALPHA)
            a = ((up + 1.0) * glu).astype(BF16)
            for c0 in range(0, d, dn):
                w = wd_ref[0, :, c0:c0 + dn].astype(BF16)
                y_scr[pl.ds(r0, rows), c0:c0 + dn] += jnp.dot(a, w, preferred_element_type=F32)
        _for_row_chunks(b0, b1, rb, max_blocks, mlp)

    @pl.when(work & (j == n_ff - 1))
    def _():
        def emit(r0, rows):
            for s in range(n_slab):
                y = y_scr[pl.ds(r0, rows), s * slab:(s + 1) * slab]
                ys_ref[s, 0, pl.ds(r0, rows), :] = _pack_bf16_pair(y[:, :slab // 2], y[:, slab // 2:])
        _for_row_chunks(b0, b1, rb, max_blocks, emit)


def _moe_experts(meta, xs, w_gu, b_gu, w_down, b_down, tile_rows):
    v_tile, v_exp, v_b0, v_b1, v_zero = meta
    rows_total, d = xs.shape
    e, _, f2 = w_gu.shape
    ff = f2 // 2
    tf = _tile(ff, 256, LANES)
    n_ff = ff // tf
    n_slab = d // _tile(d, MOE_SLAB, 2 * LANES)
    n_tiles = rows_total // tile_rows
    n_visits = v_tile.shape[0]
    ff_idx = lambda j, zf: jnp.where(zf == 1, n_ff - 1, j)
    weights = {}
    grid_spec = pltpu.PrefetchScalarGridSpec(
        num_scalar_prefetch=5,
        grid=(n_visits, n_ff),
        in_specs=[
            pl.BlockSpec((tile_rows, d), lambda v, j, vt, ve, b0, b1, zf: (vt[v], 0)),
            pl.BlockSpec((1, d, tf), lambda v, j, vt, ve, b0, b1, zf: (ve[v], 0, ff_idx(j, zf[v])), **weights),
            pl.BlockSpec((1, d, tf), lambda v, j, vt, ve, b0, b1, zf: (ve[v], 0, n_ff + ff_idx(j, zf[v])), **weights),
            pl.BlockSpec((1, 1, tf), lambda v, j, vt, ve, b0, b1, zf: (ve[v], 0, ff_idx(j, zf[v]))),
            pl.BlockSpec((1, 1, tf), lambda v, j, vt, ve, b0, b1, zf: (ve[v], 0, n_ff + ff_idx(j, zf[v]))),
            pl.BlockSpec((1, tf, d), lambda v, j, vt, ve, b0, b1, zf: (ve[v], ff_idx(j, zf[v]), 0), **weights),
            pl.BlockSpec((1, 1, d), lambda v, j, vt, ve, b0, b1, zf: (ve[v], 0, 0)),
        ],
        out_specs=pl.BlockSpec((n_slab, 1, tile_rows, d // n_slab // 2), lambda v, j, vt, ve, b0, b1, zf: (0, vt[v], 0, 0)),
        scratch_shapes=[pltpu.VMEM((tile_rows, d), F32)],
    )
    return pl.pallas_call(
        functools.partial(_moe_kernel, n_ff=n_ff, n_slab=n_slab, dn=_tile(d, 512, LANES), rb=EXPERT_ROW_BLOCK),
        grid_spec=grid_spec,
        out_shape=jax.ShapeDtypeStruct((n_slab, n_tiles, tile_rows, d // n_slab // 2), U32),
        compiler_params=_cparams(("arbitrary", "arbitrary"), 56),
        name="moe_experts",
    )(v_tile, v_exp, v_b0, v_b1, v_zero, xs, w_gu, w_gu, b_gu.reshape(e, 1, f2), b_gu.reshape(e, 1, f2), w_down, b_down.reshape(e, 1, d))


def _combine_kernel(dest_ref, ys_ref, route_ref, o_ref, stage, *, tm):
    base = pl.program_id(1) * tm

    def body(t, carry):
        for kk in range(TOP_K):
            r = dest_ref[(base + t) * TOP_K + kk]
            stage[kk, pl.ds(t, 1), :] = ys_ref[0, pl.ds(r, 1), :]
        return carry

    lax.fori_loop(0, tm, body, 0, unroll=4)
    acc_hi = acc_lo = None
    for kk in range(TOP_K):
        g = route_ref[:, TOP_K + kk:TOP_K + kk + 1]
        hi, lo = _unpack_bf16_pair(stage[kk])
        acc_hi = g * hi if acc_hi is None else acc_hi + g * hi
        acc_lo = g * lo if acc_lo is None else acc_lo + g * lo
    o_ref[...] = jnp.concatenate([acc_hi, acc_lo], axis=1)


def _combine(dest, ys, route):
    n_down, rows_total, wh = ys.shape
    n = route.shape[0]
    tm = _tile(n, 256, 8)
    grid_spec = pltpu.PrefetchScalarGridSpec(
        num_scalar_prefetch=1,
        grid=(n_down, n // tm),
        in_specs=[
            pl.BlockSpec((1, rows_total, wh), lambda c, i, dest: (c, 0, 0)),
            pl.BlockSpec((tm, LANES), lambda c, i, dest: (i, 0)),
        ],
        out_specs=pl.BlockSpec((tm, 2 * wh), lambda c, i, dest: (i, c)),
        scratch_shapes=[pltpu.VMEM((TOP_K, tm, wh), U32)],
    )
    return pl.pallas_call(
        functools.partial(_combine_kernel, tm=tm),
        grid_spec=grid_spec,
        out_shape=jax.ShapeDtypeStruct((n, n_down * 2 * wh), F32),
        compiler_params=_cparams(("arbitrary", "arbitrary"), 56),
        name="moe_combine",
    )(dest, ys, route)


def _route_meta(top_i, n_experts, tile_rows):
    n_tok = top_i.shape[0]
    n_asg = n_tok * TOP_K
    rb = EXPERT_ROW_BLOCK
    e_flat = top_i.reshape(n_asg)
    tok_flat = (jnp.arange(n_asg, dtype=I32) // TOP_K).astype(I32)
    onehot = jax.nn.one_hot(e_flat, n_experts, dtype=I32)
    rank = jnp.take_along_axis(jnp.cumsum(onehot, axis=0), e_flat[:, None], axis=1)[:, 0] - 1
    counts = onehot.sum(axis=0)
    padded = (counts + rb - 1) // rb * rb
    ends = jnp.cumsum(padded)
    starts = ends - padded
    dest = (starts[e_flat] + rank).astype(I32)
    rows_total = (-(-n_asg // rb) + n_experts) * rb
    rows_total = -(-rows_total // tile_rows) * tile_rows
    buf_tok = jnp.zeros((rows_total,), I32).at[dest].set(tok_flat, unique_indices=True, mode="promise_in_bounds")
    n_tiles = rows_total // tile_rows
    total = ends[-1].astype(I32)
    bp = jnp.sort(jnp.concatenate([jnp.arange(n_tiles, dtype=I32) * tile_rows, starts.astype(I32), total[None]]))
    lo = bp
    length = jnp.concatenate([bp[1:], jnp.full((1,), rows_total, I32)]) - lo
    valid = length > 0
    order = jnp.argsort(jnp.logical_not(valid), stable=True)
    lo, length = lo[order], length[order]
    n_valid = jnp.sum(valid.astype(I32))
    pos = jnp.arange(lo.shape[0], dtype=I32)
    src = jnp.minimum(pos, n_valid - 1)
    lo_s = lo[src]
    live = pos < n_valid
    v_tile = (lo_s // tile_rows).astype(I32)
    v_exp = jnp.minimum(jnp.searchsorted(ends, jnp.minimum(lo_s, total - 1), side="right"), n_experts - 1).astype(I32)
    v_b0 = jnp.where(live, (lo_s % tile_rows) // rb, 0).astype(I32)
    v_b1 = jnp.where(live, v_b0 + length[src] // rb, 0).astype(I32)
    v_zero = ((lo_s >= total) | jnp.logical_not(live)).astype(I32)
    return buf_tok, dest, (v_tile, v_exp, v_b0, v_b1, v_zero)


def _ple_kernel(h1_ref, moe_ref, p_ref, wp_ref, wg_ref, gp_ref, gf_ref, y_ref, *, final):
    h2 = h1_ref[...] + moe_ref[...]
    u = _rms(h2, gp_ref[...]).astype(BF16)
    gate = jax.nn.sigmoid(jnp.dot(u, wg_ref[...], preferred_element_type=F32))
    pe = jnp.dot(p_ref[...].astype(BF16), wp_ref[...], preferred_element_type=F32)
    y = h2 + pe * gate
    y_ref[...] = _rms(y, gf_ref[...]) if final else y


def _ple_final(h1, moe, row_off, p, w_ple_b, w_ple_gate_b, g_ple, g_final, final):
    n, d = h1.shape
    pd = p.shape[1]
    tm = _tile(n, 256, 8)
    assert row_off % tm == 0
    row = lambda i: (i, 0)
    fixed = lambda i: (0, 0)
    return pl.pallas_call(
        functools.partial(_ple_kernel, final=final),
        grid=(n // tm,),
        in_specs=[
            pl.BlockSpec((tm, d), row), pl.BlockSpec((tm, d), lambda i: (row_off // tm + i, 0)), pl.BlockSpec((tm, pd), row),
            pl.BlockSpec((pd, d), fixed), pl.BlockSpec((d, d), fixed), pl.BlockSpec((1, d), fixed), pl.BlockSpec((1, d), fixed),
        ],
        out_specs=pl.BlockSpec((tm, d), row),
        out_shape=jax.ShapeDtypeStruct((n, d), F32),
        compiler_params=_cparams(("parallel",)),
        name="ple_final",
    )(h1, moe, p, w_ple_b, w_ple_gate_b, g_ple.reshape(1, d), g_final.reshape(1, d))


def _rope_table(pos, rope):
    half = rope // 2
    inv_freq = jnp.power(ROPE_THETA, -jnp.arange(half, dtype=F32) / half)
    ang = pos.astype(F32)[:, None] * inv_freq[None, :]
    cos, sin = jnp.cos(ang), jnp.sin(ang)
    return jnp.concatenate([cos, cos, -sin, sin], axis=1)


def _swap_halves(w):
    half = w.shape[-1] // 2
    return jnp.concatenate([w[..., half:], w[..., :half]], axis=-1)


def kernel(x_prompt, x_sample, cache_kv_latent, cache_k_rope, state_gla, page_table, p_prompt, p_sample, g_mix, w_in, g_q_a, w_uq, g_kv_a, w_uk, w_uv, w_a2, b_a2, g_gla, w_out, g_ffn, w_router, b_router, w_gu, b_gu, w_down, b_down, g_ple, w_ple, w_ple_gate, g_final):
    depth = w_in.shape[0]
    batch, seq, d = x_prompt.shape
    ns, dec_seq, _ = x_sample.shape
    assert dec_seq == 1
    ql, kvl = g_q_a.shape[-1], g_kv_a.shape[-1]
    rope = cache_k_rope.shape[-1]
    heads, nope = w_uk.shape[2], w_uk.shape[3]
    vh = w_uv.shape[3]
    _, _, gh, dk, dv = state_gla.shape
    lr = w_a2.shape[1]
    n_experts = w_router.shape[-1]
    past = page_table.shape[1] * cache_kv_latent.shape[2]
    assert 2 * rope == LANES and nope == LANES and heads * vh == d and lr <= LANES and n_experts <= LANES
    scale = LOG2E / math.sqrt(nope + rope)
    n_p = batch * seq
    tile_rows = MOE_TILE_ROWS

    big_segs = (("gv", gh * dv), ("g_out", gh * dv), ("gate_mla", d), ("gate_gla", d), ("gq", gh * dk), ("gk", gh * dk))
    offs, o = {}, 0
    for key, w in big_segs:
        offs[key] = o
        o += w
    offs["g_lr"] = ql + kvl + 2 * rope
    in_splits = (ql, kvl, rope, gh * dk, gh * dk, gh * dv, gh * dv, lr, d, d)
    in_offsets = tuple(sum(in_splits[:i + 1]) for i in range(len(in_splits) - 1))

    tab_p = jnp.tile(_rope_table(jnp.arange(seq), rope), (batch, 1))
    tab_s = _rope_table(jnp.full((ns,), past, I32), rope)

    h_p = x_prompt.reshape(n_p, d)
    h_s = x_sample.reshape(ns, d)
    kv_p, kr_p, st_p, kv_s, kr_s, st_s = [], [], [], [], [], []
    for l in range(depth):
        seg = dict(zip(("c_q", "c_kv", "k_r", "gq", "gk", "gv", "g_out", "g_lr", "gate_mla", "gate_gla"), jnp.split(w_in[l], in_offsets, axis=1)))
        w_big = jnp.concatenate([seg[k] for k, _ in big_segs], axis=1).astype(BF16)
        w_small = jnp.concatenate([seg["c_q"], seg["c_kv"], seg["k_r"], _swap_halves(seg["k_r"]),
                                   jnp.pad(seg["g_lr"], ((0, 0), (0, LANES - lr)))], axis=1).astype(BF16)
        wq = w_uq[l].reshape(ql, heads, nope + rope)
        w_uq_p = jnp.concatenate([wq[..., :nope], wq[..., nope:], _swap_halves(wq[..., nope:])], axis=-1).transpose(1, 0, 2).astype(BF16)
        w_uk_h = w_uk[l].transpose(1, 0, 2).astype(BF16)
        w_uv_h = w_uv[l].transpose(1, 0, 2).astype(BF16)
        w_uk_f = w_uk[l].reshape(kvl, heads * nope).astype(BF16)
        w_uv_tf = w_uv[l].reshape(kvl, heads * vh).T.astype(BF16)
        w_a2p = jnp.pad(w_a2[l], ((0, LANES - lr), (0, 0)))
        w_out_b = w_out[l].astype(BF16)
        w_router_p = jnp.pad(w_router[l], ((0, 0), (0, LANES - n_experts)))
        b_router_p = jnp.pad(b_router[l], (0, LANES - n_experts), constant_values=NEG_BIG).reshape(1, LANES)
        w_ple_b = w_ple[l].astype(BF16)
        w_ple_gate_b = w_ple_gate[l].astype(BF16)

        def front(h, tab, z_dtype, q_scale):
            a = _rms_cast(h, g_mix[l])
            zb = _matmul(a, w_big, z_dtype, "inproj_big")
            zs = _matmul(a, w_small, F32, "inproj_small")
            q = _q_proj(zs, g_q_a[l], w_uq_p, tab, q_scale)
            return zb, zs, q

        zb, zs, q = front(h_p, tab_p, BF16, scale)
        c_kv, k_r, k_heads, v_t = _kv_prep(zs, g_kv_a[l], tab_p, w_uk_f, w_uv_tf, heads, ql, rope, True)
        y_mla = _flash_prompt(q, k_heads, v_t, zb, offs["gate_mla"], batch, seq)
        y_gla, s_new = _gla_prompt(zb, zs, offs, w_a2p, b_a2[l], g_gla[l], batch, seq, gh, dk, dv)
        h1_p, m_p, route_p = _outproj(y_mla, y_gla, h_p, w_out_b, g_ffn[l], w_router_p, b_router_p)
        kv_p.append(c_kv.reshape(batch, seq, kvl)); kr_p.append(k_r.reshape(batch, seq, rope)); st_p.append(s_new)

        zb, zs, q = front(h_s, tab_s, F32, 1.0)
        c_kv, k_r = _kv_prep(zs, g_kv_a[l], tab_s, w_uk_f, w_uv_tf, heads, ql, rope, False)
        q_lat = _absorb_q(q, w_uk_h).transpose(1, 0, 2)
        q_rope = q[:, :, nope:nope + rope].astype(F32).transpose(1, 0, 2)
        o_lat = _decode_attn(page_table, q_lat, q_rope, c_kv, k_r, cache_kv_latent[l], cache_k_rope[l].transpose(0, 2, 1), scale)
        y_mla = _unabsorb_o(o_lat.transpose(1, 0, 2), w_uv_h, zb, offs["gate_mla"])
        sb = 8
        zf = zb.astype(F32)
        cols = lambda key, w: zf[:, offs[key]:offs[key] + w]
        to_cols = lambda z: z.reshape(ns // sb, sb, gh, dk).transpose(0, 2, 3, 1)
        qk_cols = jnp.concatenate([to_cols(cols("gq", gh * dk)), to_cols(cols("gk", gh * dk))], axis=-1)
        lr_t = zs[:, offs["g_lr"]:offs["g_lr"] + LANES].reshape(ns // sb, sb, LANES).transpose(0, 2, 1)
        wa_t = w_a2p.reshape(LANES, gh, dk).transpose(1, 2, 0)
        b_col = b_a2[l].reshape(gh, dk, 1)
        y_gla, s_new = _gla_decode(qk_cols, lr_t, wa_t, b_col, cols("gv", gh * dv), cols("g_out", gh * dv),
                                   cols("gate_gla", gh * dv), g_gla[l], state_gla[l])
        h1_s, m_s, route_s = _outproj(y_mla, y_gla, h_s, w_out_b, g_ffn[l], w_router_p, b_router_p)
        kv_s.append(c_kv.reshape(ns, 1, kvl)); kr_s.append(k_r.reshape(ns, 1, rope)); st_s.append(s_new)

        m_all = jnp.concatenate([m_p, m_s], axis=0)
        route = jnp.concatenate([route_p, route_s], axis=0)
        top_i = route[:, :TOP_K].astype(I32)
        buf_tok, dest, meta = _route_meta(top_i, n_experts, tile_rows)
        xs = _dispatch(buf_tok, m_all)
        ys = _moe_experts(meta, xs, w_gu[l], b_gu[l], w_down[l], b_down[l], tile_rows)
        ys = ys.reshape(ys.shape[0], ys.shape[1] * ys.shape[2], ys.shape[3])
        moe = _combine(dest, ys, route)

        last = l == depth - 1
        h_p = _ple_final(h1_p, moe, 0, p_prompt[l].reshape(n_p, -1), w_ple_b, w_ple_gate_b, g_ple[l], g_final, last)
        h_s = _ple_final(h1_s, moe, n_p, p_sample[l].reshape(ns, -1), w_ple_b, w_ple_gate_b, g_ple[l], g_final, last)

    return (h_p.reshape(batch, seq, d), h_s.reshape(ns, 1, d), jnp.stack(kv_p), jnp.stack(kr_p), jnp.stack(st_p),
            jnp.stack(kv_s), jnp.stack(kr_s), jnp.stack(st_s))
```

```python
import functools
import math

import jax
import jax.numpy as jnp
from jax import lax
from jax.experimental import pallas as pl
from jax.experimental.pallas import tpu as pltpu

F32 = jnp.float32
BF16 = jnp.bfloat16
U32 = jnp.uint32
I32 = jnp.int32

EPS = 1e-6
ROPE_THETA = 10000.0
GLA_TAU = 16.0
GLA_CHUNK = 64
TOP_K = 4
SWIGLU_LIMIT = 7.0
SWIGLU_ALPHA = 1.702
EXPERT_ROW_BLOCK = 128
MOE_SLAB = 256
MOE_WINDOW_ROWS = 1536
LANES = 128
MIB = 1024 * 1024
SUM_ROWS = 16
LOG2E = 1.4426950408889634
HI = lax.Precision.HIGHEST
NEG_BIG = -1e30

_NT = (((1,), (1,)), ((), ()))
_TN = (((0,), (0,)), ((), ()))


def _cparams(sem, vmem_mib=48):
    return pltpu.CompilerParams(dimension_semantics=sem, vmem_limit_bytes=vmem_mib * MIB)


def _tile(n, cap, mult):
    best = None
    for t in range(mult, min(n, cap) + 1, mult):
        if n % t == 0:
            best = t
    assert best is not None, (n, cap, mult)
    return best


def _rms(x, g):
    return x * lax.rsqrt(jnp.mean(x * x, axis=-1, keepdims=True) + EPS) * g


def _log_sigmoid(x):
    return jnp.minimum(x, 0.0) - jnp.log1p(jnp.exp(-jnp.abs(x)))


def _pack_bf16_pair(hi, lo):
    hb = pltpu.bitcast(hi.astype(BF16).astype(F32), U32)
    lb = pltpu.bitcast(lo.astype(BF16).astype(F32), U32)
    return hb | (lb >> 16)


def _unpack_bf16_pair(w):
    hi = pltpu.bitcast(w & jnp.uint32(0xFFFF0000), F32)
    lo = pltpu.bitcast(w << 16, F32)
    return hi, lo


def _rms_cast_kernel(x_ref, g_ref, o_ref):
    o_ref[...] = _rms(x_ref[...], g_ref[...]).astype(o_ref.dtype)


def _rms_cast(x, g):
    n, d = x.shape
    tm = _tile(n, 512, 16)
    return pl.pallas_call(
        _rms_cast_kernel,
        grid=(n // tm,),
        in_specs=[pl.BlockSpec((tm, d), lambda i: (i, 0)), pl.BlockSpec((1, d), lambda i: (0, 0))],
        out_specs=pl.BlockSpec((tm, d), lambda i: (i, 0)),
        out_shape=jax.ShapeDtypeStruct((n, d), BF16),
        compiler_params=_cparams(("parallel",)),
        name="rms_cast",
    )(x, g.reshape(1, d))


def _mm_kernel(x_ref, w_ref, o_ref):
    o_ref[...] = jnp.dot(x_ref[...], w_ref[...], preferred_element_type=F32).astype(o_ref.dtype)


def _matmul(x, w, out_dtype, name):
    m, k = x.shape
    n = w.shape[1]
    tm = _tile(m, 1024, 16)
    tn = _tile(n, 1280, LANES)
    return pl.pallas_call(
        _mm_kernel,
        grid=(n // tn, m // tm),
        in_specs=[pl.BlockSpec((tm, k), lambda j, i: (i, 0)), pl.BlockSpec((k, tn), lambda j, i: (0, j))],
        out_specs=pl.BlockSpec((tm, tn), lambda j, i: (i, j)),
        out_shape=jax.ShapeDtypeStruct((m, n), out_dtype),
        compiler_params=_cparams(("parallel", "parallel")),
        name=name,
    )(x, w)


def _rope_combine(y2, tab):
    half = y2.shape[1] // 2
    t = y2 * tab
    r = t + pltpu.roll(t, half, axis=1)
    lane = lax.broadcasted_iota(I32, r.shape, 1)
    return jnp.where(lane < half, r, 0.0)


def _q_proj_kernel(cq_ref, g_ref, w_ref, tab_ref, q_ref, *, heads, nope, scale):
    cqn = _rms(cq_ref[...], g_ref[...]).astype(BF16)
    tab = tab_ref[...]
    for h in range(heads):
        y = jnp.dot(cqn, w_ref[h], preferred_element_type=F32)
        q_ref[h, :, :nope] = (y[:, :nope] * scale).astype(BF16)
        q_ref[h, :, nope:] = (_rope_combine(y[:, nope:], tab) * scale).astype(BF16)


def _q_proj(zs, g_q_a, w_uq_p, tab, scale):
    n = zs.shape[0]
    heads, ql, width = w_uq_p.shape
    tm = _tile(n, 512, 16)
    return pl.pallas_call(
        functools.partial(_q_proj_kernel, heads=heads, nope=width // 2, scale=scale),
        grid=(n // tm,),
        in_specs=[
            pl.BlockSpec((tm, ql), lambda i: (i, 0)),
            pl.BlockSpec((1, ql), lambda i: (0, 0)),
            pl.BlockSpec((heads, ql, width), lambda i: (0, 0, 0)),
            pl.BlockSpec((tm, LANES), lambda i: (i, 0)),
        ],
        out_specs=pl.BlockSpec((heads, tm, width), lambda i: (0, i, 0)),
        out_shape=jax.ShapeDtypeStruct((heads, n, width), BF16),
        compiler_params=_cparams(("parallel",)),
        name="q_proj",
    )(zs, g_q_a.reshape(1, ql), w_uq_p, tab)


def _kv_prep_kernel(ckv_ref, krs_ref, g_ref, tab_ref, wuk_ref, wuvt_ref, c_ref, kr_ref, k_ref, vt_ref, *, heads, nope, rope, vh):
    c = _rms(ckv_ref[...], g_ref[...])
    c_ref[...] = c
    cb = c.astype(BF16)
    r = _rope_combine(krs_ref[...], tab_ref[...])
    kr_ref[...] = r[:, :rope]
    rb = r.astype(BF16)
    kn = jnp.dot(cb, wuk_ref[...], preferred_element_type=F32).astype(BF16)
    vt = lax.dot_general(wuvt_ref[...], cb, _NT, preferred_element_type=F32).astype(BF16)
    for h in range(heads):
        k_ref[h, :, :nope] = kn[:, h * nope:(h + 1) * nope]
        k_ref[h, :, nope:] = rb
        vt_ref[h, :vh, :] = vt[h * vh:(h + 1) * vh, :]
        vt_ref[h, vh:, :] = jnp.ones((SUM_ROWS, cb.shape[0]), BF16)


def _latent_kernel(ckv_ref, krs_ref, g_ref, tab_ref, c_ref, kr_ref, *, rope):
    c_ref[...] = _rms(ckv_ref[...], g_ref[...])
    kr_ref[...] = _rope_combine(krs_ref[...], tab_ref[...])[:, :rope]


def _kv_prep(zs, g_kv_a, tab, w_uk_f, w_uv_tf, heads, ql, rope, with_kv):
    n = zs.shape[0]
    kvl = w_uk_f.shape[0]
    nope = w_uk_f.shape[1] // heads
    vh = w_uv_tf.shape[0] // heads
    assert ql % kvl == 0 and (ql + kvl) % LANES == 0
    tm = _tile(n, 640, LANES if with_kv else 8)
    in_specs = [
        pl.BlockSpec((tm, kvl), lambda i: (i, ql // kvl)),
        pl.BlockSpec((tm, LANES), lambda i: (i, (ql + kvl) // LANES)),
        pl.BlockSpec((1, kvl), lambda i: (0, 0)),
        pl.BlockSpec((tm, LANES), lambda i: (i, 0)),
    ]
    out_specs = [pl.BlockSpec((tm, kvl), lambda i: (i, 0)), pl.BlockSpec((tm, rope), lambda i: (i, 0))]
    out_shape = [jax.ShapeDtypeStruct((n, kvl), F32), jax.ShapeDtypeStruct((n, rope), F32)]
    args = [zs, zs, g_kv_a.reshape(1, kvl), tab]
    if not with_kv:
        return pl.pallas_call(
            functools.partial(_latent_kernel, rope=rope),
            grid=(n // tm,), in_specs=in_specs, out_specs=out_specs, out_shape=out_shape,
            compiler_params=_cparams(("parallel",)), name="latent",
        )(*args)
    in_specs += [pl.BlockSpec((kvl, heads * nope), lambda i: (0, 0)), pl.BlockSpec((heads * vh, kvl), lambda i: (0, 0))]
    out_specs += [pl.BlockSpec((heads, tm, 2 * nope), lambda i: (0, i, 0)), pl.BlockSpec((heads, vh + SUM_ROWS, tm), lambda i: (0, 0, i))]
    out_shape += [jax.ShapeDtypeStruct((heads, n, 2 * nope), BF16), jax.ShapeDtypeStruct((heads, vh + SUM_ROWS, n), BF16)]
    return pl.pallas_call(
        functools.partial(_kv_prep_kernel, heads=heads, nope=nope, rope=rope, vh=vh),
        grid=(n // tm,), in_specs=in_specs, out_specs=out_specs, out_shape=out_shape,
        compiler_params=_cparams(("parallel",)), name="kv_prep",
    )(*args, w_uk_f, w_uv_tf)


def _flash_kernel(qi_ref, ki_ref, q_ref, k_ref, vt_ref, gate_ref, o_ref, m_scr, acc_scr, s_scr, *, heads, vh):
    p = pl.program_id(1)
    qi = qi_ref[p]
    ki = ki_ref[p]

    @pl.when(ki == 0)
    def _():
        m_scr[...] = jnp.full(m_scr.shape, -jnp.inf, F32)
        acc_scr[...] = jnp.zeros(acc_scr.shape, F32)

    def scores(h, slot):
        s_scr[slot] = lax.dot_general(k_ref[h], q_ref[h], _NT, preferred_element_type=F32)

    def update(h, slot, masked):
        s = s_scr[slot]
        if masked:
            kpos = lax.broadcasted_iota(I32, s.shape, 0)
            qpos = lax.broadcasted_iota(I32, s.shape, 1)
            s = jnp.where(kpos <= qpos, s, -jnp.inf)
        m_prev = m_scr[h]
        m_new = jnp.maximum(m_prev, jnp.max(s, axis=0, keepdims=True))
        pr = jnp.exp2(s - m_new).astype(BF16)
        acc_scr[h] = jnp.exp2(m_prev - m_new) * acc_scr[h] + jnp.dot(vt_ref[h], pr, preferred_element_type=F32)
        m_scr[h] = m_new

    def run(masked):
        scores(0, 0)

        def pair(i, carry):
            h0 = 2 * i
            scores(h0 + 1, 1)
            update(h0, 0, masked)
            scores(jnp.minimum(h0 + 2, heads - 1), 0)
            update(h0 + 1, 1, masked)
            return carry

        lax.fori_loop(0, heads // 2, pair, 0)

    @pl.when(ki < qi)
    def _():
        run(False)

    @pl.when(ki == qi)
    def _():
        run(True)
        for h in range(heads):
            acc = acc_scr[h]
            o = (acc[:vh] / acc[vh:vh + 1]).T
            g = jax.nn.sigmoid(gate_ref[:, h * vh:(h + 1) * vh].astype(F32))
            o_ref[:, h * vh:(h + 1) * vh] = (o * g).astype(o_ref.dtype)


def _flash_prompt(q, k, vt, zb, gate_off, batch, seq):
    heads, n, width = q.shape
    vr = vt.shape[1]
    vh = vr - SUM_ROWS
    d = heads * vh
    assert heads % 2 == 0
    tq = _tile(seq, 512, LANES)
    nq = seq // tq
    pairs = [(a, b) for a in range(nq) for b in range(a + 1)]
    qi = jnp.asarray([a for a, _ in pairs], I32)
    ki = jnp.asarray([b for _, b in pairs], I32)
    assert gate_off % d == 0
    grid_spec = pltpu.PrefetchScalarGridSpec(
        num_scalar_prefetch=2,
        grid=(batch, len(pairs)),
        in_specs=[
            pl.BlockSpec((heads, tq, width), lambda b, p, qi, ki: (0, b * nq + qi[p], 0)),
            pl.BlockSpec((heads, tq, width), lambda b, p, qi, ki: (0, b * nq + ki[p], 0)),
            pl.BlockSpec((heads, vr, tq), lambda b, p, qi, ki: (0, 0, b * nq + ki[p])),
            pl.BlockSpec((tq, d), lambda b, p, qi, ki: (b * nq + qi[p], gate_off // d)),
        ],
        out_specs=pl.BlockSpec((tq, d), lambda b, p, qi, ki: (b * nq + qi[p], 0)),
        scratch_shapes=[pltpu.VMEM((heads, 1, tq), F32), pltpu.VMEM((heads, vr, tq), F32), pltpu.VMEM((2, tq, tq), F32)],
    )
    return pl.pallas_call(
        functools.partial(_flash_kernel, heads=heads, vh=vh),
        grid_spec=grid_spec,
        out_shape=jax.ShapeDtypeStruct((n, d), BF16),
        compiler_params=_cparams(("parallel", "arbitrary")),
        name="flash_prompt",
    )(qi, ki, q, k, vt, zb)


def _absorb_kernel(q_ref, w_ref, o_ref, *, nope):
    o_ref[0] = lax.dot_general(q_ref[0, :, :nope], w_ref[0], _NT, preferred_element_type=F32)


def _absorb_q(qs, w_uk_h):
    heads, ns, width = qs.shape
    _, kvl, nope = w_uk_h.shape
    return pl.pallas_call(
        functools.partial(_absorb_kernel, nope=nope),
        grid=(heads,),
        in_specs=[pl.BlockSpec((1, ns, width), lambda h: (h, 0, 0)), pl.BlockSpec((1, kvl, nope), lambda h: (h, 0, 0))],
        out_specs=pl.BlockSpec((1, ns, kvl), lambda h: (h, 0, 0)),
        out_shape=jax.ShapeDtypeStruct((heads, ns, kvl), F32),
        compiler_params=_cparams(("parallel",)),
        name="absorb_q",
    )(qs, w_uk_h)


def _decode_attn_kernel(pt_ref, ql_ref, qr_ref, cn_ref, krn_ref, cc_hbm, ckr_hbm, o_ref, kc_buf, kr_buf, sem, *, chunk_pages, n_chunks, n_slots, page, scale):
    s = pl.program_id(0)
    n_seq = pl.num_programs(0)

    def copies(seq, chunk, slot):
        out = []
        for j in range(chunk_pages):
            pg = pt_ref[seq, chunk * chunk_pages + j]
            out.append(pltpu.make_async_copy(cc_hbm.at[pg], kc_buf.at[slot, pl.ds(j * page, page)], sem.at[slot, 0]))
            out.append(pltpu.make_async_copy(ckr_hbm.at[pg], kr_buf.at[slot, :, pl.ds(j * page, page)], sem.at[slot, 1]))
        return out

    def start(seq, chunk, slot):
        for c in copies(seq, chunk, slot):
            c.start()

    ahead = n_slots - 1

    @pl.when(s == 0)
    def _():
        for c in range(ahead):
            start(0, c, c)

    ql = ql_ref[0]
    qr = qr_ref[0]
    cn = cn_ref[0]
    krn = krn_ref[0]
    m = (jnp.sum(ql * cn, axis=1, keepdims=True) + jnp.sum(qr * krn, axis=1, keepdims=True)) * scale
    l = jnp.ones_like(m)
    acc = jnp.broadcast_to(cn, ql.shape)
    for c in range(n_chunks):
        slot = c % n_slots
        nxt = c + ahead
        if nxt < n_chunks:
            start(s, nxt, nxt % n_slots)
        else:
            @pl.when(s + 1 < n_seq)
            def _(nxt=nxt):
                start(s + 1, nxt - n_chunks, nxt % n_slots)
        for cp in copies(s, c, slot):
            cp.wait()
        kc = kc_buf[slot]
        kr_t = kr_buf[slot]
        sc = (lax.dot_general(ql, kc, _NT, preferred_element_type=F32)
              + jnp.dot(qr, kr_t, preferred_element_type=F32)) * scale
        m_new = jnp.maximum(m, jnp.max(sc, axis=1, keepdims=True))
        alpha = jnp.exp2(m - m_new)
        pr = jnp.exp2(sc - m_new)
        l = alpha * l + jnp.sum(pr, axis=1, keepdims=True)
        acc = alpha * acc + jnp.dot(pr, kc, preferred_element_type=F32)
        m = m_new
    o_ref[0] = acc / l


def _decode_attn(page_table, q_lat, q_rope, c_new, kr_new, cache_c, cache_kr_t, scale):
    ns, heads, kvl = q_lat.shape
    rope = q_rope.shape[-1]
    n_pages = page_table.shape[1]
    page = cache_c.shape[1]
    assert cache_kr_t.shape[1:] == (rope, page)
    chunk_pages = _tile(n_pages, 16, 1)
    n_chunks = n_pages // chunk_pages
    assert n_chunks % 2 == 0
    n_slots = 4 if n_chunks % 4 == 0 else 2
    keys = chunk_pages * page
    grid_spec = pltpu.PrefetchScalarGridSpec(
        num_scalar_prefetch=1,
        grid=(ns,),
        in_specs=[
            pl.BlockSpec((1, heads, kvl), lambda s, pt: (s, 0, 0)),
            pl.BlockSpec((1, heads, rope), lambda s, pt: (s, 0, 0)),
            pl.BlockSpec((1, 1, kvl), lambda s, pt: (s, 0, 0)),
            pl.BlockSpec((1, 1, rope), lambda s, pt: (s, 0, 0)),
            pl.BlockSpec(memory_space=pl.ANY),
            pl.BlockSpec(memory_space=pl.ANY),
        ],
        out_specs=pl.BlockSpec((1, heads, kvl), lambda s, pt: (s, 0, 0)),
        scratch_shapes=[
            pltpu.VMEM((n_slots, keys, kvl), cache_c.dtype),
            pltpu.VMEM((n_slots, rope, keys), cache_kr_t.dtype),
            pltpu.SemaphoreType.DMA((n_slots, 2)),
        ],
    )
    return pl.pallas_call(
        functools.partial(_decode_attn_kernel, chunk_pages=chunk_pages, n_chunks=n_chunks, n_slots=n_slots, page=page, scale=scale),
        grid_spec=grid_spec,
        out_shape=jax.ShapeDtypeStruct((ns, heads, kvl), F32),
        compiler_params=_cparams(("arbitrary",)),
        name="decode_attn",
    )(page_table, q_lat, q_rope, c_new.reshape(ns, 1, kvl), kr_new.reshape(ns, 1, rope), cache_c, cache_kr_t)


def _unabsorb_kernel(o_ref, w_ref, gate_ref, y_ref):
    y = jnp.dot(o_ref[0].astype(BF16), w_ref[0], preferred_element_type=F32)
    y_ref[...] = (y * jax.nn.sigmoid(gate_ref[...].astype(F32))).astype(y_ref.dtype)


def _unabsorb_o(o_lat_h, w_uv_h, zb, gate_off):
    heads, ns, kvl = o_lat_h.shape
    vh = w_uv_h.shape[-1]
    assert gate_off % vh == 0
    return pl.pallas_call(
        _unabsorb_kernel,
        grid=(heads,),
        in_specs=[
            pl.BlockSpec((1, ns, kvl), lambda h: (h, 0, 0)),
            pl.BlockSpec((1, kvl, vh), lambda h: (h, 0, 0)),
            pl.BlockSpec((ns, vh), lambda h: (0, gate_off // vh + h)),
        ],
        out_specs=pl.BlockSpec((ns, vh), lambda h: (0, h)),
        out_shape=jax.ShapeDtypeStruct((ns, heads * vh), F32),
        compiler_params=_cparams(("parallel",)),
        name="unabsorb_o",
    )(o_lat_h, w_uv_h, zb)


def _segment_mid(bc, s):
    c, dk = bc.shape
    if 2 * s >= 8:
        n = c // (2 * s)
        mid = bc.reshape(n, 2 * s, dk)[:, s - 1:s, :]
        return jnp.broadcast_to(mid, (n, 2 * s, dk)).reshape(c, dk)
    r = lax.broadcasted_iota(I32, bc.shape, 0) & (2 * s - 1)
    prev1 = pltpu.roll(bc, 1, 0)
    if s == 1:
        return jnp.where(r == 0, bc, prev1)
    assert s == 2
    return jnp.where(r == 0, pltpu.roll(bc, c - 1, 0), jnp.where(r == 1, bc, jnp.where(r == 2, prev1, pltpu.roll(bc, 2, 0))))


def _gla_prompt_kernel(q_ref, k_ref, v_ref, go_ref, gg_ref, lr_ref, wa_ref, ba_ref, gn_ref, y_ref, st_ref, st_scr, la_scr, *, n_chunks, c, scale):
    tb = pl.program_id(2)

    @pl.when(tb == 0)
    def _():
        st_scr[...] = jnp.zeros(st_scr.shape, F32)

    x = jnp.dot(lr_ref[...], wa_ref[...], preferred_element_type=F32, precision=HI) + ba_ref[...]
    la_scr[...] = _log_sigmoid(x) / GLA_TAU

    row = lax.broadcasted_iota(I32, (c, c), 0)
    col = lax.broadcasted_iota(I32, (c, c), 1)
    levels = []
    s = c // 2
    while s >= 1:
        levels.append(s)
        s //= 2
    tril = jnp.where(col <= row, 1.0, 0.0)
    masks = []
    for s in levels:
        sh = s.bit_length() - 1
        masks.append(((row >> (sh + 1)) == (col >> (sh + 1))) & (((row >> sh) & 1) == 1) & (((col >> sh) & 1) == 0))
    eye = row == col

    def chunk(ci, carry):
        r0 = pl.multiple_of(ci * c, c)
        q = q_ref[pl.ds(r0, c), :].astype(F32) * scale
        k = k_ref[pl.ds(r0, c), :].astype(F32)
        v = v_ref[pl.ds(r0, c), :]
        bc = jnp.dot(tril, la_scr[pl.ds(r0, c), :], preferred_element_type=F32, precision=HI)
        bl = bc[c - 1:c, :]
        att = jnp.where(eye, jnp.sum(q * k, axis=1, keepdims=True), 0.0)
        for li, s in enumerate(levels):
            ref = _segment_mid(bc, s)
            qh = (q * jnp.exp(jnp.minimum(bc - ref, 0.0))).astype(BF16)
            kh = (k * jnp.exp(jnp.minimum(ref - bc, 0.0))).astype(BF16)
            a = lax.dot_general(qh, kh, _NT, preferred_element_type=F32)
            att = att + jnp.where(masks[li], a, 0.0)
        st = st_scr[...]
        qt = (q * jnp.exp(bc)).astype(BF16)
        o = (lax.dot_general(qt, st.astype(BF16), _NT, preferred_element_type=F32)
             + jnp.dot(att.astype(BF16), v, preferred_element_type=F32))
        kb = (k * jnp.exp(bl - bc)).astype(BF16)
        st_scr[...] = st * jnp.exp(bl) + lax.dot_general(v, kb, _TN, preferred_element_type=F32)
        on = _rms(o, gn_ref[...])
        go = go_ref[pl.ds(r0, c), :].astype(F32)
        gg = gg_ref[pl.ds(r0, c), :].astype(F32)
        y_ref[pl.ds(r0, c), :] = (on * (go * jax.nn.sigmoid(go)) * jax.nn.sigmoid(gg)).astype(y_ref.dtype)
        return carry

    lax.fori_loop(0, n_chunks, chunk, 0, unroll=2 if n_chunks % 2 == 0 else 1)

    @pl.when(tb == pl.num_programs(2) - 1)
    def _():
        st_ref[0, 0] = st_scr[...].T


def _gla_prompt(zb, zs, offs, w_a2p, b_a2, g_gla, batch, seq, gh, dk, dv):
    n = batch * seq
    c = math.gcd(seq, GLA_CHUNK)
    tb = _tile(seq, 512, c)
    nb = seq // tb
    for key, w in (("gq", dk), ("gk", dk), ("gv", dv), ("g_out", dv), ("gate_gla", dv)):
        assert offs[key] % w == 0
    assert offs["g_lr"] % LANES == 0

    def rows(b, h, t):
        return b * nb + t

    in_specs = [
        pl.BlockSpec((tb, dk), lambda b, h, t: (rows(b, h, t), offs["gq"] // dk + h)),
        pl.BlockSpec((tb, dk), lambda b, h, t: (rows(b, h, t), offs["gk"] // dk + h)),
        pl.BlockSpec((tb, dv), lambda b, h, t: (rows(b, h, t), offs["gv"] // dv + h)),
        pl.BlockSpec((tb, dv), lambda b, h, t: (rows(b, h, t), offs["g_out"] // dv + h)),
        pl.BlockSpec((tb, dv), lambda b, h, t: (rows(b, h, t), offs["gate_gla"] // dv + h)),
        pl.BlockSpec((tb, LANES), lambda b, h, t: (rows(b, h, t), offs["g_lr"] // LANES)),
        pl.BlockSpec((LANES, dk), lambda b, h, t: (0, h)),
        pl.BlockSpec((1, dk), lambda b, h, t: (0, h)),
        pl.BlockSpec((1, dv), lambda b, h, t: (0, 0)),
    ]
    out_specs = [
        pl.BlockSpec((tb, dv), lambda b, h, t: (rows(b, h, t), h)),
        pl.BlockSpec((1, 1, dk, dv), lambda b, h, t: (b, h, 0, 0)),
    ]
    return pl.pallas_call(
        functools.partial(_gla_prompt_kernel, n_chunks=tb // c, c=c, scale=dk ** -0.5),
        grid=(batch, gh, nb),
        in_specs=in_specs,
        out_specs=out_specs,
        out_shape=[jax.ShapeDtypeStruct((n, gh * dv), BF16), jax.ShapeDtypeStruct((batch, gh, dk, dv), F32)],
        scratch_shapes=[pltpu.VMEM((dv, dk), F32), pltpu.VMEM((tb, dk), F32)],
        compiler_params=_cparams(("parallel", "parallel", "arbitrary")),
        name="gla_prompt",
    )(zb, zb, zb, zb, zb, zs, w_a2p, b_a2.reshape(1, gh * dk), g_gla.reshape(1, dv))


def _gla_decode_kernel(qk_ref, lrt_ref, wat_ref, bcol_ref, v_ref, go_ref, gg_ref, gn_ref, s0_ref, y_ref, s1_ref, *, sb, scale):
    x = jnp.dot(wat_ref[0], lrt_ref[0], preferred_element_type=F32, precision=HI) + bcol_ref[0]
    a = jnp.exp(_log_sigmoid(x) / GLA_TAU)
    outs = []
    for u in range(sb):
        qc = qk_ref[0, 0, :, u:u + 1] * scale
        kc = qk_ref[0, 0, :, sb + u:sb + u + 1]
        sn = a[:, u:u + 1] * s0_ref[u, 0] + kc * v_ref[u:u + 1, :]
        s1_ref[u, 0] = sn
        outs.append(jnp.sum(qc * sn, axis=0, keepdims=True))
    o = jnp.concatenate(outs, axis=0)
    go = go_ref[...]
    y_ref[...] = _rms(o, gn_ref[...]) * (go * jax.nn.sigmoid(go)) * jax.nn.sigmoid(gg_ref[...])


def _gla_decode(qk_cols, lr_t, wa_t, b_col, gv, g_out, gate_gla, g_gla, state):
    ns, gh, dk, dv = state.shape
    sb = qk_cols.shape[-1] // 2
    return pl.pallas_call(
        functools.partial(_gla_decode_kernel, sb=sb, scale=dk ** -0.5),
        grid=(ns // sb, gh),
        in_specs=[
            pl.BlockSpec((1, 1, dk, 2 * sb), lambda i, h: (i, h, 0, 0)),
            pl.BlockSpec((1, LANES, sb), lambda i, h: (i, 0, 0)),
            pl.BlockSpec((1, dk, LANES), lambda i, h: (h, 0, 0)),
            pl.BlockSpec((1, dk, 1), lambda i, h: (h, 0, 0)),
            pl.BlockSpec((sb, dv), lambda i, h: (i, h)),
            pl.BlockSpec((sb, dv), lambda i, h: (i, h)),
            pl.BlockSpec((sb, dv), lambda i, h: (i, h)),
            pl.BlockSpec((1, dv), lambda i, h: (0, 0)),
            pl.BlockSpec((sb, 1, dk, dv), lambda i, h: (i, h, 0, 0)),
        ],
        out_specs=[pl.BlockSpec((sb, dv), lambda i, h: (i, h)), pl.BlockSpec((sb, 1, dk, dv), lambda i, h: (i, h, 0, 0))],
        out_shape=[jax.ShapeDtypeStruct((ns, gh * dv), F32), jax.ShapeDtypeStruct((ns, gh, dk, dv), F32)],
        compiler_params=_cparams(("parallel", "parallel")),
        name="gla_decode",
    )(qk_cols, lr_t, wa_t, b_col, gv, g_out, gate_gla, g_gla.reshape(1, dv), state)


def _outproj_kernel(ym_ref, yg_ref, h_ref, w_ref, g_ref, wr_ref, br_ref, h1_ref, m_ref, route_ref):
    mix = (ym_ref[...].astype(F32) + yg_ref[...].astype(F32)).astype(BF16)
    h1 = h_ref[...] + jnp.dot(mix, w_ref[...], preferred_element_type=F32)
    h1_ref[...] = h1
    m = _rms(h1, g_ref[...])
    half = m.shape[1] // 2
    m_ref[...] = _pack_bf16_pair(m[:, :half], m[:, half:])
    logits = jnp.dot(m.astype(BF16), wr_ref[...].astype(BF16), preferred_element_type=F32) + br_ref[...]
    lane = lax.broadcasted_iota(I32, logits.shape, 1).astype(F32)
    vals, idxs = [], []
    for _ in range(TOP_K):
        mx = jnp.max(logits, axis=1, keepdims=True)
        ix = jnp.min(jnp.where(logits == mx, lane, float(LANES)), axis=1, keepdims=True)
        vals.append(mx)
        idxs.append(ix)
        logits = jnp.where(lane == ix, -jnp.inf, logits)
    ex = [jnp.exp(v - vals[0]) for v in vals]
    den = ex[0]
    for e in ex[1:]:
        den = den + e
    route = jnp.zeros(logits.shape, F32)
    for kk in range(TOP_K):
        route = jnp.where(lane == float(kk), idxs[kk], route)
        route = jnp.where(lane == float(TOP_K + kk), ex[kk] / den, route)
    route_ref[...] = route


def _outproj(y_mla, y_gla, h, w_out_b, g_ffn, w_router_p, b_router_p):
    n, d = h.shape
    tm = _tile(n, 256, 16)
    row = lambda i: (i, 0)
    fixed = lambda i: (0, 0)
    return pl.pallas_call(
        _outproj_kernel,
        grid=(n // tm,),
        in_specs=[
            pl.BlockSpec((tm, d), row), pl.BlockSpec((tm, d), row), pl.BlockSpec((tm, d), row),
            pl.BlockSpec((d, d), fixed), pl.BlockSpec((1, d), fixed),
            pl.BlockSpec((d, LANES), fixed), pl.BlockSpec((1, LANES), fixed),
        ],
        out_specs=[pl.BlockSpec((tm, d), row), pl.BlockSpec((tm, d // 2), row), pl.BlockSpec((tm, LANES), row)],
        out_shape=[jax.ShapeDtypeStruct((n, d), F32), jax.ShapeDtypeStruct((n, d // 2), U32), jax.ShapeDtypeStruct((n, LANES), F32)],
        compiler_params=_cparams(("parallel",)),
        name="outproj",
    )(y_mla, y_gla, h, w_out_b, g_ffn.reshape(1, d), w_router_p, b_router_p)


def _dispatch_kernel(idx_ref, x_ref, o_ref, g_scr, *, rows):
    base = pl.program_id(0) * rows

    def body(r, carry):
        tok = idx_ref[base + r]
        g_scr[pl.ds(r, 1), :] = x_ref[pl.ds(tok, 1), :]
        return carry

    lax.fori_loop(0, rows, body, 0, unroll=8)
    hi, lo = _unpack_bf16_pair(g_scr[...])
    o_ref[...] = jnp.concatenate([hi, lo], axis=1).astype(BF16)


def _dispatch(buf_tok, m_packed):
    rows_total = buf_tok.shape[0]
    n, w = m_packed.shape
    rows = _tile(rows_total, 512, 16)
    grid_spec = pltpu.PrefetchScalarGridSpec(
        num_scalar_prefetch=1,
        grid=(rows_total // rows,),
        in_specs=[pl.BlockSpec(memory_space=pltpu.VMEM)],
        out_specs=pl.BlockSpec((rows, 2 * w), lambda i, idx: (i, 0)),
        scratch_shapes=[pltpu.VMEM((rows, w), U32)],
    )
    return pl.pallas_call(
        functools.partial(_dispatch_kernel, rows=rows),
        grid_spec=grid_spec,
        out_shape=jax.ShapeDtypeStruct((rows_total, 2 * w), BF16),
        compiler_params=_cparams(("arbitrary",), 56),
        name="moe_dispatch",
    )(buf_tok, m_packed)


def _for_row_chunks(b0, b1, rb, max_blocks, fn):
    n = b1 - b0
    size = 1 << (max_blocks.bit_length() - 1)
    while size >= 1:
        start = b0 + (n & ~(2 * size - 1))

        @pl.when((n & size) != 0)
        def _(start=start, size=size):
            fn(pl.multiple_of(start * rb, rb), size * rb)

        size //= 2


def _moe_kernel(ve_ref, blk_ref, nb_ref, zf_ref, xs_ref, wg_ref, wu_ref, bg_ref, bu_ref, wd_ref, bd_ref, ys_hbm,
                y_scr, o_scr, sem, *, n_ff, n_slab, dn, rb):
    v = pl.program_id(0)
    j = pl.program_id(1)
    blk0 = blk_ref[v]
    nb = nb_ref[v]
    b0, b1 = 0, nb
    fill = zf_ref[v] == 1
    work = jnp.logical_not(fill)
    max_blocks = xs_ref.shape[0] // rb
    d = y_scr.shape[1]
    slab = d // n_slab

    def out_copy(b):
        return pltpu.make_async_copy(o_scr.at[pl.ds(pl.multiple_of(b * rb, rb), rb)], ys_hbm.at[blk0 + b], sem.at[0])

    def flush():
        lax.fori_loop(0, nb, lambda b, c: (out_copy(b).start(), c)[1], 0)
        lax.fori_loop(0, nb, lambda b, c: (out_copy(b).wait(), c)[1], 0)

    @pl.when(fill & (j == 0))
    def _():
        def zero(b, carry):
            o_scr[pl.ds(pl.multiple_of(b * rb, rb), rb), :] = jnp.zeros((rb, o_scr.shape[1]), U32)
            return carry
        lax.fori_loop(0, nb, zero, 0)
        flush()

    @pl.when(work & (j == 0))
    def _():
        def init(r0, rows):
            y_scr[pl.ds(r0, rows), :] = jnp.broadcast_to(bd_ref[0], (rows, d))
        _for_row_chunks(b0, b1, rb, max_blocks, init)

    @pl.when(work)
    def _():
        def mlp(r0, rows):
            x = xs_ref[pl.ds(r0, rows), :]
            g = jnp.dot(x, wg_ref[0].astype(BF16), preferred_element_type=F32) + bg_ref[0]
            u = jnp.dot(x, wu_ref[0].astype(BF16), preferred_element_type=F32) + bu_ref[0]
            gate = jnp.minimum(g, SWIGLU_LIMIT)
            up = jnp.clip(u, -SWIGLU_LIMIT, SWIGLU_LIMIT)
            glu = gate * jax.nn.sigmoid(gate * SWIGLU_ALPHA)
            a = ((up + 1.0) * glu).astype(BF16)
            for c0 in range(0, d, dn):
                w = wd_ref[0, :, c0:c0 + dn].astype(BF16)
                y_scr[pl.ds(r0, rows), c0:c0 + dn] += jnp.dot(a, w, preferred_element_type=F32)
        _for_row_chunks(b0, b1, rb, max_blocks, mlp)

    @pl.when(work & (j == n_ff - 1))
    def _():
        def emit(r0, rows):
            for s in range(n_slab):
                y = y_scr[pl.ds(r0, rows), s * slab:(s + 1) * slab]
                o_scr[pl.ds(r0, rows), s * (slab // 2):(s + 1) * (slab // 2)] = _pack_bf16_pair(y[:, :slab // 2], y[:, slab // 2:])
        _for_row_chunks(b0, b1, rb, max_blocks, emit)
        flush()


def _moe_experts(meta, xs, w_gu, b_gu, w_down, b_down, rows_total, window_rows):
    v_exp, v_blk, v_nb, v_zero = meta
    d = xs.shape[1]
    e, _, f2 = w_gu.shape
    rb = EXPERT_ROW_BLOCK
    ff = f2 // 2
    tf = _tile(ff, 256, LANES)
    n_ff = ff // tf
    n_slab = d // _tile(d, MOE_SLAB, 2 * LANES)
    n_visits = v_exp.shape[0]
    ff_idx = lambda j, zf: jnp.where(zf == 1, n_ff - 1, j)
    grid_spec = pltpu.PrefetchScalarGridSpec(
        num_scalar_prefetch=4,
        grid=(n_visits, n_ff),
        in_specs=[
            pl.BlockSpec((pl.Element(window_rows), pl.Element(d)), lambda v, j, ve, blk, nb, zf: (blk[v] * rb, 0)),
            pl.BlockSpec((1, d, tf), lambda v, j, ve, blk, nb, zf: (ve[v], 0, ff_idx(j, zf[v]))),
            pl.BlockSpec((1, d, tf), lambda v, j, ve, blk, nb, zf: (ve[v], 0, n_ff + ff_idx(j, zf[v]))),
            pl.BlockSpec((1, 1, tf), lambda v, j, ve, blk, nb, zf: (ve[v], 0, ff_idx(j, zf[v]))),
            pl.BlockSpec((1, 1, tf), lambda v, j, ve, blk, nb, zf: (ve[v], 0, n_ff + ff_idx(j, zf[v]))),
            pl.BlockSpec((1, tf, d), lambda v, j, ve, blk, nb, zf: (ve[v], ff_idx(j, zf[v]), 0)),
            pl.BlockSpec((1, 1, d), lambda v, j, ve, blk, nb, zf: (ve[v], 0, 0)),
        ],
        out_specs=pl.BlockSpec(memory_space=pl.ANY),
        scratch_shapes=[pltpu.VMEM((window_rows, d), F32), pltpu.VMEM((window_rows, d // 2), U32), pltpu.SemaphoreType.DMA((1,))],
    )
    return pl.pallas_call(
        functools.partial(_moe_kernel, n_ff=n_ff, n_slab=n_slab, dn=_tile(d, 512, LANES), rb=rb),
        grid_spec=grid_spec,
        out_shape=jax.ShapeDtypeStruct((rows_total // rb, rb, d // 2), U32),
        compiler_params=_cparams(("arbitrary", "arbitrary"), 56),
        name="moe_experts",
    )(v_exp, v_blk, v_nb, v_zero, xs, w_gu, w_gu, b_gu.reshape(e, 1, f2), b_gu.reshape(e, 1, f2), w_down, b_down.reshape(e, 1, d))


def _combine_kernel(dest_ref, ys_ref, route_ref, o_ref, stage, *, tm):
    base = pl.program_id(1) * tm

    def body(t, carry):
        for kk in range(TOP_K):
            r = dest_ref[(base + t) * TOP_K + kk]
            stage[kk, pl.ds(t, 1), :] = ys_ref[pl.ds(r, 1), :]
        return carry

    lax.fori_loop(0, tm, body, 0, unroll=4)
    acc_hi = acc_lo = None
    for kk in range(TOP_K):
        g = route_ref[:, TOP_K + kk:TOP_K + kk + 1]
        hi, lo = _unpack_bf16_pair(stage[kk])
        acc_hi = g * hi if acc_hi is None else acc_hi + g * hi
        acc_lo = g * lo if acc_lo is None else acc_lo + g * lo
    o_ref[...] = jnp.concatenate([acc_hi, acc_lo], axis=1)


def _combine(dest, ys, route):
    rows_total, packed = ys.shape
    wh = MOE_SLAB // 2
    n_down = packed // wh
    n = route.shape[0]
    tm = _tile(n, 256, 8)
    grid_spec = pltpu.PrefetchScalarGridSpec(
        num_scalar_prefetch=1,
        grid=(n_down, n // tm),
        in_specs=[
            pl.BlockSpec((rows_total, wh), lambda c, i, dest: (0, c)),
            pl.BlockSpec((tm, LANES), lambda c, i, dest: (i, 0)),
        ],
        out_specs=pl.BlockSpec((tm, 2 * wh), lambda c, i, dest: (i, c)),
        scratch_shapes=[pltpu.VMEM((TOP_K, tm, wh), U32)],
    )
    return pl.pallas_call(
        functools.partial(_combine_kernel, tm=tm),
        grid_spec=grid_spec,
        out_shape=jax.ShapeDtypeStruct((n, n_down * 2 * wh), F32),
        compiler_params=_cparams(("arbitrary", "arbitrary"), 56),
        name="moe_combine",
    )(dest, ys, route)


def _route_meta(top_i, n_experts, window_rows):
    n_tok = top_i.shape[0]
    n_asg = n_tok * TOP_K
    rb = EXPERT_ROW_BLOCK
    e_flat = top_i.reshape(n_asg)
    tok_flat = (jnp.arange(n_asg, dtype=I32) // TOP_K).astype(I32)
    onehot = jax.nn.one_hot(e_flat, n_experts, dtype=I32)
    rank = jnp.take_along_axis(jnp.cumsum(onehot, axis=0), e_flat[:, None], axis=1)[:, 0] - 1
    counts = onehot.sum(axis=0)
    padded = (counts + rb - 1) // rb * rb
    ends = jnp.cumsum(padded)
    starts = ends - padded
    dest = (starts[e_flat] + rank).astype(I32)
    rows_total = (-(-n_asg // rb) + n_experts) * rb
    rows_alloc = -(-(rows_total + window_rows) // 512) * 512
    buf_tok = jnp.zeros((rows_alloc,), I32).at[dest].set(tok_flat, unique_indices=True, mode="promise_in_bounds")
    mb = window_rows // rb
    n_blk = (padded // rb).astype(I32)
    vpe = -(-n_blk // mb)
    cv = jnp.cumsum(vpe).astype(I32)
    n_work = cv[-1]
    total_blk = (ends[-1] // rb).astype(I32)
    tail_blk = rows_total // rb - total_blk
    n_fill = -(-tail_blk // mb)
    n_visits = n_experts + rows_total // rb // mb + 1 + -(-n_experts // mb)
    pos = jnp.arange(n_visits, dtype=I32)
    is_work = pos < n_work
    e = jnp.minimum(jnp.searchsorted(cv, pos, side="right"), n_experts - 1).astype(I32)
    k = pos - (cv[e] - vpe[e])
    f = pos - n_work
    is_fill = (f >= 0) & (f < n_fill)
    blk = jnp.where(is_work, (starts[e] // rb).astype(I32) + k * mb, total_blk + jnp.clip(f, 0, n_fill) * mb)
    blk = jnp.minimum(blk, rows_total // rb)
    nb = jnp.where(is_work, jnp.clip(n_blk[e] - k * mb, 0, mb), jnp.where(is_fill, jnp.clip(tail_blk - f * mb, 0, mb), 0))
    v_exp = jnp.where(is_work, e, e[jnp.maximum(n_work - 1, 0)])
    v_zero = jnp.logical_not(is_work).astype(I32)
    return buf_tok, dest, (v_exp.astype(I32), blk.astype(I32), nb.astype(I32), v_zero), rows_total


def _ple_kernel(h1_ref, moe_ref, p_ref, wp_ref, wg_ref, gp_ref, gf_ref, y_ref, *, final):
    h2 = h1_ref[...] + moe_ref[...]
    u = _rms(h2, gp_ref[...]).astype(BF16)
    gate = jax.nn.sigmoid(jnp.dot(u, wg_ref[...], preferred_element_type=F32))
    pe = jnp.dot(p_ref[...].astype(BF16), wp_ref[...], preferred_element_type=F32)
    y = h2 + pe * gate
    y_ref[...] = _rms(y, gf_ref[...]) if final else y


def _ple_final(h1, moe, row_off, p, w_ple_b, w_ple_gate_b, g_ple, g_final, final):
    n, d = h1.shape
    pd = p.shape[1]
    tm = _tile(n, 256, 8)
    assert row_off % tm == 0
    row = lambda i: (i, 0)
    fixed = lambda i: (0, 0)
    return pl.pallas_call(
        functools.partial(_ple_kernel, final=final),
        grid=(n // tm,),
        in_specs=[
            pl.BlockSpec((tm, d), row), pl.BlockSpec((tm, d), lambda i: (row_off // tm + i, 0)), pl.BlockSpec((tm, pd), row),
            pl.BlockSpec((pd, d), fixed), pl.BlockSpec((d, d), fixed), pl.BlockSpec((1, d), fixed), pl.BlockSpec((1, d), fixed),
        ],
        out_specs=pl.BlockSpec((tm, d), row),
        out_shape=jax.ShapeDtypeStruct((n, d), F32),
        compiler_params=_cparams(("parallel",)),
        name="ple_final",
    )(h1, moe, p, w_ple_b, w_ple_gate_b, g_ple.reshape(1, d), g_final.reshape(1, d))


def _rope_table(pos, rope):
    half = rope // 2
    inv_freq = jnp.power(ROPE_THETA, -jnp.arange(half, dtype=F32) / half)
    ang = pos.astype(F32)[:, None] * inv_freq[None, :]
    cos, sin = jnp.cos(ang), jnp.sin(ang)
    return jnp.concatenate([cos, cos, -sin, sin], axis=1)


def _swap_halves(w):
    half = w.shape[-1] // 2
    return jnp.concatenate([w[..., half:], w[..., :half]], axis=-1)


def kernel(x_prompt, x_sample, cache_kv_latent, cache_k_rope, state_gla, page_table, p_prompt, p_sample, g_mix, w_in, g_q_a, w_uq, g_kv_a, w_uk, w_uv, w_a2, b_a2, g_gla, w_out, g_ffn, w_router, b_router, w_gu, b_gu, w_down, b_down, g_ple, w_ple, w_ple_gate, g_final):
    depth = w_in.shape[0]
    batch, seq, d = x_prompt.shape
    ns, dec_seq, _ = x_sample.shape
    assert dec_seq == 1
    ql, kvl = g_q_a.shape[-1], g_kv_a.shape[-1]
    rope = cache_k_rope.shape[-1]
    heads, nope = w_uk.shape[2], w_uk.shape[3]
    vh = w_uv.shape[3]
    _, _, gh, dk, dv = state_gla.shape
    lr = w_a2.shape[1]
    n_experts = w_router.shape[-1]
    past = page_table.shape[1] * cache_kv_latent.shape[2]
    assert 2 * rope == LANES and nope == LANES and heads * vh == d and lr <= LANES and n_experts <= LANES
    scale = LOG2E / math.sqrt(nope + rope)
    n_p = batch * seq
    big_segs = (("gv", gh * dv), ("g_out", gh * dv), ("gate_mla", d), ("gate_gla", d), ("gq", gh * dk), ("gk", gh * dk))
    offs, o = {}, 0
    for key, w in big_segs:
        offs[key] = o
        o += w
    offs["g_lr"] = ql + kvl + 2 * rope
    in_splits = (ql, kvl, rope, gh * dk, gh * dk, gh * dv, gh * dv, lr, d, d)
    in_offsets = tuple(sum(in_splits[:i + 1]) for i in range(len(in_splits) - 1))

    tab_p = jnp.tile(_rope_table(jnp.arange(seq), rope), (batch, 1))
    tab_s = _rope_table(jnp.full((ns,), past, I32), rope)

    h_p = x_prompt.reshape(n_p, d)
    h_s = x_sample.reshape(ns, d)
    kv_p, kr_p, st_p, kv_s, kr_s, st_s = [], [], [], [], [], []
    for l in range(depth):
        seg = dict(zip(("c_q", "c_kv", "k_r", "gq", "gk", "gv", "g_out", "g_lr", "gate_mla", "gate_gla"), jnp.split(w_in[l], in_offsets, axis=1)))
        w_big = jnp.concatenate([seg[k] for k, _ in big_segs], axis=1).astype(BF16)
        w_small = jnp.concatenate([seg["c_q"], seg["c_kv"], seg["k_r"], _swap_halves(seg["k_r"]),
                                   jnp.pad(seg["g_lr"], ((0, 0), (0, LANES - lr)))], axis=1).astype(BF16)
        wq = w_uq[l].reshape(ql, heads, nope + rope)
        w_uq_p = jnp.concatenate([wq[..., :nope], wq[..., nope:], _swap_halves(wq[..., nope:])], axis=-1).transpose(1, 0, 2).astype(BF16)
        w_uk_h = w_uk[l].transpose(1, 0, 2).astype(BF16)
        w_uv_h = w_uv[l].transpose(1, 0, 2).astype(BF16)
        w_uk_f = w_uk[l].reshape(kvl, heads * nope).astype(BF16)
        w_uv_tf = w_uv[l].reshape(kvl, heads * vh).T.astype(BF16)
        w_a2p = jnp.pad(w_a2[l], ((0, LANES - lr), (0, 0)))
        w_out_b = w_out[l].astype(BF16)
        w_router_p = jnp.pad(w_router[l], ((0, 0), (0, LANES - n_experts)))
        b_router_p = jnp.pad(b_router[l], (0, LANES - n_experts), constant_values=NEG_BIG).reshape(1, LANES)
        w_ple_b = w_ple[l].astype(BF16)
        w_ple_gate_b = w_ple_gate[l].astype(BF16)

        def front(h, tab, z_dtype, q_scale):
            a = _rms_cast(h, g_mix[l])
            zb = _matmul(a, w_big, z_dtype, "inproj_big")
            zs = _matmul(a, w_small, F32, "inproj_small")
            q = _q_proj(zs, g_q_a[l], w_uq_p, tab, q_scale)
            return zb, zs, q

        zb, zs, q = front(h_p, tab_p, BF16, scale)
        c_kv, k_r, k_heads, v_t = _kv_prep(zs, g_kv_a[l], tab_p, w_uk_f, w_uv_tf, heads, ql, rope, True)
        y_mla = _flash_prompt(q, k_heads, v_t, zb, offs["gate_mla"], batch, seq)
        y_gla, s_new = _gla_prompt(zb, zs, offs, w_a2p, b_a2[l], g_gla[l], batch, seq, gh, dk, dv)
        h1_p, m_p, route_p = _outproj(y_mla, y_gla, h_p, w_out_b, g_ffn[l], w_router_p, b_router_p)
        kv_p.append(c_kv.reshape(batch, seq, kvl)); kr_p.append(k_r.reshape(batch, seq, rope)); st_p.append(s_new)

        zb, zs, q = front(h_s, tab_s, F32, 1.0)
        c_kv, k_r = _kv_prep(zs, g_kv_a[l], tab_s, w_uk_f, w_uv_tf, heads, ql, rope, False)
        q_lat = _absorb_q(q, w_uk_h).transpose(1, 0, 2)
        q_rope = q[:, :, nope:nope + rope].astype(F32).transpose(1, 0, 2)
        o_lat = _decode_attn(page_table, q_lat, q_rope, c_kv, k_r, cache_kv_latent[l], cache_k_rope[l].transpose(0, 2, 1), scale)
        y_mla = _unabsorb_o(o_lat.transpose(1, 0, 2), w_uv_h, zb, offs["gate_mla"])
        sb = 8
        zf = zb.astype(F32)
        cols = lambda key, w: zf[:, offs[key]:offs[key] + w]
        to_cols = lambda z: z.reshape(ns // sb, sb, gh, dk).transpose(0, 2, 3, 1)
        qk_cols = jnp.concatenate([to_cols(cols("gq", gh * dk)), to_cols(cols("gk", gh * dk))], axis=-1)
        lr_t = zs[:, offs["g_lr"]:offs["g_lr"] + LANES].reshape(ns // sb, sb, LANES).transpose(0, 2, 1)
        wa_t = w_a2p.reshape(LANES, gh, dk).transpose(1, 2, 0)
        b_col = b_a2[l].reshape(gh, dk, 1)
        y_gla, s_new = _gla_decode(qk_cols, lr_t, wa_t, b_col, cols("gv", gh * dv), cols("g_out", gh * dv),
                                   cols("gate_gla", gh * dv), g_gla[l], state_gla[l])
        h1_s, m_s, route_s = _outproj(y_mla, y_gla, h_s, w_out_b, g_ffn[l], w_router_p, b_router_p)
        kv_s.append(c_kv.reshape(ns, 1, kvl)); kr_s.append(k_r.reshape(ns, 1, rope)); st_s.append(s_new)

        m_all = jnp.concatenate([m_p, m_s], axis=0)
        route = jnp.concatenate([route_p, route_s], axis=0)
        top_i = route[:, :TOP_K].astype(I32)
        buf_tok, dest, meta, rows_total = _route_meta(top_i, n_experts, MOE_WINDOW_ROWS)
        xs = _dispatch(buf_tok, m_all)
        ys = _moe_experts(meta, xs, w_gu[l], b_gu[l], w_down[l], b_down[l], rows_total, MOE_WINDOW_ROWS)
        ys = ys.reshape(rows_total, d // 2)
        moe = _combine(dest, ys, route)

        last = l == depth - 1
        h_p = _ple_final(h1_p, moe, 0, p_prompt[l].reshape(n_p, -1), w_ple_b, w_ple_gate_b, g_ple[l], g_final, last)
        h_s = _ple_final(h1_s, moe, n_p, p_sample[l].reshape(ns, -1), w_ple_b, w_ple_gate_b, g_ple[l], g_final, last)

    return (h_p.reshape(batch, seq, d), h_s.reshape(ns, 1, d), jnp.stack(kv_p), jnp.stack(kr_p), jnp.stack(st_p),
            jnp.stack(kv_s), jnp.stack(kr_s), jnp.stack(st_s))
```

```python
import functools
import math

import jax
import jax.numpy as jnp
from jax import lax
from jax.experimental import pallas as pl
from jax.experimental.pallas import tpu as pltpu

F32 = jnp.float32
BF16 = jnp.bfloat16
U32 = jnp.uint32
I32 = jnp.int32

EPS = 1e-6
ROPE_THETA = 10000.0
GLA_TAU = 16.0
GLA_CHUNK = 64
GLA_UNROLL = 4
TOP_K = 4
SWIGLU_LIMIT = 7.0
SWIGLU_ALPHA = 1.702
EXPERT_ROW_BLOCK = 128
MOE_SLAB = 256
MOE_WINDOW_ROWS = 1536
LANES = 128
MIB = 1024 * 1024
SUM_ROWS = 16
LOG2E = 1.4426950408889634
HI = lax.Precision.HIGHEST
NEG_BIG = -1e30

_NT = (((1,), (1,)), ((), ()))
_TN = (((0,), (0,)), ((), ()))


def _cparams(sem, vmem_mib=48):
    return pltpu.CompilerParams(dimension_semantics=sem, vmem_limit_bytes=vmem_mib * MIB)


def _tile(n, cap, mult):
    best = None
    for t in range(mult, min(n, cap) + 1, mult):
        if n % t == 0:
            best = t
    assert best is not None, (n, cap, mult)
    return best


def _rms(x, g):
    return x * lax.rsqrt(jnp.mean(x * x, axis=-1, keepdims=True) + EPS) * g


def _log_sigmoid(x):
    return jnp.minimum(x, 0.0) - jnp.log1p(jnp.exp(-jnp.abs(x)))


def _pack_bf16_pair(hi, lo):
    hb = pltpu.bitcast(hi.astype(BF16).astype(F32), U32)
    lb = pltpu.bitcast(lo.astype(BF16).astype(F32), U32)
    return hb | (lb >> 16)


def _unpack_bf16_pair(w):
    hi = pltpu.bitcast(w & jnp.uint32(0xFFFF0000), F32)
    lo = pltpu.bitcast(w << 16, F32)
    return hi, lo


def _rms_cast_kernel(x_ref, g_ref, o_ref):
    o_ref[...] = _rms(x_ref[...], g_ref[...]).astype(o_ref.dtype)


def _rms_cast(x, g):
    n, d = x.shape
    tm = _tile(n, 512, 16)
    return pl.pallas_call(
        _rms_cast_kernel,
        grid=(n // tm,),
        in_specs=[pl.BlockSpec((tm, d), lambda i: (i, 0)), pl.BlockSpec((1, d), lambda i: (0, 0))],
        out_specs=pl.BlockSpec((tm, d), lambda i: (i, 0)),
        out_shape=jax.ShapeDtypeStruct((n, d), BF16),
        compiler_params=_cparams(("parallel",)),
        name="rms_cast",
    )(x, g.reshape(1, d))


def _mm_kernel(x_ref, w_ref, o_ref):
    o_ref[...] = jnp.dot(x_ref[...], w_ref[...], preferred_element_type=F32).astype(o_ref.dtype)


def _matmul(x, w, out_dtype, name):
    m, k = x.shape
    n = w.shape[1]
    tm = _tile(m, 1024, 16)
    tn = _tile(n, 1280, LANES)
    return pl.pallas_call(
        _mm_kernel,
        grid=(n // tn, m // tm),
        in_specs=[pl.BlockSpec((tm, k), lambda j, i: (i, 0)), pl.BlockSpec((k, tn), lambda j, i: (0, j))],
        out_specs=pl.BlockSpec((tm, tn), lambda j, i: (i, j)),
        out_shape=jax.ShapeDtypeStruct((m, n), out_dtype),
        compiler_params=_cparams(("parallel", "parallel")),
        name=name,
    )(x, w)


def _rope_combine(y2, tab):
    half = y2.shape[1] // 2
    t = y2 * tab
    r = t + pltpu.roll(t, half, axis=1)
    lane = lax.broadcasted_iota(I32, r.shape, 1)
    return jnp.where(lane < half, r, 0.0)


def _q_proj_kernel(cq_ref, g_ref, w_ref, tab_ref, q_ref, *, heads, nope, scale):
    cqn = _rms(cq_ref[...], g_ref[...]).astype(BF16)
    tab = tab_ref[...]
    for h in range(heads):
        y = jnp.dot(cqn, w_ref[h], preferred_element_type=F32)
        q_ref[h, :, :nope] = (y[:, :nope] * scale).astype(BF16)
        q_ref[h, :, nope:] = (_rope_combine(y[:, nope:], tab) * scale).astype(BF16)


def _q_proj(zs, g_q_a, w_uq_p, tab, scale):
    n = zs.shape[0]
    heads, ql, width = w_uq_p.shape
    tm = _tile(n, 512, 16)
    return pl.pallas_call(
        functools.partial(_q_proj_kernel, heads=heads, nope=width // 2, scale=scale),
        grid=(n // tm,),
        in_specs=[
            pl.BlockSpec((tm, ql), lambda i: (i, 0)),
            pl.BlockSpec((1, ql), lambda i: (0, 0)),
            pl.BlockSpec((heads, ql, width), lambda i: (0, 0, 0)),
            pl.BlockSpec((tm, LANES), lambda i: (i, 0)),
        ],
        out_specs=pl.BlockSpec((heads, tm, width), lambda i: (0, i, 0)),
        out_shape=jax.ShapeDtypeStruct((heads, n, width), BF16),
        compiler_params=_cparams(("parallel",)),
        name="q_proj",
    )(zs, g_q_a.reshape(1, ql), w_uq_p, tab)


def _kv_prep_kernel(ckv_ref, krs_ref, g_ref, tab_ref, wuk_ref, wuvt_ref, c_ref, kr_ref, k_ref, vt_ref, *, heads, nope, rope, vh):
    c = _rms(ckv_ref[...], g_ref[...])
    c_ref[...] = c
    cb = c.astype(BF16)
    r = _rope_combine(krs_ref[...], tab_ref[...])
    kr_ref[...] = r[:, :rope]
    rb = r.astype(BF16)
    kn = jnp.dot(cb, wuk_ref[...], preferred_element_type=F32).astype(BF16)
    vt = lax.dot_general(wuvt_ref[...], cb, _NT, preferred_element_type=F32).astype(BF16)
    for h in range(heads):
        k_ref[h, :, :nope] = kn[:, h * nope:(h + 1) * nope]
        k_ref[h, :, nope:] = rb
        vt_ref[h, :vh, :] = vt[h * vh:(h + 1) * vh, :]
        vt_ref[h, vh:, :] = jnp.ones((SUM_ROWS, cb.shape[0]), BF16)


def _latent_kernel(ckv_ref, krs_ref, g_ref, tab_ref, c_ref, kr_ref, *, rope):
    c_ref[...] = _rms(ckv_ref[...], g_ref[...])
    kr_ref[...] = _rope_combine(krs_ref[...], tab_ref[...])[:, :rope]


def _kv_prep(zs, g_kv_a, tab, w_uk_f, w_uv_tf, heads, ql, rope, with_kv):
    n = zs.shape[0]
    kvl = w_uk_f.shape[0]
    nope = w_uk_f.shape[1] // heads
    vh = w_uv_tf.shape[0] // heads
    assert ql % kvl == 0 and (ql + kvl) % LANES == 0
    tm = _tile(n, 640, LANES if with_kv else 8)
    in_specs = [
        pl.BlockSpec((tm, kvl), lambda i: (i, ql // kvl)),
        pl.BlockSpec((tm, LANES), lambda i: (i, (ql + kvl) // LANES)),
        pl.BlockSpec((1, kvl), lambda i: (0, 0)),
        pl.BlockSpec((tm, LANES), lambda i: (i, 0)),
    ]
    out_specs = [pl.BlockSpec((tm, kvl), lambda i: (i, 0)), pl.BlockSpec((tm, rope), lambda i: (i, 0))]
    out_shape = [jax.ShapeDtypeStruct((n, kvl), F32), jax.ShapeDtypeStruct((n, rope), F32)]
    args = [zs, zs, g_kv_a.reshape(1, kvl), tab]
    if not with_kv:
        return pl.pallas_call(
            functools.partial(_latent_kernel, rope=rope),
            grid=(n // tm,), in_specs=in_specs, out_specs=out_specs, out_shape=out_shape,
            compiler_params=_cparams(("parallel",)), name="latent",
        )(*args)
    in_specs += [pl.BlockSpec((kvl, heads * nope), lambda i: (0, 0)), pl.BlockSpec((heads * vh, kvl), lambda i: (0, 0))]
    out_specs += [pl.BlockSpec((heads, tm, 2 * nope), lambda i: (0, i, 0)), pl.BlockSpec((heads, vh + SUM_ROWS, tm), lambda i: (0, 0, i))]
    out_shape += [jax.ShapeDtypeStruct((heads, n, 2 * nope), BF16), jax.ShapeDtypeStruct((heads, vh + SUM_ROWS, n), BF16)]
    return pl.pallas_call(
        functools.partial(_kv_prep_kernel, heads=heads, nope=nope, rope=rope, vh=vh),
        grid=(n // tm,), in_specs=in_specs, out_specs=out_specs, out_shape=out_shape,
        compiler_params=_cparams(("parallel",)), name="kv_prep",
    )(*args, w_uk_f, w_uv_tf)


def _flash_kernel(qi_ref, ki_ref, q_ref, k_ref, vt_ref, gate_ref, o_ref, m_scr, acc_scr, s_scr, *, heads, vh):
    p = pl.program_id(1)
    qi = qi_ref[p]
    ki = ki_ref[p]

    @pl.when(ki == 0)
    def _():
        m_scr[...] = jnp.full(m_scr.shape, -jnp.inf, F32)
        acc_scr[...] = jnp.zeros(acc_scr.shape, F32)

    def scores(h, slot):
        s_scr[slot] = lax.dot_general(k_ref[h], q_ref[h], _NT, preferred_element_type=F32)

    def update(h, slot, masked):
        s = s_scr[slot]
        if masked:
            kpos = lax.broadcasted_iota(I32, s.shape, 0)
            qpos = lax.broadcasted_iota(I32, s.shape, 1)
            s = jnp.where(kpos <= qpos, s, -jnp.inf)
        m_prev = m_scr[h]
        m_new = jnp.maximum(m_prev, jnp.max(s, axis=0, keepdims=True))
        pr = jnp.exp2(s - m_new).astype(BF16)
        acc_scr[h] = jnp.exp2(m_prev - m_new) * acc_scr[h] + jnp.dot(vt_ref[h], pr, preferred_element_type=F32)
        m_scr[h] = m_new

    def run(masked):
        scores(0, 0)

        def pair(i, carry):
            h0 = 2 * i
            scores(h0 + 1, 1)
            update(h0, 0, masked)
            scores(jnp.minimum(h0 + 2, heads - 1), 0)
            update(h0 + 1, 1, masked)
            return carry

        lax.fori_loop(0, heads // 2, pair, 0)

    @pl.when(ki < qi)
    def _():
        run(False)

    @pl.when(ki == qi)
    def _():
        run(True)
        for h in range(heads):
            acc = acc_scr[h]
            o = (acc[:vh] / acc[vh:vh + 1]).T
            g = jax.nn.sigmoid(gate_ref[:, h * vh:(h + 1) * vh].astype(F32))
            o_ref[:, h * vh:(h + 1) * vh] = (o * g).astype(o_ref.dtype)


def _flash_prompt(q, k, vt, zb, gate_off, batch, seq):
    heads, n, width = q.shape
    vr = vt.shape[1]
    vh = vr - SUM_ROWS
    d = heads * vh
    assert heads % 2 == 0
    tq = _tile(seq, 512, LANES)
    nq = seq // tq
    pairs = [(a, b) for a in range(nq) for b in range(a + 1)]
    qi = jnp.asarray([a for a, _ in pairs], I32)
    ki = jnp.asarray([b for _, b in pairs], I32)
    assert gate_off % d == 0
    grid_spec = pltpu.PrefetchScalarGridSpec(
        num_scalar_prefetch=2,
        grid=(batch, len(pairs)),
        in_specs=[
            pl.BlockSpec((heads, tq, width), lambda b, p, qi, ki: (0, b * nq + qi[p], 0)),
            pl.BlockSpec((heads, tq, width), lambda b, p, qi, ki: (0, b * nq + ki[p], 0)),
            pl.BlockSpec((heads, vr, tq), lambda b, p, qi, ki: (0, 0, b * nq + ki[p])),
            pl.BlockSpec((tq, d), lambda b, p, qi, ki: (b * nq + qi[p], gate_off // d)),
        ],
        out_specs=pl.BlockSpec((tq, d), lambda b, p, qi, ki: (b * nq + qi[p], 0)),
        scratch_shapes=[pltpu.VMEM((heads, 1, tq), F32), pltpu.VMEM((heads, vr, tq), F32), pltpu.VMEM((2, tq, tq), F32)],
    )
    return pl.pallas_call(
        functools.partial(_flash_kernel, heads=heads, vh=vh),
        grid_spec=grid_spec,
        out_shape=jax.ShapeDtypeStruct((n, d), BF16),
        compiler_params=_cparams(("parallel", "arbitrary")),
        name="flash_prompt",
    )(qi, ki, q, k, vt, zb)


def _absorb_kernel(q_ref, w_ref, o_ref, *, nope):
    o_ref[0] = lax.dot_general(q_ref[0, :, :nope], w_ref[0], _NT, preferred_element_type=F32)


def _absorb_q(qs, w_uk_h):
    heads, ns, width = qs.shape
    _, kvl, nope = w_uk_h.shape
    return pl.pallas_call(
        functools.partial(_absorb_kernel, nope=nope),
        grid=(heads,),
        in_specs=[pl.BlockSpec((1, ns, width), lambda h: (h, 0, 0)), pl.BlockSpec((1, kvl, nope), lambda h: (h, 0, 0))],
        out_specs=pl.BlockSpec((1, ns, kvl), lambda h: (h, 0, 0)),
        out_shape=jax.ShapeDtypeStruct((heads, ns, kvl), F32),
        compiler_params=_cparams(("parallel",)),
        name="absorb_q",
    )(qs, w_uk_h)


def _decode_attn_kernel(pt_ref, ql_ref, qr_ref, cn_ref, krn_ref, cc_hbm, ckr_hbm, o_ref, kc_buf, kr_buf, sem, *, chunk_pages, n_chunks, n_slots, page, scale):
    s = pl.program_id(0)
    n_seq = pl.num_programs(0)

    def copies(seq, chunk, slot):
        out = []
        for j in range(chunk_pages):
            pg = pt_ref[seq, chunk * chunk_pages + j]
            out.append(pltpu.make_async_copy(cc_hbm.at[pg], kc_buf.at[slot, pl.ds(j * page, page)], sem.at[slot, 0]))
            out.append(pltpu.make_async_copy(ckr_hbm.at[pg], kr_buf.at[slot, :, pl.ds(j * page, page)], sem.at[slot, 1]))
        return out

    def start(seq, chunk, slot):
        for c in copies(seq, chunk, slot):
            c.start()

    ahead = n_slots - 1

    @pl.when(s == 0)
    def _():
        for c in range(ahead):
            start(0, c, c)

    ql = ql_ref[0]
    qr = qr_ref[0]
    cn = cn_ref[0]
    krn = krn_ref[0]
    m = (jnp.sum(ql * cn, axis=1, keepdims=True) + jnp.sum(qr * krn, axis=1, keepdims=True)) * scale
    l = jnp.ones_like(m)
    acc = jnp.broadcast_to(cn, ql.shape)
    for c in range(n_chunks):
        slot = c % n_slots
        nxt = c + ahead
        if nxt < n_chunks:
            start(s, nxt, nxt % n_slots)
        else:
            @pl.when(s + 1 < n_seq)
            def _(nxt=nxt):
                start(s + 1, nxt - n_chunks, nxt % n_slots)
        for cp in copies(s, c, slot):
            cp.wait()
        kc = kc_buf[slot]
        kr_t = kr_buf[slot]
        sc = (lax.dot_general(ql, kc, _NT, preferred_element_type=F32)
              + jnp.dot(qr, kr_t, preferred_element_type=F32)) * scale
        m_new = jnp.maximum(m, jnp.max(sc, axis=1, keepdims=True))
        alpha = jnp.exp2(m - m_new)
        pr = jnp.exp2(sc - m_new)
        l = alpha * l + jnp.sum(pr, axis=1, keepdims=True)
        acc = alpha * acc + jnp.dot(pr, kc, preferred_element_type=F32)
        m = m_new
    o_ref[0] = acc / l


def _decode_attn(page_table, q_lat, q_rope, c_new, kr_new, cache_c, cache_kr_t, scale):
    ns, heads, kvl = q_lat.shape
    rope = q_rope.shape[-1]
    n_pages = page_table.shape[1]
    page = cache_c.shape[1]
    assert cache_kr_t.shape[1:] == (rope, page)
    chunk_pages = _tile(n_pages, 16, 1)
    n_chunks = n_pages // chunk_pages
    assert n_chunks % 2 == 0
    n_slots = 4 if n_chunks % 4 == 0 else 2
    keys = chunk_pages * page
    grid_spec = pltpu.PrefetchScalarGridSpec(
        num_scalar_prefetch=1,
        grid=(ns,),
        in_specs=[
            pl.BlockSpec((1, heads, kvl), lambda s, pt: (s, 0, 0)),
            pl.BlockSpec((1, heads, rope), lambda s, pt: (s, 0, 0)),
            pl.BlockSpec((1, 1, kvl), lambda s, pt: (s, 0, 0)),
            pl.BlockSpec((1, 1, rope), lambda s, pt: (s, 0, 0)),
            pl.BlockSpec(memory_space=pl.ANY),
            pl.BlockSpec(memory_space=pl.ANY),
        ],
        out_specs=pl.BlockSpec((1, heads, kvl), lambda s, pt: (s, 0, 0)),
        scratch_shapes=[
            pltpu.VMEM((n_slots, keys, kvl), cache_c.dtype),
            pltpu.VMEM((n_slots, rope, keys), cache_kr_t.dtype),
            pltpu.SemaphoreType.DMA((n_slots, 2)),
        ],
    )
    return pl.pallas_call(
        functools.partial(_decode_attn_kernel, chunk_pages=chunk_pages, n_chunks=n_chunks, n_slots=n_slots, page=page, scale=scale),
        grid_spec=grid_spec,
        out_shape=jax.ShapeDtypeStruct((ns, heads, kvl), F32),
        compiler_params=_cparams(("arbitrary",)),
        name="decode_attn",
    )(page_table, q_lat, q_rope, c_new.reshape(ns, 1, kvl), kr_new.reshape(ns, 1, rope), cache_c, cache_kr_t)


def _unabsorb_kernel(o_ref, w_ref, gate_ref, y_ref):
    y = jnp.dot(o_ref[0].astype(BF16), w_ref[0], preferred_element_type=F32)
    y_ref[...] = (y * jax.nn.sigmoid(gate_ref[...].astype(F32))).astype(y_ref.dtype)


def _unabsorb_o(o_lat_h, w_uv_h, zb, gate_off):
    heads, ns, kvl = o_lat_h.shape
    vh = w_uv_h.shape[-1]
    assert gate_off % vh == 0
    return pl.pallas_call(
        _unabsorb_kernel,
        grid=(heads,),
        in_specs=[
            pl.BlockSpec((1, ns, kvl), lambda h: (h, 0, 0)),
            pl.BlockSpec((1, kvl, vh), lambda h: (h, 0, 0)),
            pl.BlockSpec((ns, vh), lambda h: (0, gate_off // vh + h)),
        ],
        out_specs=pl.BlockSpec((ns, vh), lambda h: (0, h)),
        out_shape=jax.ShapeDtypeStruct((ns, heads * vh), F32),
        compiler_params=_cparams(("parallel",)),
        name="unabsorb_o",
    )(o_lat_h, w_uv_h, zb)


def _segment_mid(bc, s):
    c, dk = bc.shape
    if 2 * s >= 8:
        n = c // (2 * s)
        mid = bc.reshape(n, 2 * s, dk)[:, s - 1:s, :]
        return jnp.broadcast_to(mid, (n, 2 * s, dk)).reshape(c, dk)
    r = lax.broadcasted_iota(I32, bc.shape, 0) & (2 * s - 1)
    prev1 = pltpu.roll(bc, 1, 0)
    if s == 1:
        return jnp.where(r == 0, bc, prev1)
    assert s == 2
    return jnp.where(r == 0, pltpu.roll(bc, c - 1, 0), jnp.where(r == 1, bc, jnp.where(r == 2, prev1, pltpu.roll(bc, 2, 0))))


def _gla_prompt_kernel(q_ref, k_ref, v_ref, go_ref, gg_ref, lr_ref, wa_ref, ba_ref, gn_ref, y_ref, st_ref, st_scr, la_scr, *, n_chunks, c, scale):
    tb = pl.program_id(2)

    @pl.when(tb == 0)
    def _():
        st_scr[...] = jnp.zeros(st_scr.shape, F32)

    x = jnp.dot(lr_ref[...], wa_ref[...], preferred_element_type=F32, precision=HI) + ba_ref[...]
    la_scr[...] = _log_sigmoid(x) / GLA_TAU

    row = lax.broadcasted_iota(I32, (c, c), 0)
    col = lax.broadcasted_iota(I32, (c, c), 1)
    levels = []
    s = c // 2
    while s >= 1:
        levels.append(s)
        s //= 2
    tril = jnp.where(col <= row, 1.0, 0.0)
    masks = []
    for s in levels:
        sh = s.bit_length() - 1
        masks.append(((row >> (sh + 1)) == (col >> (sh + 1))) & (((row >> sh) & 1) == 1) & (((col >> sh) & 1) == 0))
    eye = row == col

    def chunk(ci, carry):
        r0 = pl.multiple_of(ci * c, c)
        q = q_ref[pl.ds(r0, c), :].astype(F32) * scale
        k = k_ref[pl.ds(r0, c), :].astype(F32)
        v = v_ref[pl.ds(r0, c), :]
        bc = jnp.dot(tril, la_scr[pl.ds(r0, c), :], preferred_element_type=F32, precision=HI)
        bl = bc[c - 1:c, :]
        att = jnp.where(eye, jnp.sum(q * k, axis=1, keepdims=True), 0.0)
        for li, s in enumerate(levels):
            ref = _segment_mid(bc, s)
            qh = (q * jnp.exp(jnp.minimum(bc - ref, 0.0))).astype(BF16)
            kh = (k * jnp.exp(jnp.minimum(ref - bc, 0.0))).astype(BF16)
            a = lax.dot_general(qh, kh, _NT, preferred_element_type=F32)
            att = att + jnp.where(masks[li], a, 0.0)
        st = st_scr[...]
        qt = (q * jnp.exp(bc)).astype(BF16)
        o = (lax.dot_general(qt, st.astype(BF16), _NT, preferred_element_type=F32)
             + jnp.dot(att.astype(BF16), v, preferred_element_type=F32))
        kb = (k * jnp.exp(bl - bc)).astype(BF16)
        st_scr[...] = st * jnp.exp(bl) + lax.dot_general(v, kb, _TN, preferred_element_type=F32)
        on = _rms(o, gn_ref[...])
        go = go_ref[pl.ds(r0, c), :].astype(F32)
        gg = gg_ref[pl.ds(r0, c), :].astype(F32)
        y_ref[pl.ds(r0, c), :] = (on * (go * jax.nn.sigmoid(go)) * jax.nn.sigmoid(gg)).astype(y_ref.dtype)
        return carry

    lax.fori_loop(0, n_chunks, chunk, 0, unroll=math.gcd(n_chunks, GLA_UNROLL))

    @pl.when(tb == pl.num_programs(2) - 1)
    def _():
        st_ref[0, 0] = st_scr[...].T


def _gla_prompt(zb, zs, offs, w_a2p, b_a2, g_gla, batch, seq, gh, dk, dv):
    n = batch * seq
    c = math.gcd(seq, GLA_CHUNK)
    tb = _tile(seq, 512, c)
    nb = seq // tb
    for key, w in (("gq", dk), ("gk", dk), ("gv", dv), ("g_out", dv), ("gate_gla", dv)):
        assert offs[key] % w == 0
    assert offs["g_lr"] % LANES == 0

    def rows(b, h, t):
        return b * nb + t

    in_specs = [
        pl.BlockSpec((tb, dk), lambda b, h, t: (rows(b, h, t), offs["gq"] // dk + h)),
        pl.BlockSpec((tb, dk), lambda b, h, t: (rows(b, h, t), offs["gk"] // dk + h)),
        pl.BlockSpec((tb, dv), lambda b, h, t: (rows(b, h, t), offs["gv"] // dv + h)),
        pl.BlockSpec((tb, dv), lambda b, h, t: (rows(b, h, t), offs["g_out"] // dv + h)),
        pl.BlockSpec((tb, dv), lambda b, h, t: (rows(b, h, t), offs["gate_gla"] // dv + h)),
        pl.BlockSpec((tb, LANES), lambda b, h, t: (rows(b, h, t), offs["g_lr"] // LANES)),
        pl.BlockSpec((LANES, dk), lambda b, h, t: (0, h)),
        pl.BlockSpec((1, dk), lambda b, h, t: (0, h)),
        pl.BlockSpec((1, dv), lambda b, h, t: (0, 0)),
    ]
    out_specs = [
        pl.BlockSpec((tb, dv), lambda b, h, t: (rows(b, h, t), h)),
        pl.BlockSpec((1, 1, dk, dv), lambda b, h, t: (b, h, 0, 0)),
    ]
    return pl.pallas_call(
        functools.partial(_gla_prompt_kernel, n_chunks=tb // c, c=c, scale=dk ** -0.5),
        grid=(batch, gh, nb),
        in_specs=in_specs,
        out_specs=out_specs,
        out_shape=[jax.ShapeDtypeStruct((n, gh * dv), BF16), jax.ShapeDtypeStruct((batch, gh, dk, dv), F32)],
        scratch_shapes=[pltpu.VMEM((dv, dk), F32), pltpu.VMEM((tb, dk), F32)],
        compiler_params=_cparams(("parallel", "parallel", "arbitrary")),
        name="gla_prompt",
    )(zb, zb, zb, zb, zb, zs, w_a2p, b_a2.reshape(1, gh * dk), g_gla.reshape(1, dv))


def _gla_decode_kernel(qk_ref, lrt_ref, wat_ref, bcol_ref, v_ref, go_ref, gg_ref, gn_ref, s0_ref, y_ref, s1_ref, *, sb, scale):
    x = jnp.dot(wat_ref[0], lrt_ref[0], preferred_element_type=F32, precision=HI) + bcol_ref[0]
    a = jnp.exp(_log_sigmoid(x) / GLA_TAU)
    outs = []
    for u in range(sb):
        qc = qk_ref[0, 0, :, u:u + 1] * scale
        kc = qk_ref[0, 0, :, sb + u:sb + u + 1]
        sn = a[:, u:u + 1] * s0_ref[u, 0] + kc * v_ref[u:u + 1, :]
        s1_ref[u, 0] = sn
        outs.append(jnp.sum(qc * sn, axis=0, keepdims=True))
    o = jnp.concatenate(outs, axis=0)
    go = go_ref[...]
    y_ref[...] = _rms(o, gn_ref[...]) * (go * jax.nn.sigmoid(go)) * jax.nn.sigmoid(gg_ref[...])


def _gla_decode(qk_cols, lr_t, wa_t, b_col, gv, g_out, gate_gla, g_gla, state):
    ns, gh, dk, dv = state.shape
    sb = qk_cols.shape[-1] // 2
    return pl.pallas_call(
        functools.partial(_gla_decode_kernel, sb=sb, scale=dk ** -0.5),
        grid=(ns // sb, gh),
        in_specs=[
            pl.BlockSpec((1, 1, dk, 2 * sb), lambda i, h: (i, h, 0, 0)),
            pl.BlockSpec((1, LANES, sb), lambda i, h: (i, 0, 0)),
            pl.BlockSpec((1, dk, LANES), lambda i, h: (h, 0, 0)),
            pl.BlockSpec((1, dk, 1), lambda i, h: (h, 0, 0)),
            pl.BlockSpec((sb, dv), lambda i, h: (i, h)),
            pl.BlockSpec((sb, dv), lambda i, h: (i, h)),
            pl.BlockSpec((sb, dv), lambda i, h: (i, h)),
            pl.BlockSpec((1, dv), lambda i, h: (0, 0)),
            pl.BlockSpec((sb, 1, dk, dv), lambda i, h: (i, h, 0, 0)),
        ],
        out_specs=[pl.BlockSpec((sb, dv), lambda i, h: (i, h)), pl.BlockSpec((sb, 1, dk, dv), lambda i, h: (i, h, 0, 0))],
        out_shape=[jax.ShapeDtypeStruct((ns, gh * dv), F32), jax.ShapeDtypeStruct((ns, gh, dk, dv), F32)],
        compiler_params=_cparams(("parallel", "parallel")),
        name="gla_decode",
    )(qk_cols, lr_t, wa_t, b_col, gv, g_out, gate_gla, g_gla.reshape(1, dv), state)


def _outproj_kernel(ym_ref, yg_ref, h_ref, w_ref, g_ref, wr_ref, br_ref, h1_ref, m_ref, route_ref):
    mix = (ym_ref[...].astype(F32) + yg_ref[...].astype(F32)).astype(BF16)
    h1 = h_ref[...] + jnp.dot(mix, w_ref[...], preferred_element_type=F32)
    h1_ref[...] = h1
    m = _rms(h1, g_ref[...])
    half = m.shape[1] // 2
    m_ref[...] = _pack_bf16_pair(m[:, :half], m[:, half:])
    logits = jnp.dot(m.astype(BF16), wr_ref[...].astype(BF16), preferred_element_type=F32) + br_ref[...]
    lane = lax.broadcasted_iota(I32, logits.shape, 1).astype(F32)
    vals, idxs = [], []
    for _ in range(TOP_K):
        mx = jnp.max(logits, axis=1, keepdims=True)
        ix = jnp.min(jnp.where(logits == mx, lane, float(LANES)), axis=1, keepdims=True)
        vals.append(mx)
        idxs.append(ix)
        logits = jnp.where(lane == ix, -jnp.inf, logits)
    ex = [jnp.exp(v - vals[0]) for v in vals]
    den = ex[0]
    for e in ex[1:]:
        den = den + e
    route = jnp.zeros(logits.shape, F32)
    for kk in range(TOP_K):
        route = jnp.where(lane == float(kk), idxs[kk], route)
        route = jnp.where(lane == float(TOP_K + kk), ex[kk] / den, route)
    route_ref[...] = route


def _outproj(y_mla, y_gla, h, w_out_b, g_ffn, w_router_p, b_router_p):
    n, d = h.shape
    tm = _tile(n, 256, 16)
    row = lambda i: (i, 0)
    fixed = lambda i: (0, 0)
    return pl.pallas_call(
        _outproj_kernel,
        grid=(n // tm,),
        in_specs=[
            pl.BlockSpec((tm, d), row), pl.BlockSpec((tm, d), row), pl.BlockSpec((tm, d), row),
            pl.BlockSpec((d, d), fixed), pl.BlockSpec((1, d), fixed),
            pl.BlockSpec((d, LANES), fixed), pl.BlockSpec((1, LANES), fixed),
        ],
        out_specs=[pl.BlockSpec((tm, d), row), pl.BlockSpec((tm, d // 2), row), pl.BlockSpec((tm, LANES), row)],
        out_shape=[jax.ShapeDtypeStruct((n, d), F32), jax.ShapeDtypeStruct((n, d // 2), U32), jax.ShapeDtypeStruct((n, LANES), F32)],
        compiler_params=_cparams(("parallel",)),
        name="outproj",
    )(y_mla, y_gla, h, w_out_b, g_ffn.reshape(1, d), w_router_p, b_router_p)


def _dispatch_kernel(idx_ref, x_ref, o_ref, g_scr, *, rows):
    base = pl.program_id(0) * rows

    def body(r, carry):
        tok = idx_ref[base + r]
        g_scr[pl.ds(r, 1), :] = x_ref[pl.ds(tok, 1), :]
        return carry

    lax.fori_loop(0, rows, body, 0, unroll=8)
    hi, lo = _unpack_bf16_pair(g_scr[...])
    o_ref[...] = jnp.concatenate([hi, lo], axis=1).astype(BF16)


def _dispatch(buf_tok, m_packed):
    rows_total = buf_tok.shape[0]
    n, w = m_packed.shape
    rows = _tile(rows_total, 512, 16)
    grid_spec = pltpu.PrefetchScalarGridSpec(
        num_scalar_prefetch=1,
        grid=(rows_total // rows,),
        in_specs=[pl.BlockSpec(memory_space=pltpu.VMEM)],
        out_specs=pl.BlockSpec((rows, 2 * w), lambda i, idx: (i, 0)),
        scratch_shapes=[pltpu.VMEM((rows, w), U32)],
    )
    return pl.pallas_call(
        functools.partial(_dispatch_kernel, rows=rows),
        grid_spec=grid_spec,
        out_shape=jax.ShapeDtypeStruct((rows_total, 2 * w), BF16),
        compiler_params=_cparams(("arbitrary",), 56),
        name="moe_dispatch",
    )(buf_tok, m_packed)


def _for_row_chunks(b0, b1, rb, max_blocks, fn):
    n = b1 - b0
    size = 1 << (max_blocks.bit_length() - 1)
    while size >= 1:
        start = b0 + (n & ~(2 * size - 1))

        @pl.when((n & size) != 0)
        def _(start=start, size=size):
            fn(pl.multiple_of(start * rb, rb), size * rb)

        size //= 2


def _moe_kernel(ve_ref, blk_ref, nb_ref, zf_ref, xs_ref, wg_ref, wu_ref, bg_ref, bu_ref, wd_ref, bd_ref, ys_hbm,
                y_scr, o_scr, sem, *, n_ff, n_slab, dn, rb):
    v = pl.program_id(0)
    j = pl.program_id(1)
    blk0 = blk_ref[v]
    nb = nb_ref[v]
    b0, b1 = 0, nb
    fill = zf_ref[v] == 1
    work = jnp.logical_not(fill)
    max_blocks = xs_ref.shape[0] // rb
    d = y_scr.shape[1]
    slab = d // n_slab

    def out_copy(b):
        return pltpu.make_async_copy(o_scr.at[pl.ds(pl.multiple_of(b * rb, rb), rb)], ys_hbm.at[blk0 + b], sem.at[0])

    def flush():
        lax.fori_loop(0, nb, lambda b, c: (out_copy(b).start(), c)[1], 0)
        lax.fori_loop(0, nb, lambda b, c: (out_copy(b).wait(), c)[1], 0)

    @pl.when(fill & (j == 0))
    def _():
        def zero(b, carry):
            o_scr[pl.ds(pl.multiple_of(b * rb, rb), rb), :] = jnp.zeros((rb, o_scr.shape[1]), U32)
            return carry
        lax.fori_loop(0, nb, zero, 0)
        flush()

    @pl.when(work & (j == 0))
    def _():
        def init(r0, rows):
            y_scr[pl.ds(r0, rows), :] = jnp.broadcast_to(bd_ref[0], (rows, d))
        _for_row_chunks(b0, b1, rb, max_blocks, init)

    @pl.when(work)
    def _():
        def mlp(r0, rows):
            x = xs_ref[pl.ds(r0, rows), :]
            g = jnp.dot(x, wg_ref[0].astype(BF16), preferred_element_type=F32) + bg_ref[0]
            u = jnp.dot(x, wu_ref[0].astype(BF16), preferred_element_type=F32) + bu_ref[0]
            gate = jnp.minimum(g, SWIGLU_LIMIT)
            up = jnp.clip(u, -SWIGLU_LIMIT, SWIGLU_LIMIT)
            glu = gate * jax.nn.sigmoid(gate * SWIGLU_ALPHA)
            a = ((up + 1.0) * glu).astype(BF16)
            for c0 in range(0, d, dn):
                w = wd_ref[0, :, c0:c0 + dn].astype(BF16)
                y_scr[pl.ds(r0, rows), c0:c0 + dn] += jnp.dot(a, w, preferred_element_type=F32)
        _for_row_chunks(b0, b1, rb, max_blocks, mlp)

    @pl.when(work & (j == n_ff - 1))
    def _():
        def emit(r0, rows):
            for s in range(n_slab):
                y = y_scr[pl.ds(r0, rows), s * slab:(s + 1) * slab]
                o_scr[pl.ds(r0, rows), s * (slab // 2):(s + 1) * (slab // 2)] = _pack_bf16_pair(y[:, :slab // 2], y[:, slab // 2:])
        _for_row_chunks(b0, b1, rb, max_blocks, emit)
        flush()


def _moe_experts(meta, xs, w_gu, b_gu, w_down, b_down, rows_total, window_rows):
    v_exp, v_blk, v_nb, v_zero = meta
    d = xs.shape[1]
    e, _, f2 = w_gu.shape
    rb = EXPERT_ROW_BLOCK
    ff = f2 // 2
    tf = _tile(ff, 256, LANES)
    n_ff = ff // tf
    n_slab = d // _tile(d, MOE_SLAB, 2 * LANES)
    n_visits = v_exp.shape[0]
    ff_idx = lambda j, zf: jnp.where(zf == 1, n_ff - 1, j)
    grid_spec = pltpu.PrefetchScalarGridSpec(
        num_scalar_prefetch=4,
        grid=(n_visits, n_ff),
        in_specs=[
            pl.BlockSpec((pl.Element(window_rows), pl.Element(d)), lambda v, j, ve, blk, nb, zf: (blk[v] * rb, 0)),
            pl.BlockSpec((1, d, tf), lambda v, j, ve, blk, nb, zf: (ve[v], 0, ff_idx(j, zf[v]))),
            pl.BlockSpec((1, d, tf), lambda v, j, ve, blk, nb, zf: (ve[v], 0, n_ff + ff_idx(j, zf[v]))),
            pl.BlockSpec((1, 1, tf), lambda v, j, ve, blk, nb, zf: (ve[v], 0, ff_idx(j, zf[v]))),
            pl.BlockSpec((1, 1, tf), lambda v, j, ve, blk, nb, zf: (ve[v], 0, n_ff + ff_idx(j, zf[v]))),
            pl.BlockSpec((1, tf, d), lambda v, j, ve, blk, nb, zf: (ve[v], ff_idx(j, zf[v]), 0)),
            pl.BlockSpec((1, 1, d), lambda v, j, ve, blk, nb, zf: (ve[v], 0, 0)),
        ],
        out_specs=pl.BlockSpec(memory_space=pl.ANY),
        scratch_shapes=[pltpu.VMEM((window_rows, d), F32), pltpu.VMEM((window_rows, d // 2), U32), pltpu.SemaphoreType.DMA((1,))],
    )
    return pl.pallas_call(
        functools.partial(_moe_kernel, n_ff=n_ff, n_slab=n_slab, dn=_tile(d, 512, LANES), rb=rb),
        grid_spec=grid_spec,
        out_shape=jax.ShapeDtypeStruct((rows_total // rb, rb, d // 2), U32),
        compiler_params=_cparams(("arbitrary", "arbitrary"), 56),
        name="moe_experts",
    )(v_exp, v_blk, v_nb, v_zero, xs, w_gu, w_gu, b_gu.reshape(e, 1, f2), b_gu.reshape(e, 1, f2), w_down, b_down.reshape(e, 1, d))


def _combine_kernel(dest_ref, ys_ref, route_ref, o_ref, stage, *, tm):
    base = pl.program_id(1) * tm

    def body(t, carry):
        for kk in range(TOP_K):
            r = dest_ref[(base + t) * TOP_K + kk]
            stage[kk, pl.ds(t, 1), :] = ys_ref[pl.ds(r, 1), :]
        return carry

    lax.fori_loop(0, tm, body, 0, unroll=4)
    acc_hi = acc_lo = None
    for kk in range(TOP_K):
        g = route_ref[:, TOP_K + kk:TOP_K + kk + 1]
        hi, lo = _unpack_bf16_pair(stage[kk])
        acc_hi = g * hi if acc_hi is None else acc_hi + g * hi
        acc_lo = g * lo if acc_lo is None else acc_lo + g * lo
    o_ref[...] = jnp.concatenate([acc_hi, acc_lo], axis=1)


def _combine(dest, ys, route):
    rows_total, packed = ys.shape
    wh = MOE_SLAB // 2
    n_down = packed // wh
    n = route.shape[0]
    tm = _tile(n, 256, 8)
    grid_spec = pltpu.PrefetchScalarGridSpec(
        num_scalar_prefetch=1,
        grid=(n_down, n // tm),
        in_specs=[
            pl.BlockSpec((rows_total, wh), lambda c, i, dest: (0, c)),
            pl.BlockSpec((tm, LANES), lambda c, i, dest: (i, 0)),
        ],
        out_specs=pl.BlockSpec((tm, 2 * wh), lambda c, i, dest: (i, c)),
        scratch_shapes=[pltpu.VMEM((TOP_K, tm, wh), U32)],
    )
    return pl.pallas_call(
        functools.partial(_combine_kernel, tm=tm),
        grid_spec=grid_spec,
        out_shape=jax.ShapeDtypeStruct((n, n_down * 2 * wh), F32),
        compiler_params=_cparams(("arbitrary", "arbitrary"), 56),
        name="moe_combine",
    )(dest, ys, route)


def _dest_kernel(route_ref, starts_ref, o_ref, cnt_scr):
    @pl.when(pl.program_id(0) == 0)
    def _():
        cnt_scr[...] = jnp.zeros(cnt_scr.shape, F32)

    tb = route_ref.shape[0]
    lane = lax.broadcasted_iota(I32, (tb, LANES), 1).astype(F32)
    row = lax.broadcasted_iota(I32, (tb, tb), 0)
    col = lax.broadcasted_iota(I32, (tb, tb), 1)
    before = jnp.where(col < row, 1.0, 0.0).astype(BF16)
    base = starts_ref[...] + cnt_scr[...]
    out = jnp.zeros((tb, LANES), F32)
    for kk in range(TOP_K):
        hit = lane == route_ref[:, kk:kk + 1]
        seen = jnp.dot(before, jnp.where(hit, 1.0, 0.0).astype(BF16), preferred_element_type=F32)
        dest = jnp.sum(jnp.where(hit, seen + base, 0.0), axis=1, keepdims=True)
        out = jnp.where(lane == float(kk), dest, out)
        base = base + jnp.sum(jnp.where(hit, 1.0, 0.0), axis=0, keepdims=True)
    cnt_scr[...] = base - starts_ref[...]
    o_ref[...] = out


def _dest_rows(route, starts_row):
    n = route.shape[0]
    tb = _tile(n, 512, 8)
    return pl.pallas_call(
        _dest_kernel,
        grid=(n // tb,),
        in_specs=[pl.BlockSpec((tb, LANES), lambda i: (i, 0)), pl.BlockSpec((1, LANES), lambda i: (0, 0))],
        out_specs=pl.BlockSpec((tb, LANES), lambda i: (i, 0)),
        out_shape=jax.ShapeDtypeStruct((n, LANES), F32),
        scratch_shapes=[pltpu.VMEM((1, LANES), F32)],
        compiler_params=_cparams(("arbitrary",)),
        name="moe_dest",
    )(route, starts_row)


def _route_meta(route, n_experts, window_rows):
    n_tok = route.shape[0]
    n_asg = n_tok * TOP_K
    rb = EXPERT_ROW_BLOCK
    e_flat = route[:, :TOP_K].astype(I32).reshape(n_asg)
    tok_flat = (jnp.arange(n_asg, dtype=I32) // TOP_K).astype(I32)
    counts = jnp.sum(jax.nn.one_hot(e_flat, n_experts, dtype=I32), axis=0)
    padded = (counts + rb - 1) // rb * rb
    ends = jnp.cumsum(padded)
    starts = ends - padded
    starts_row = jnp.pad(starts.astype(F32), (0, LANES - n_experts)).reshape(1, LANES)
    dest = _dest_rows(route, starts_row)[:, :TOP_K].astype(I32).reshape(n_asg)
    rows_total = (-(-n_asg // rb) + n_experts) * rb
    rows_alloc = -(-(rows_total + window_rows) // 512) * 512
    buf_tok = jnp.zeros((rows_alloc,), I32).at[dest].set(tok_flat, unique_indices=True, mode="promise_in_bounds")
    mb = window_rows // rb
    n_blk = (padded // rb).astype(I32)
    vpe = -(-n_blk // mb)
    cv = jnp.cumsum(vpe).astype(I32)
    n_work = cv[-1]
    total_blk = (ends[-1] // rb).astype(I32)
    tail_blk = rows_total // rb - total_blk
    n_fill = -(-tail_blk // mb)
    n_visits = n_experts + rows_total // rb // mb + 1 + -(-n_experts // mb)
    pos = jnp.arange(n_visits, dtype=I32)
    is_work = pos < n_work
    e = jnp.minimum(jnp.searchsorted(cv, pos, side="right"), n_experts - 1).astype(I32)
    k = pos - (cv[e] - vpe[e])
    f = pos - n_work
    is_fill = (f >= 0) & (f < n_fill)
    blk = jnp.where(is_work, (starts[e] // rb).astype(I32) + k * mb, total_blk + jnp.clip(f, 0, n_fill) * mb)
    blk = jnp.minimum(blk, rows_total // rb)
    nb = jnp.where(is_work, jnp.clip(n_blk[e] - k * mb, 0, mb), jnp.where(is_fill, jnp.clip(tail_blk - f * mb, 0, mb), 0))
    v_exp = jnp.where(is_work, e, e[jnp.maximum(n_work - 1, 0)])
    v_zero = jnp.logical_not(is_work).astype(I32)
    return buf_tok, dest, (v_exp.astype(I32), blk.astype(I32), nb.astype(I32), v_zero), rows_total


def _ple_kernel(h1_ref, moe_ref, p_ref, wp_ref, wg_ref, gp_ref, gf_ref, y_ref, *, final):
    h2 = h1_ref[...] + moe_ref[...]
    u = _rms(h2, gp_ref[...]).astype(BF16)
    gate = jax.nn.sigmoid(jnp.dot(u, wg_ref[...], preferred_element_type=F32))
    pe = jnp.dot(p_ref[...].astype(BF16), wp_ref[...], preferred_element_type=F32)
    y = h2 + pe * gate
    y_ref[...] = _rms(y, gf_ref[...]) if final else y


def _ple_final(h1, moe, row_off, p, w_ple_b, w_ple_gate_b, g_ple, g_final, final):
    n, d = h1.shape
    pd = p.shape[1]
    tm = _tile(n, 256, 8)
    assert row_off % tm == 0
    row = lambda i: (i, 0)
    fixed = lambda i: (0, 0)
    return pl.pallas_call(
        functools.partial(_ple_kernel, final=final),
        grid=(n // tm,),
        in_specs=[
            pl.BlockSpec((tm, d), row), pl.BlockSpec((tm, d), lambda i: (row_off // tm + i, 0)), pl.BlockSpec((tm, pd), row),
            pl.BlockSpec((pd, d), fixed), pl.BlockSpec((d, d), fixed), pl.BlockSpec((1, d), fixed), pl.BlockSpec((1, d), fixed),
        ],
        out_specs=pl.BlockSpec((tm, d), row),
        out_shape=jax.ShapeDtypeStruct((n, d), F32),
        compiler_params=_cparams(("parallel",)),
        name="ple_final",
    )(h1, moe, p, w_ple_b, w_ple_gate_b, g_ple.reshape(1, d), g_final.reshape(1, d))


def _rope_table(pos, rope):
    half = rope // 2
    inv_freq = jnp.power(ROPE_THETA, -jnp.arange(half, dtype=F32) / half)
    ang = pos.astype(F32)[:, None] * inv_freq[None, :]
    cos, sin = jnp.cos(ang), jnp.sin(ang)
    return jnp.concatenate([cos, cos, -sin, sin], axis=1)


def _swap_halves(w):
    half = w.shape[-1] // 2
    return jnp.concatenate([w[..., half:], w[..., :half]], axis=-1)


def kernel(x_prompt, x_sample, cache_kv_latent, cache_k_rope, state_gla, page_table, p_prompt, p_sample, g_mix, w_in, g_q_a, w_uq, g_kv_a, w_uk, w_uv, w_a2, b_a2, g_gla, w_out, g_ffn, w_router, b_router, w_gu, b_gu, w_down, b_down, g_ple, w_ple, w_ple_gate, g_final):
    depth = w_in.shape[0]
    batch, seq, d = x_prompt.shape
    ns, dec_seq, _ = x_sample.shape
    assert dec_seq == 1
    ql, kvl = g_q_a.shape[-1], g_kv_a.shape[-1]
    rope = cache_k_rope.shape[-1]
    heads, nope = w_uk.shape[2], w_uk.shape[3]
    vh = w_uv.shape[3]
    _, _, gh, dk, dv = state_gla.shape
    lr = w_a2.shape[1]
    n_experts = w_router.shape[-1]
    past = page_table.shape[1] * cache_kv_latent.shape[2]
    assert 2 * rope == LANES and nope == LANES and heads * vh == d and lr <= LANES and n_experts <= LANES
    scale = LOG2E / math.sqrt(nope + rope)
    n_p = batch * seq
    big_segs = (("gv", gh * dv), ("g_out", gh * dv), ("gate_mla", d), ("gate_gla", d), ("gq", gh * dk), ("gk", gh * dk))
    offs, o = {}, 0
    for key, w in big_segs:
        offs[key] = o
        o += w
    offs["g_lr"] = ql + kvl + 2 * rope
    in_splits = (ql, kvl, rope, gh * dk, gh * dk, gh * dv, gh * dv, lr, d, d)
    in_offsets = tuple(sum(in_splits[:i + 1]) for i in range(len(in_splits) - 1))

    tab_p = jnp.tile(_rope_table(jnp.arange(seq), rope), (batch, 1))
    tab_s = _rope_table(jnp.full((ns,), past, I32), rope)

    h_p = x_prompt.reshape(n_p, d)
    h_s = x_sample.reshape(ns, d)
    kv_p, kr_p, st_p, kv_s, kr_s, st_s = [], [], [], [], [], []
    for l in range(depth):
        seg = dict(zip(("c_q", "c_kv", "k_r", "gq", "gk", "gv", "g_out", "g_lr", "gate_mla", "gate_gla"), jnp.split(w_in[l], in_offsets, axis=1)))
        w_big = jnp.concatenate([seg[k] for k, _ in big_segs], axis=1).astype(BF16)
        w_small = jnp.concatenate([seg["c_q"], seg["c_kv"], seg["k_r"], _swap_halves(seg["k_r"]),
                                   jnp.pad(seg["g_lr"], ((0, 0), (0, LANES - lr)))], axis=1).astype(BF16)
        wq = w_uq[l].reshape(ql, heads, nope + rope)
        w_uq_p = jnp.concatenate([wq[..., :nope], wq[..., nope:], _swap_halves(wq[..., nope:])], axis=-1).transpose(1, 0, 2).astype(BF16)
        w_uk_h = w_uk[l].transpose(1, 0, 2).astype(BF16)
        w_uv_h = w_uv[l].transpose(1, 0, 2).astype(BF16)
        w_uk_f = w_uk[l].reshape(kvl, heads * nope).astype(BF16)
        w_uv_tf = w_uv[l].reshape(kvl, heads * vh).T.astype(BF16)
        w_a2p = jnp.pad(w_a2[l], ((0, LANES - lr), (0, 0)))
        w_out_b = w_out[l].astype(BF16)
        w_router_p = jnp.pad(w_router[l], ((0, 0), (0, LANES - n_experts)))
        b_router_p = jnp.pad(b_router[l], (0, LANES - n_experts), constant_values=NEG_BIG).reshape(1, LANES)
        w_ple_b = w_ple[l].astype(BF16)
        w_ple_gate_b = w_ple_gate[l].astype(BF16)

        def front(h, tab, z_dtype, q_scale):
            a = _rms_cast(h, g_mix[l])
            zb = _matmul(a, w_big, z_dtype, "inproj_big")
            zs = _matmul(a, w_small, F32, "inproj_small")
            q = _q_proj(zs, g_q_a[l], w_uq_p, tab, q_scale)
            return zb, zs, q

        zb, zs, q = front(h_p, tab_p, BF16, scale)
        c_kv, k_r, k_heads, v_t = _kv_prep(zs, g_kv_a[l], tab_p, w_uk_f, w_uv_tf, heads, ql, rope, True)
        y_mla = _flash_prompt(q, k_heads, v_t, zb, offs["gate_mla"], batch, seq)
        y_gla, s_new = _gla_prompt(zb, zs, offs, w_a2p, b_a2[l], g_gla[l], batch, seq, gh, dk, dv)
        h1_p, m_p, route_p = _outproj(y_mla, y_gla, h_p, w_out_b, g_ffn[l], w_router_p, b_router_p)
        kv_p.append(c_kv.reshape(batch, seq, kvl)); kr_p.append(k_r.reshape(batch, seq, rope)); st_p.append(s_new)

        zb, zs, q = front(h_s, tab_s, F32, 1.0)
        c_kv, k_r = _kv_prep(zs, g_kv_a[l], tab_s, w_uk_f, w_uv_tf, heads, ql, rope, False)
        q_lat = _absorb_q(q, w_uk_h).transpose(1, 0, 2)
        q_rope = q[:, :, nope:nope + rope].astype(F32).transpose(1, 0, 2)
        o_lat = _decode_attn(page_table, q_lat, q_rope, c_kv, k_r, cache_kv_latent[l], cache_k_rope[l].transpose(0, 2, 1), scale)
        y_mla = _unabsorb_o(o_lat.transpose(1, 0, 2), w_uv_h, zb, offs["gate_mla"])
        sb = 8
        zf = zb.astype(F32)
        cols = lambda key, w: zf[:, offs[key]:offs[key] + w]
        to_cols = lambda z: z.reshape(ns // sb, sb, gh, dk).transpose(0, 2, 3, 1)
        qk_cols = jnp.concatenate([to_cols(cols("gq", gh * dk)), to_cols(cols("gk", gh * dk))], axis=-1)
        lr_t = zs[:, offs["g_lr"]:offs["g_lr"] + LANES].reshape(ns // sb, sb, LANES).transpose(0, 2, 1)
        wa_t = w_a2p.reshape(LANES, gh, dk).transpose(1, 2, 0)
        b_col = b_a2[l].reshape(gh, dk, 1)
        y_gla, s_new = _gla_decode(qk_cols, lr_t, wa_t, b_col, cols("gv", gh * dv), cols("g_out", gh * dv),
                                   cols("gate_gla", gh * dv), g_gla[l], state_gla[l])
        h1_s, m_s, route_s = _outproj(y_mla, y_gla, h_s, w_out_b, g_ffn[l], w_router_p, b_router_p)
        kv_s.append(c_kv.reshape(ns, 1, kvl)); kr_s.append(k_r.reshape(ns, 1, rope)); st_s.append(s_new)

        m_all = jnp.concatenate([m_p, m_s], axis=0)
        route = jnp.concatenate([route_p, route_s], axis=0)
        buf_tok, dest, meta, rows_total = _route_meta(route, n_experts, MOE_WINDOW_ROWS)
        xs = _dispatch(buf_tok, m_all)
        ys = _moe_experts(meta, xs, w_gu[l], b_gu[l], w_down[l], b_down[l], rows_total, MOE_WINDOW_ROWS)
        ys = ys.reshape(rows_total, d // 2)
        moe = _combine(dest, ys, route)

        last = l == depth - 1
        h_p = _ple_final(h1_p, moe, 0, p_prompt[l].reshape(n_p, -1), w_ple_b, w_ple_gate_b, g_ple[l], g_final, last)
        h_s = _ple_final(h1_s, moe, n_p, p_sample[l].reshape(ns, -1), w_ple_b, w_ple_gate_b, g_ple[l], g_final, last)

    return (h_p.reshape(batch, seq, d), h_s.reshape(ns, 1, d), jnp.stack(kv_p), jnp.stack(kr_p), jnp.stack(st_p),
            jnp.stack(kv_s), jnp.stack(kr_s), jnp.stack(st_s))
```

```python
import functools
import math

import jax
import jax.numpy as jnp
from jax import lax
from jax.experimental import pallas as pl
from jax.experimental.pallas import tpu as pltpu

F32 = jnp.float32
BF16 = jnp.bfloat16
U32 = jnp.uint32
I32 = jnp.int32

EPS = 1e-6
ROPE_THETA = 10000.0
GLA_TAU = 16.0
GLA_CHUNK = 64
GLA_UNROLL = 4
TOP_K = 4
SWIGLU_LIMIT = 7.0
SWIGLU_ALPHA = 1.702
EXPERT_ROW_BLOCK = 128
MOE_SLAB = 256
MOE_WINDOW_ROWS = 1536
LANES = 128
MIB = 1024 * 1024
SUM_ROWS = 16
LOG2E = 1.4426950408889634
HI = lax.Precision.HIGHEST
NEG_BIG = -1e30

_NT = (((1,), (1,)), ((), ()))
_TN = (((0,), (0,)), ((), ()))


def _cparams(sem, vmem_mib=48):
    return pltpu.CompilerParams(dimension_semantics=sem, vmem_limit_bytes=vmem_mib * MIB)


def _tile(n, cap, mult):
    best = None
    for t in range(mult, min(n, cap) + 1, mult):
        if n % t == 0:
            best = t
    assert best is not None, (n, cap, mult)
    return best


def _rms(x, g):
    return x * lax.rsqrt(jnp.mean(x * x, axis=-1, keepdims=True) + EPS) * g


def _log_sigmoid(x):
    return jnp.minimum(x, 0.0) - jnp.log1p(jnp.exp(-jnp.abs(x)))


def _pack_bf16_pair(hi, lo):
    hb = pltpu.bitcast(hi.astype(BF16).astype(F32), U32)
    lb = pltpu.bitcast(lo.astype(BF16).astype(F32), U32)
    return hb | (lb >> 16)


def _unpack_bf16_pair(w):
    hi = pltpu.bitcast(w & jnp.uint32(0xFFFF0000), F32)
    lo = pltpu.bitcast(w << 16, F32)
    return hi, lo


def _rms_cast_kernel(x_ref, g_ref, o_ref):
    o_ref[...] = _rms(x_ref[...], g_ref[...]).astype(o_ref.dtype)


def _rms_cast(x, g):
    n, d = x.shape
    tm = _tile(n, 512, 16)
    return pl.pallas_call(
        _rms_cast_kernel,
        grid=(n // tm,),
        in_specs=[pl.BlockSpec((tm, d), lambda i: (i, 0)), pl.BlockSpec((1, d), lambda i: (0, 0))],
        out_specs=pl.BlockSpec((tm, d), lambda i: (i, 0)),
        out_shape=jax.ShapeDtypeStruct((n, d), BF16),
        compiler_params=_cparams(("parallel",)),
        name="rms_cast",
    )(x, g.reshape(1, d))


def _mm_kernel(x_ref, w_ref, o_ref):
    o_ref[...] = jnp.dot(x_ref[...], w_ref[...], preferred_element_type=F32).astype(o_ref.dtype)


def _matmul(x, w, out_dtype, name):
    m, k = x.shape
    n = w.shape[1]
    tm = _tile(m, 1024, 16)
    tn = _tile(n, 1280, LANES)
    return pl.pallas_call(
        _mm_kernel,
        grid=(n // tn, m // tm),
        in_specs=[pl.BlockSpec((tm, k), lambda j, i: (i, 0)), pl.BlockSpec((k, tn), lambda j, i: (0, j))],
        out_specs=pl.BlockSpec((tm, tn), lambda j, i: (i, j)),
        out_shape=jax.ShapeDtypeStruct((m, n), out_dtype),
        compiler_params=_cparams(("parallel", "parallel")),
        name=name,
    )(x, w)


def _rope_combine(y2, tab):
    half = y2.shape[1] // 2
    t = y2 * tab
    r = t + pltpu.roll(t, half, axis=1)
    lane = lax.broadcasted_iota(I32, r.shape, 1)
    return jnp.where(lane < half, r, 0.0)


def _q_proj_kernel(cq_ref, g_ref, w_ref, tab_ref, q_ref, *, heads, nope, scale):
    cqn = _rms(cq_ref[...], g_ref[...]).astype(BF16)
    tab = tab_ref[...]
    for h in range(heads):
        y = jnp.dot(cqn, w_ref[h], preferred_element_type=F32)
        q_ref[h, :, :nope] = (y[:, :nope] * scale).astype(BF16)
        q_ref[h, :, nope:] = (_rope_combine(y[:, nope:], tab) * scale).astype(BF16)


def _q_proj(zs, g_q_a, w_uq_p, tab, scale):
    n = zs.shape[0]
    heads, ql, width = w_uq_p.shape
    tm = _tile(n, 512, 16)
    return pl.pallas_call(
        functools.partial(_q_proj_kernel, heads=heads, nope=width // 2, scale=scale),
        grid=(n // tm,),
        in_specs=[
            pl.BlockSpec((tm, ql), lambda i: (i, 0)),
            pl.BlockSpec((1, ql), lambda i: (0, 0)),
            pl.BlockSpec((heads, ql, width), lambda i: (0, 0, 0)),
            pl.BlockSpec((tm, LANES), lambda i: (i, 0)),
        ],
        out_specs=pl.BlockSpec((heads, tm, width), lambda i: (0, i, 0)),
        out_shape=jax.ShapeDtypeStruct((heads, n, width), BF16),
        compiler_params=_cparams(("parallel",)),
        name="q_proj",
    )(zs, g_q_a.reshape(1, ql), w_uq_p, tab)


def _kv_prep_kernel(ckv_ref, krs_ref, g_ref, tab_ref, wuk_ref, wuvt_ref, c_ref, kr_ref, k_ref, vt_ref, *, heads, nope, rope, vh):
    c = _rms(ckv_ref[...], g_ref[...])
    c_ref[...] = c
    cb = c.astype(BF16)
    r = _rope_combine(krs_ref[...], tab_ref[...])
    kr_ref[...] = r[:, :rope]
    rb = r.astype(BF16)
    kn = jnp.dot(cb, wuk_ref[...], preferred_element_type=F32).astype(BF16)
    vt = lax.dot_general(wuvt_ref[...], cb, _NT, preferred_element_type=F32).astype(BF16)
    for h in range(heads):
        k_ref[h, :, :nope] = kn[:, h * nope:(h + 1) * nope]
        k_ref[h, :, nope:] = rb
        vt_ref[h, :vh, :] = vt[h * vh:(h + 1) * vh, :]
        vt_ref[h, vh:, :] = jnp.ones((SUM_ROWS, cb.shape[0]), BF16)


def _latent_kernel(ckv_ref, krs_ref, g_ref, tab_ref, c_ref, kr_ref, *, rope):
    c_ref[...] = _rms(ckv_ref[...], g_ref[...])
    kr_ref[...] = _rope_combine(krs_ref[...], tab_ref[...])[:, :rope]


def _kv_prep(zs, g_kv_a, tab, w_uk_f, w_uv_tf, heads, ql, rope, with_kv):
    n = zs.shape[0]
    kvl = w_uk_f.shape[0]
    nope = w_uk_f.shape[1] // heads
    vh = w_uv_tf.shape[0] // heads
    assert ql % kvl == 0 and (ql + kvl) % LANES == 0
    tm = _tile(n, 640, LANES if with_kv else 8)
    in_specs = [
        pl.BlockSpec((tm, kvl), lambda i: (i, ql // kvl)),
        pl.BlockSpec((tm, LANES), lambda i: (i, (ql + kvl) // LANES)),
        pl.BlockSpec((1, kvl), lambda i: (0, 0)),
        pl.BlockSpec((tm, LANES), lambda i: (i, 0)),
    ]
    out_specs = [pl.BlockSpec((tm, kvl), lambda i: (i, 0)), pl.BlockSpec((tm, rope), lambda i: (i, 0))]
    out_shape = [jax.ShapeDtypeStruct((n, kvl), F32), jax.ShapeDtypeStruct((n, rope), F32)]
    args = [zs, zs, g_kv_a.reshape(1, kvl), tab]
    if not with_kv:
        return pl.pallas_call(
            functools.partial(_latent_kernel, rope=rope),
            grid=(n // tm,), in_specs=in_specs, out_specs=out_specs, out_shape=out_shape,
            compiler_params=_cparams(("parallel",)), name="latent",
        )(*args)
    in_specs += [pl.BlockSpec((kvl, heads * nope), lambda i: (0, 0)), pl.BlockSpec((heads * vh, kvl), lambda i: (0, 0))]
    out_specs += [pl.BlockSpec((heads, tm, 2 * nope), lambda i: (0, i, 0)), pl.BlockSpec((heads, vh + SUM_ROWS, tm), lambda i: (0, 0, i))]
    out_shape += [jax.ShapeDtypeStruct((heads, n, 2 * nope), BF16), jax.ShapeDtypeStruct((heads, vh + SUM_ROWS, n), BF16)]
    return pl.pallas_call(
        functools.partial(_kv_prep_kernel, heads=heads, nope=nope, rope=rope, vh=vh),
        grid=(n // tm,), in_specs=in_specs, out_specs=out_specs, out_shape=out_shape,
        compiler_params=_cparams(("parallel",)), name="kv_prep",
    )(*args, w_uk_f, w_uv_tf)


def _flash_kernel(qi_ref, ki_ref, q_ref, k_ref, vt_ref, gate_ref, o_ref, m_scr, acc_scr, s_scr, *, heads, vh):
    p = pl.program_id(1)
    qi = qi_ref[p]
    ki = ki_ref[p]

    @pl.when(ki == 0)
    def _():
        m_scr[...] = jnp.full(m_scr.shape, -jnp.inf, F32)
        acc_scr[...] = jnp.zeros(acc_scr.shape, F32)

    def scores(h, slot):
        s_scr[slot] = lax.dot_general(k_ref[h], q_ref[h], _NT, preferred_element_type=F32)

    def update(h, slot, masked):
        s = s_scr[slot]
        if masked:
            kpos = lax.broadcasted_iota(I32, s.shape, 0)
            qpos = lax.broadcasted_iota(I32, s.shape, 1)
            s = jnp.where(kpos <= qpos, s, -jnp.inf)
        m_prev = m_scr[h]
        m_new = jnp.maximum(m_prev, jnp.max(s, axis=0, keepdims=True))
        pr = jnp.exp2(s - m_new).astype(BF16)
        acc_scr[h] = jnp.exp2(m_prev - m_new) * acc_scr[h] + jnp.dot(vt_ref[h], pr, preferred_element_type=F32)
        m_scr[h] = m_new

    def run(masked):
        scores(0, 0)

        def pair(i, carry):
            h0 = 2 * i
            scores(h0 + 1, 1)
            update(h0, 0, masked)
            scores(jnp.minimum(h0 + 2, heads - 1), 0)
            update(h0 + 1, 1, masked)
            return carry

        lax.fori_loop(0, heads // 2, pair, 0, unroll=4 if heads % 8 == 0 else 1)

    @pl.when(ki < qi)
    def _():
        run(False)

    @pl.when(ki == qi)
    def _():
        run(True)
        for h in range(heads):
            acc = acc_scr[h]
            o = (acc[:vh] / acc[vh:vh + 1]).T
            g = jax.nn.sigmoid(gate_ref[:, h * vh:(h + 1) * vh].astype(F32))
            o_ref[:, h * vh:(h + 1) * vh] = (o * g).astype(o_ref.dtype)


def _flash_prompt(q, k, vt, zb, gate_off, batch, seq):
    heads, n, width = q.shape
    vr = vt.shape[1]
    vh = vr - SUM_ROWS
    d = heads * vh
    assert heads % 2 == 0
    tq = _tile(seq, 512, LANES)
    nq = seq // tq
    pairs = [(a, b) for a in range(nq) for b in range(a + 1)]
    qi = jnp.asarray([a for a, _ in pairs], I32)
    ki = jnp.asarray([b for _, b in pairs], I32)
    assert gate_off % d == 0
    grid_spec = pltpu.PrefetchScalarGridSpec(
        num_scalar_prefetch=2,
        grid=(batch, len(pairs)),
        in_specs=[
            pl.BlockSpec((heads, tq, width), lambda b, p, qi, ki: (0, b * nq + qi[p], 0)),
            pl.BlockSpec((heads, tq, width), lambda b, p, qi, ki: (0, b * nq + ki[p], 0)),
            pl.BlockSpec((heads, vr, tq), lambda b, p, qi, ki: (0, 0, b * nq + ki[p])),
            pl.BlockSpec((tq, d), lambda b, p, qi, ki: (b * nq + qi[p], gate_off // d)),
        ],
        out_specs=pl.BlockSpec((tq, d), lambda b, p, qi, ki: (b * nq + qi[p], 0)),
        scratch_shapes=[pltpu.VMEM((heads, 1, tq), F32), pltpu.VMEM((heads, vr, tq), F32), pltpu.VMEM((2, tq, tq), F32)],
    )
    return pl.pallas_call(
        functools.partial(_flash_kernel, heads=heads, vh=vh),
        grid_spec=grid_spec,
        out_shape=jax.ShapeDtypeStruct((n, d), BF16),
        compiler_params=_cparams(("parallel", "arbitrary")),
        name="flash_prompt",
    )(qi, ki, q, k, vt, zb)


def _absorb_kernel(q_ref, w_ref, o_ref, *, nope):
    o_ref[0] = lax.dot_general(q_ref[0, :, :nope], w_ref[0], _NT, preferred_element_type=F32)


def _absorb_q(qs, w_uk_h):
    heads, ns, width = qs.shape
    _, kvl, nope = w_uk_h.shape
    return pl.pallas_call(
        functools.partial(_absorb_kernel, nope=nope),
        grid=(heads,),
        in_specs=[pl.BlockSpec((1, ns, width), lambda h: (h, 0, 0)), pl.BlockSpec((1, kvl, nope), lambda h: (h, 0, 0))],
        out_specs=pl.BlockSpec((1, ns, kvl), lambda h: (h, 0, 0)),
        out_shape=jax.ShapeDtypeStruct((heads, ns, kvl), F32),
        compiler_params=_cparams(("parallel",)),
        name="absorb_q",
    )(qs, w_uk_h)


def _decode_attn_kernel(pt_ref, ql_ref, qr_ref, cn_ref, krn_ref, cc_hbm, ckr_hbm, o_ref, kc_buf, kr_buf, sem, *, chunk_pages, n_chunks, n_slots, page, scale):
    s = pl.program_id(0)
    n_seq = pl.num_programs(0)

    def copies(seq, chunk, slot):
        out = []
        for j in range(chunk_pages):
            pg = pt_ref[seq, chunk * chunk_pages + j]
            out.append(pltpu.make_async_copy(cc_hbm.at[pg], kc_buf.at[slot, pl.ds(j * page, page)], sem.at[slot, 0]))
            out.append(pltpu.make_async_copy(ckr_hbm.at[pg], kr_buf.at[slot, :, pl.ds(j * page, page)], sem.at[slot, 1]))
        return out

    def start(seq, chunk, slot):
        for c in copies(seq, chunk, slot):
            c.start()

    ahead = n_slots - 1

    @pl.when(s == 0)
    def _():
        for c in range(ahead):
            start(0, c, c)

    ql = ql_ref[0]
    qr = qr_ref[0]
    cn = cn_ref[0]
    krn = krn_ref[0]
    m = (jnp.sum(ql * cn, axis=1, keepdims=True) + jnp.sum(qr * krn, axis=1, keepdims=True)) * scale
    l = jnp.ones_like(m)
    acc = jnp.broadcast_to(cn, ql.shape)
    for c in range(n_chunks):
        slot = c % n_slots
        nxt = c + ahead
        if nxt < n_chunks:
            start(s, nxt, nxt % n_slots)
        else:
            @pl.when(s + 1 < n_seq)
            def _(nxt=nxt):
                start(s + 1, nxt - n_chunks, nxt % n_slots)
        for cp in copies(s, c, slot):
            cp.wait()
        kc = kc_buf[slot]
        kr_t = kr_buf[slot]
        sc = (lax.dot_general(ql, kc, _NT, preferred_element_type=F32)
              + jnp.dot(qr, kr_t, preferred_element_type=F32)) * scale
        m_new = jnp.maximum(m, jnp.max(sc, axis=1, keepdims=True))
        alpha = jnp.exp2(m - m_new)
        pr = jnp.exp2(sc - m_new)
        l = alpha * l + jnp.sum(pr, axis=1, keepdims=True)
        acc = alpha * acc + jnp.dot(pr, kc, preferred_element_type=F32)
        m = m_new
    o_ref[0] = acc / l


def _decode_attn(page_table, q_lat, q_rope, c_new, kr_new, cache_c, cache_kr_t, scale):
    ns, heads, kvl = q_lat.shape
    rope = q_rope.shape[-1]
    n_pages = page_table.shape[1]
    page = cache_c.shape[1]
    assert cache_kr_t.shape[1:] == (rope, page)
    chunk_pages = _tile(n_pages, 16, 1)
    n_chunks = n_pages // chunk_pages
    assert n_chunks % 2 == 0
    n_slots = 4 if n_chunks % 4 == 0 else 2
    keys = chunk_pages * page
    grid_spec = pltpu.PrefetchScalarGridSpec(
        num_scalar_prefetch=1,
        grid=(ns,),
        in_specs=[
            pl.BlockSpec((1, heads, kvl), lambda s, pt: (s, 0, 0)),
            pl.BlockSpec((1, heads, rope), lambda s, pt: (s, 0, 0)),
            pl.BlockSpec((1, 1, kvl), lambda s, pt: (s, 0, 0)),
            pl.BlockSpec((1, 1, rope), lambda s, pt: (s, 0, 0)),
            pl.BlockSpec(memory_space=pl.ANY),
            pl.BlockSpec(memory_space=pl.ANY),
        ],
        out_specs=pl.BlockSpec((1, heads, kvl), lambda s, pt: (s, 0, 0)),
        scratch_shapes=[
            pltpu.VMEM((n_slots, keys, kvl), cache_c.dtype),
            pltpu.VMEM((n_slots, rope, keys), cache_kr_t.dtype),
            pltpu.SemaphoreType.DMA((n_slots, 2)),
        ],
    )
    return pl.pallas_call(
        functools.partial(_decode_attn_kernel, chunk_pages=chunk_pages, n_chunks=n_chunks, n_slots=n_slots, page=page, scale=scale),
        grid_spec=grid_spec,
        out_shape=jax.ShapeDtypeStruct((ns, heads, kvl), F32),
        compiler_params=_cparams(("arbitrary",)),
        name="decode_attn",
    )(page_table, q_lat, q_rope, c_new.reshape(ns, 1, kvl), kr_new.reshape(ns, 1, rope), cache_c, cache_kr_t)


def _unabsorb_kernel(o_ref, w_ref, gate_ref, y_ref):
    y = jnp.dot(o_ref[0].astype(BF16), w_ref[0], preferred_element_type=F32)
    y_ref[...] = (y * jax.nn.sigmoid(gate_ref[...].astype(F32))).astype(y_ref.dtype)


def _unabsorb_o(o_lat_h, w_uv_h, zb, gate_off):
    heads, ns, kvl = o_lat_h.shape
    vh = w_uv_h.shape[-1]
    assert gate_off % vh == 0
    return pl.pallas_call(
        _unabsorb_kernel,
        grid=(heads,),
        in_specs=[
            pl.BlockSpec((1, ns, kvl), lambda h: (h, 0, 0)),
            pl.BlockSpec((1, kvl, vh), lambda h: (h, 0, 0)),
            pl.BlockSpec((ns, vh), lambda h: (0, gate_off // vh + h)),
        ],
        out_specs=pl.BlockSpec((ns, vh), lambda h: (0, h)),
        out_shape=jax.ShapeDtypeStruct((ns, heads * vh), F32),
        compiler_params=_cparams(("parallel",)),
        name="unabsorb_o",
    )(o_lat_h, w_uv_h, zb)


def _segment_mid(bc, s):
    c, dk = bc.shape
    if 2 * s >= 8:
        n = c // (2 * s)
        mid = bc.reshape(n, 2 * s, dk)[:, s - 1:s, :]
        return jnp.broadcast_to(mid, (n, 2 * s, dk)).reshape(c, dk)
    r = lax.broadcasted_iota(I32, bc.shape, 0) & (2 * s - 1)
    prev1 = pltpu.roll(bc, 1, 0)
    if s == 1:
        return jnp.where(r == 0, bc, prev1)
    assert s == 2
    return jnp.where(r == 0, pltpu.roll(bc, c - 1, 0), jnp.where(r == 1, bc, jnp.where(r == 2, prev1, pltpu.roll(bc, 2, 0))))


def _gla_prompt_kernel(q_ref, k_ref, v_ref, go_ref, gg_ref, lr_ref, wa_ref, ba_ref, gn_ref, y_ref, st_ref, st_scr, la_scr, *, n_chunks, c, scale):
    tb = pl.program_id(2)

    @pl.when(tb == 0)
    def _():
        st_scr[...] = jnp.zeros(st_scr.shape, F32)

    x = jnp.dot(lr_ref[...], wa_ref[...], preferred_element_type=F32, precision=HI) + ba_ref[...]
    la_scr[...] = _log_sigmoid(x) / GLA_TAU

    row = lax.broadcasted_iota(I32, (c, c), 0)
    col = lax.broadcasted_iota(I32, (c, c), 1)
    levels = []
    s = c // 2
    while s >= 1:
        levels.append(s)
        s //= 2
    tril = jnp.where(col <= row, 1.0, 0.0)
    masks = []
    for s in levels:
        sh = s.bit_length() - 1
        masks.append(((row >> (sh + 1)) == (col >> (sh + 1))) & (((row >> sh) & 1) == 1) & (((col >> sh) & 1) == 0))
    eye = row == col

    def chunk(ci, carry):
        r0 = pl.multiple_of(ci * c, c)
        q = q_ref[pl.ds(r0, c), :].astype(F32) * scale
        k = k_ref[pl.ds(r0, c), :].astype(F32)
        v = v_ref[pl.ds(r0, c), :]
        bc = jnp.dot(tril, la_scr[pl.ds(r0, c), :], preferred_element_type=F32, precision=HI)
        bl = bc[c - 1:c, :]
        att = jnp.where(eye, jnp.sum(q * k, axis=1, keepdims=True), 0.0)
        for li, s in enumerate(levels):
            ref = _segment_mid(bc, s)
            qh = (q * jnp.exp(jnp.minimum(bc - ref, 0.0))).astype(BF16)
            kh = (k * jnp.exp(jnp.minimum(ref - bc, 0.0))).astype(BF16)
            a = lax.dot_general(qh, kh, _NT, preferred_element_type=F32)
            att = att + jnp.where(masks[li], a, 0.0)
        st = st_scr[...]
        qt = (q * jnp.exp(bc)).astype(BF16)
        o = (lax.dot_general(qt, st.astype(BF16), _NT, preferred_element_type=F32)
             + jnp.dot(att.astype(BF16), v, preferred_element_type=F32))
        kb = (k * jnp.exp(bl - bc)).astype(BF16)
        st_scr[...] = st * jnp.exp(bl) + lax.dot_general(v, kb, _TN, preferred_element_type=F32)
        on = _rms(o, gn_ref[...])
        go = go_ref[pl.ds(r0, c), :].astype(F32)
        gg = gg_ref[pl.ds(r0, c), :].astype(F32)
        y_ref[pl.ds(r0, c), :] = (on * (go * jax.nn.sigmoid(go)) * jax.nn.sigmoid(gg)).astype(y_ref.dtype)
        return carry

    lax.fori_loop(0, n_chunks, chunk, 0, unroll=math.gcd(n_chunks, GLA_UNROLL))

    @pl.when(tb == pl.num_programs(2) - 1)
    def _():
        st_ref[0, 0] = st_scr[...].T


def _gla_prompt(zb, zs, offs, w_a2p, b_a2, g_gla, batch, seq, gh, dk, dv):
    n = batch * seq
    c = math.gcd(seq, GLA_CHUNK)
    tb = _tile(seq, 512, c)
    nb = seq // tb
    for key, w in (("gq", dk), ("gk", dk), ("gv", dv), ("g_out", dv), ("gate_gla", dv)):
        assert offs[key] % w == 0
    assert offs["g_lr"] % LANES == 0

    def rows(b, h, t):
        return b * nb + t

    in_specs = [
        pl.BlockSpec((tb, dk), lambda b, h, t: (rows(b, h, t), offs["gq"] // dk + h)),
        pl.BlockSpec((tb, dk), lambda b, h, t: (rows(b, h, t), offs["gk"] // dk + h)),
        pl.BlockSpec((tb, dv), lambda b, h, t: (rows(b, h, t), offs["gv"] // dv + h)),
        pl.BlockSpec((tb, dv), lambda b, h, t: (rows(b, h, t), offs["g_out"] // dv + h)),
        pl.BlockSpec((tb, dv), lambda b, h, t: (rows(b, h, t), offs["gate_gla"] // dv + h)),
        pl.BlockSpec((tb, LANES), lambda b, h, t: (rows(b, h, t), offs["g_lr"] // LANES)),
        pl.BlockSpec((LANES, dk), lambda b, h, t: (0, h)),
        pl.BlockSpec((1, dk), lambda b, h, t: (0, h)),
        pl.BlockSpec((1, dv), lambda b, h, t: (0, 0)),
    ]
    out_specs = [
        pl.BlockSpec((tb, dv), lambda b, h, t: (rows(b, h, t), h)),
        pl.BlockSpec((1, 1, dk, dv), lambda b, h, t: (b, h, 0, 0)),
    ]
    return pl.pallas_call(
        functools.partial(_gla_prompt_kernel, n_chunks=tb // c, c=c, scale=dk ** -0.5),
        grid=(batch, gh, nb),
        in_specs=in_specs,
        out_specs=out_specs,
        out_shape=[jax.ShapeDtypeStruct((n, gh * dv), BF16), jax.ShapeDtypeStruct((batch, gh, dk, dv), F32)],
        scratch_shapes=[pltpu.VMEM((dv, dk), F32), pltpu.VMEM((tb, dk), F32)],
        compiler_params=_cparams(("parallel", "parallel", "arbitrary")),
        name="gla_prompt",
    )(zb, zb, zb, zb, zb, zs, w_a2p, b_a2.reshape(1, gh * dk), g_gla.reshape(1, dv))


def _gla_decode_kernel(qk_ref, lrt_ref, wat_ref, bcol_ref, v_ref, go_ref, gg_ref, gn_ref, s0_ref, y_ref, s1_ref, *, sb, scale):
    x = jnp.dot(wat_ref[0], lrt_ref[0], preferred_element_type=F32, precision=HI) + bcol_ref[0]
    a = jnp.exp(_log_sigmoid(x) / GLA_TAU)
    outs = []
    for u in range(sb):
        qc = qk_ref[0, 0, :, u:u + 1] * scale
        kc = qk_ref[0, 0, :, sb + u:sb + u + 1]
        sn = a[:, u:u + 1] * s0_ref[u, 0] + kc * v_ref[u:u + 1, :]
        s1_ref[u, 0] = sn
        outs.append(jnp.sum(qc * sn, axis=0, keepdims=True))
    o = jnp.concatenate(outs, axis=0)
    go = go_ref[...]
    y_ref[...] = _rms(o, gn_ref[...]) * (go * jax.nn.sigmoid(go)) * jax.nn.sigmoid(gg_ref[...])


def _gla_decode(qk_cols, lr_t, wa_t, b_col, gv, g_out, gate_gla, g_gla, state):
    ns, gh, dk, dv = state.shape
    sb = qk_cols.shape[-1] // 2
    return pl.pallas_call(
        functools.partial(_gla_decode_kernel, sb=sb, scale=dk ** -0.5),
        grid=(ns // sb, gh),
        in_specs=[
            pl.BlockSpec((1, 1, dk, 2 * sb), lambda i, h: (i, h, 0, 0)),
            pl.BlockSpec((1, LANES, sb), lambda i, h: (i, 0, 0)),
            pl.BlockSpec((1, dk, LANES), lambda i, h: (h, 0, 0)),
            pl.BlockSpec((1, dk, 1), lambda i, h: (h, 0, 0)),
            pl.BlockSpec((sb, dv), lambda i, h: (i, h)),
            pl.BlockSpec((sb, dv), lambda i, h: (i, h)),
            pl.BlockSpec((sb, dv), lambda i, h: (i, h)),
            pl.BlockSpec((1, dv), lambda i, h: (0, 0)),
            pl.BlockSpec((sb, 1, dk, dv), lambda i, h: (i, h, 0, 0)),
        ],
        out_specs=[pl.BlockSpec((sb, dv), lambda i, h: (i, h)), pl.BlockSpec((sb, 1, dk, dv), lambda i, h: (i, h, 0, 0))],
        out_shape=[jax.ShapeDtypeStruct((ns, gh * dv), F32), jax.ShapeDtypeStruct((ns, gh, dk, dv), F32)],
        compiler_params=_cparams(("parallel", "parallel")),
        name="gla_decode",
    )(qk_cols, lr_t, wa_t, b_col, gv, g_out, gate_gla, g_gla.reshape(1, dv), state)


def _outproj_kernel(ym_ref, yg_ref, h_ref, w_ref, g_ref, wr_ref, br_ref, h1_ref, m_ref, route_ref):
    mix = (ym_ref[...].astype(F32) + yg_ref[...].astype(F32)).astype(BF16)
    h1 = h_ref[...] + jnp.dot(mix, w_ref[...], preferred_element_type=F32)
    h1_ref[...] = h1
    m = _rms(h1, g_ref[...])
    half = m.shape[1] // 2
    m_ref[...] = _pack_bf16_pair(m[:, :half], m[:, half:])
    logits = jnp.dot(m.astype(BF16), wr_ref[...].astype(BF16), preferred_element_type=F32) + br_ref[...]
    lane = lax.broadcasted_iota(I32, logits.shape, 1).astype(F32)
    vals, idxs = [], []
    for _ in range(TOP_K):
        mx = jnp.max(logits, axis=1, keepdims=True)
        ix = jnp.min(jnp.where(logits == mx, lane, float(LANES)), axis=1, keepdims=True)
        vals.append(mx)
        idxs.append(ix)
        logits = jnp.where(lane == ix, -jnp.inf, logits)
    ex = [jnp.exp(v - vals[0]) for v in vals]
    den = ex[0]
    for e in ex[1:]:
        den = den + e
    route = jnp.zeros(logits.shape, F32)
    for kk in range(TOP_K):
        route = jnp.where(lane == float(kk), idxs[kk], route)
        route = jnp.where(lane == float(TOP_K + kk), ex[kk] / den, route)
    route_ref[...] = route


def _outproj(y_mla, y_gla, h, w_out_b, g_ffn, w_router_p, b_router_p):
    n, d = h.shape
    tm = _tile(n, 512, 16)
    row = lambda i: (i, 0)
    fixed = lambda i: (0, 0)
    return pl.pallas_call(
        _outproj_kernel,
        grid=(n // tm,),
        in_specs=[
            pl.BlockSpec((tm, d), row), pl.BlockSpec((tm, d), row), pl.BlockSpec((tm, d), row),
            pl.BlockSpec((d, d), fixed, pipeline_mode=pl.Buffered(1)), pl.BlockSpec((1, d), fixed),
            pl.BlockSpec((d, LANES), fixed), pl.BlockSpec((1, LANES), fixed),
        ],
        out_specs=[pl.BlockSpec((tm, d), row), pl.BlockSpec((tm, d // 2), row), pl.BlockSpec((tm, LANES), row)],
        out_shape=[jax.ShapeDtypeStruct((n, d), F32), jax.ShapeDtypeStruct((n, d // 2), U32), jax.ShapeDtypeStruct((n, LANES), F32)],
        compiler_params=_cparams(("parallel",)),
        name="outproj",
    )(y_mla, y_gla, h, w_out_b, g_ffn.reshape(1, d), w_router_p, b_router_p)


def _dispatch_kernel(idx_ref, x_ref, o_ref, g_scr, *, rows):
    base = pl.program_id(0) * rows

    def body(r, carry):
        tok = idx_ref[base + r]
        g_scr[pl.ds(r, 1), :] = x_ref[pl.ds(tok, 1), :]
        return carry

    lax.fori_loop(0, rows, body, 0, unroll=8)
    hi, lo = _unpack_bf16_pair(g_scr[...])
    o_ref[...] = jnp.concatenate([hi, lo], axis=1).astype(BF16)


def _dispatch(buf_tok, m_packed):
    rows_total = buf_tok.shape[0]
    n, w = m_packed.shape
    rows = _tile(rows_total, 512, 16)
    grid_spec = pltpu.PrefetchScalarGridSpec(
        num_scalar_prefetch=1,
        grid=(rows_total // rows,),
        in_specs=[pl.BlockSpec(memory_space=pltpu.VMEM)],
        out_specs=pl.BlockSpec((rows, 2 * w), lambda i, idx: (i, 0)),
        scratch_shapes=[pltpu.VMEM((rows, w), U32)],
    )
    return pl.pallas_call(
        functools.partial(_dispatch_kernel, rows=rows),
        grid_spec=grid_spec,
        out_shape=jax.ShapeDtypeStruct((rows_total, 2 * w), BF16),
        compiler_params=_cparams(("arbitrary",), 56),
        name="moe_dispatch",
    )(buf_tok, m_packed)


def _for_row_chunks(b0, b1, rb, max_blocks, fn):
    n = b1 - b0
    size = 1 << (max_blocks.bit_length() - 1)
    while size >= 1:
        start = b0 + (n & ~(2 * size - 1))

        @pl.when((n & size) != 0)
        def _(start=start, size=size):
            fn(pl.multiple_of(start * rb, rb), size * rb)

        size //= 2


def _moe_kernel(ve_ref, blk_ref, nb_ref, zf_ref, xs_ref, wg_ref, wu_ref, bg_ref, bu_ref, wd_ref, bd_ref, ys_hbm,
                y_scr, o_scr, sem, *, n_ff, n_slab, dn, rb):
    v = pl.program_id(0)
    j = pl.program_id(1)
    blk0 = blk_ref[v]
    nb = nb_ref[v]
    b0, b1 = 0, nb
    fill = zf_ref[v] == 1
    work = jnp.logical_not(fill)
    max_blocks = xs_ref.shape[0] // rb
    d = y_scr.shape[1]
    slab = d // n_slab

    def out_copy(b):
        return pltpu.make_async_copy(o_scr.at[pl.ds(pl.multiple_of(b * rb, rb), rb)], ys_hbm.at[blk0 + b], sem.at[0])

    def flush():
        lax.fori_loop(0, nb, lambda b, c: (out_copy(b).start(), c)[1], 0)
        lax.fori_loop(0, nb, lambda b, c: (out_copy(b).wait(), c)[1], 0)

    @pl.when(fill & (j == 0))
    def _():
        def zero(b, carry):
            o_scr[pl.ds(pl.multiple_of(b * rb, rb), rb), :] = jnp.zeros((rb, o_scr.shape[1]), U32)
            return carry
        lax.fori_loop(0, nb, zero, 0)
        flush()

    @pl.when(work & (j == 0))
    def _():
        def init(r0, rows):
            y_scr[pl.ds(r0, rows), :] = jnp.broadcast_to(bd_ref[0], (rows, d))
        _for_row_chunks(b0, b1, rb, max_blocks, init)

    @pl.when(work)
    def _():
        def mlp(r0, rows):
            x = xs_ref[pl.ds(r0, rows), :]
            g = jnp.dot(x, wg_ref[0].astype(BF16), preferred_element_type=F32) + bg_ref[0]
            u = jnp.dot(x, wu_ref[0].astype(BF16), preferred_element_type=F32) + bu_ref[0]
            gate = jnp.minimum(g, SWIGLU_LIMIT)
            up = jnp.clip(u, -SWIGLU_LIMIT, SWIGLU_LIMIT)
            glu = gate * jax.nn.sigmoid(gate * SWIGLU_ALPHA)
            a = ((up + 1.0) * glu).astype(BF16)
            for c0 in range(0, d, dn):
                w = wd_ref[0, :, c0:c0 + dn].astype(BF16)
                y_scr[pl.ds(r0, rows), c0:c0 + dn] += jnp.dot(a, w, preferred_element_type=F32)
        _for_row_chunks(b0, b1, rb, max_blocks, mlp)

    @pl.when(work & (j == n_ff - 1))
    def _():
        def emit(r0, rows):
            for s in range(n_slab):
                y = y_scr[pl.ds(r0, rows), s * slab:(s + 1) * slab]
                o_scr[pl.ds(r0, rows), s * (slab // 2):(s + 1) * (slab // 2)] = _pack_bf16_pair(y[:, :slab // 2], y[:, slab // 2:])
        _for_row_chunks(b0, b1, rb, max_blocks, emit)
        flush()


def _moe_experts(meta, xs, w_gu, b_gu, w_down, b_down, rows_total, window_rows):
    v_exp, v_blk, v_nb, v_zero = meta
    d = xs.shape[1]
    e, _, f2 = w_gu.shape
    rb = EXPERT_ROW_BLOCK
    ff = f2 // 2
    tf = _tile(ff, 256, LANES)
    n_ff = ff // tf
    n_slab = d // _tile(d, MOE_SLAB, 2 * LANES)
    n_visits = v_exp.shape[0]
    ff_idx = lambda j, zf: jnp.where(zf == 1, n_ff - 1, j)
    grid_spec = pltpu.PrefetchScalarGridSpec(
        num_scalar_prefetch=4,
        grid=(n_visits, n_ff),
        in_specs=[
            pl.BlockSpec((pl.Element(window_rows), pl.Element(d)), lambda v, j, ve, blk, nb, zf: (blk[v] * rb, 0)),
            pl.BlockSpec((1, d, tf), lambda v, j, ve, blk, nb, zf: (ve[v], 0, ff_idx(j, zf[v]))),
            pl.BlockSpec((1, d, tf), lambda v, j, ve, blk, nb, zf: (ve[v], 0, n_ff + ff_idx(j, zf[v]))),
            pl.BlockSpec((1, 1, tf), lambda v, j, ve, blk, nb, zf: (ve[v], 0, ff_idx(j, zf[v]))),
            pl.BlockSpec((1, 1, tf), lambda v, j, ve, blk, nb, zf: (ve[v], 0, n_ff + ff_idx(j, zf[v]))),
            pl.BlockSpec((1, tf, d), lambda v, j, ve, blk, nb, zf: (ve[v], ff_idx(j, zf[v]), 0)),
            pl.BlockSpec((1, 1, d), lambda v, j, ve, blk, nb, zf: (ve[v], 0, 0)),
        ],
        out_specs=pl.BlockSpec(memory_space=pl.ANY),
        scratch_shapes=[pltpu.VMEM((window_rows, d), F32), pltpu.VMEM((window_rows, d // 2), U32), pltpu.SemaphoreType.DMA((1,))],
    )
    return pl.pallas_call(
        functools.partial(_moe_kernel, n_ff=n_ff, n_slab=n_slab, dn=_tile(d, 512, LANES), rb=rb),
        grid_spec=grid_spec,
        out_shape=jax.ShapeDtypeStruct((rows_total // rb, rb, d // 2), U32),
        compiler_params=_cparams(("arbitrary", "arbitrary"), 56),
        name="moe_experts",
    )(v_exp, v_blk, v_nb, v_zero, xs, w_gu, w_gu, b_gu.reshape(e, 1, f2), b_gu.reshape(e, 1, f2), w_down, b_down.reshape(e, 1, d))


def _combine_kernel(dest_ref, ys_ref, route_ref, o_ref, stage, *, tm):
    base = pl.program_id(1) * tm

    def body(t, carry):
        for kk in range(TOP_K):
            r = dest_ref[(base + t) * TOP_K + kk]
            stage[kk, pl.ds(t, 1), :] = ys_ref[pl.ds(r, 1), :]
        return carry

    lax.fori_loop(0, tm, body, 0, unroll=4)
    acc_hi = acc_lo = None
    for kk in range(TOP_K):
        g = route_ref[:, TOP_K + kk:TOP_K + kk + 1]
        hi, lo = _unpack_bf16_pair(stage[kk])
        acc_hi = g * hi if acc_hi is None else acc_hi + g * hi
        acc_lo = g * lo if acc_lo is None else acc_lo + g * lo
    o_ref[...] = jnp.concatenate([acc_hi, acc_lo], axis=1)


def _combine(dest, ys, route):
    rows_total, packed = ys.shape
    wh = MOE_SLAB // 2
    n_down = packed // wh
    n = route.shape[0]
    tm = _tile(n, 256, 8)
    grid_spec = pltpu.PrefetchScalarGridSpec(
        num_scalar_prefetch=1,
        grid=(n_down, n // tm),
        in_specs=[
            pl.BlockSpec((rows_total, wh), lambda c, i, dest: (0, c)),
            pl.BlockSpec((tm, LANES), lambda c, i, dest: (i, 0)),
        ],
        out_specs=pl.BlockSpec((tm, 2 * wh), lambda c, i, dest: (i, c)),
        scratch_shapes=[pltpu.VMEM((TOP_K, tm, wh), U32)],
    )
    return pl.pallas_call(
        functools.partial(_combine_kernel, tm=tm),
        grid_spec=grid_spec,
        out_shape=jax.ShapeDtypeStruct((n, n_down * 2 * wh), F32),
        compiler_params=_cparams(("arbitrary", "arbitrary"), 56),
        name="moe_combine",
    )(dest, ys, route)


def _dest_kernel(route_ref, starts_ref, o_ref, cnt_scr):
    @pl.when(pl.program_id(0) == 0)
    def _():
        cnt_scr[...] = jnp.zeros(cnt_scr.shape, F32)

    tb = route_ref.shape[0]
    lane = lax.broadcasted_iota(I32, (tb, LANES), 1).astype(F32)
    row = lax.broadcasted_iota(I32, (tb, tb), 0)
    col = lax.broadcasted_iota(I32, (tb, tb), 1)
    before = jnp.where(col < row, 1.0, 0.0).astype(BF16)
    base = starts_ref[...] + cnt_scr[...]
    out = jnp.zeros((tb, LANES), F32)
    for kk in range(TOP_K):
        hit = lane == route_ref[:, kk:kk + 1]
        seen = jnp.dot(before, jnp.where(hit, 1.0, 0.0).astype(BF16), preferred_element_type=F32)
        dest = jnp.sum(jnp.where(hit, seen + base, 0.0), axis=1, keepdims=True)
        out = jnp.where(lane == float(kk), dest, out)
        base = base + jnp.sum(jnp.where(hit, 1.0, 0.0), axis=0, keepdims=True)
    cnt_scr[...] = base - starts_ref[...]
    o_ref[...] = out


def _dest_rows(route, starts_row):
    n = route.shape[0]
    tb = _tile(n, 512, 8)
    return pl.pallas_call(
        _dest_kernel,
        grid=(n // tb,),
        in_specs=[pl.BlockSpec((tb, LANES), lambda i: (i, 0)), pl.BlockSpec((1, LANES), lambda i: (0, 0))],
        out_specs=pl.BlockSpec((tb, LANES), lambda i: (i, 0)),
        out_shape=jax.ShapeDtypeStruct((n, LANES), F32),
        scratch_shapes=[pltpu.VMEM((1, LANES), F32)],
        compiler_params=_cparams(("arbitrary",)),
        name="moe_dest",
    )(route, starts_row)


def _route_meta(route, n_experts, window_rows):
    n_tok = route.shape[0]
    n_asg = n_tok * TOP_K
    rb = EXPERT_ROW_BLOCK
    e_flat = route[:, :TOP_K].astype(I32).reshape(n_asg)
    tok_flat = (jnp.arange(n_asg, dtype=I32) // TOP_K).astype(I32)
    counts = jnp.sum(jax.nn.one_hot(e_flat, n_experts, dtype=I32), axis=0)
    padded = (counts + rb - 1) // rb * rb
    ends = jnp.cumsum(padded)
    starts = ends - padded
    starts_row = jnp.pad(starts.astype(F32), (0, LANES - n_experts)).reshape(1, LANES)
    dest = _dest_rows(route, starts_row)[:, :TOP_K].astype(I32).reshape(n_asg)
    rows_total = (-(-n_asg // rb) + n_experts) * rb
    rows_alloc = -(-(rows_total + window_rows) // 512) * 512
    buf_tok = jnp.zeros((rows_alloc,), I32).at[dest].set(tok_flat, unique_indices=True, mode="promise_in_bounds")
    mb = window_rows // rb
    n_blk = (padded // rb).astype(I32)
    vpe = -(-n_blk // mb)
    cv = jnp.cumsum(vpe).astype(I32)
    n_work = cv[-1]
    total_blk = (ends[-1] // rb).astype(I32)
    tail_blk = rows_total // rb - total_blk
    n_fill = -(-tail_blk // mb)
    n_visits = n_experts + rows_total // rb // mb + 1 + -(-n_experts // mb)
    pos = jnp.arange(n_visits, dtype=I32)
    is_work = pos < n_work
    e = jnp.minimum(jnp.searchsorted(cv, pos, side="right"), n_experts - 1).astype(I32)
    k = pos - (cv[e] - vpe[e])
    f = pos - n_work
    is_fill = (f >= 0) & (f < n_fill)
    blk = jnp.where(is_work, (starts[e] // rb).astype(I32) + k * mb, total_blk + jnp.clip(f, 0, n_fill) * mb)
    blk = jnp.minimum(blk, rows_total // rb)
    nb = jnp.where(is_work, jnp.clip(n_blk[e] - k * mb, 0, mb), jnp.where(is_fill, jnp.clip(tail_blk - f * mb, 0, mb), 0))
    v_exp = jnp.where(is_work, e, e[jnp.maximum(n_work - 1, 0)])
    v_zero = jnp.logical_not(is_work).astype(I32)
    return buf_tok, dest, (v_exp.astype(I32), blk.astype(I32), nb.astype(I32), v_zero), rows_total


def _ple_kernel(h1_ref, moe_ref, p_ref, wp_ref, wg_ref, gp_ref, gf_ref, y_ref, *, final):
    h2 = h1_ref[...] + moe_ref[...]
    u = _rms(h2, gp_ref[...]).astype(BF16)
    gate = jax.nn.sigmoid(jnp.dot(u, wg_ref[...], preferred_element_type=F32))
    pe = jnp.dot(p_ref[...].astype(BF16), wp_ref[...], preferred_element_type=F32)
    y = h2 + pe * gate
    y_ref[...] = _rms(y, gf_ref[...]) if final else y


def _ple_final(h1, moe, row_off, p, w_ple_b, w_ple_gate_b, g_ple, g_final, final):
    n, d = h1.shape
    pd = p.shape[1]
    tm = _tile(n, 512, 8)
    assert row_off % tm == 0
    row = lambda i: (i, 0)
    fixed = lambda i: (0, 0)
    return pl.pallas_call(
        functools.partial(_ple_kernel, final=final),
        grid=(n // tm,),
        in_specs=[
            pl.BlockSpec((tm, d), row), pl.BlockSpec((tm, d), lambda i: (row_off // tm + i, 0)), pl.BlockSpec((tm, pd), row),
            pl.BlockSpec((pd, d), fixed), pl.BlockSpec((d, d), fixed, pipeline_mode=pl.Buffered(1)),
            pl.BlockSpec((1, d), fixed), pl.BlockSpec((1, d), fixed),
        ],
        out_specs=pl.BlockSpec((tm, d), row),
        out_shape=jax.ShapeDtypeStruct((n, d), F32),
        compiler_params=_cparams(("parallel",)),
        name="ple_final",
    )(h1, moe, p, w_ple_b, w_ple_gate_b, g_ple.reshape(1, d), g_final.reshape(1, d))


def _rope_table(pos, rope):
    half = rope // 2
    inv_freq = jnp.power(ROPE_THETA, -jnp.arange(half, dtype=F32) / half)
    ang = pos.astype(F32)[:, None] * inv_freq[None, :]
    cos, sin = jnp.cos(ang), jnp.sin(ang)
    return jnp.concatenate([cos, cos, -sin, sin], axis=1)


def _swap_halves(w):
    half = w.shape[-1] // 2
    return jnp.concatenate([w[..., half:], w[..., :half]], axis=-1)


def kernel(x_prompt, x_sample, cache_kv_latent, cache_k_rope, state_gla, page_table, p_prompt, p_sample, g_mix, w_in, g_q_a, w_uq, g_kv_a, w_uk, w_uv, w_a2, b_a2, g_gla, w_out, g_ffn, w_router, b_router, w_gu, b_gu, w_down, b_down, g_ple, w_ple, w_ple_gate, g_final):
    depth = w_in.shape[0]
    batch, seq, d = x_prompt.shape
    ns, dec_seq, _ = x_sample.shape
    assert dec_seq == 1
    ql, kvl = g_q_a.shape[-1], g_kv_a.shape[-1]
    rope = cache_k_rope.shape[-1]
    heads, nope = w_uk.shape[2], w_uk.shape[3]
    vh = w_uv.shape[3]
    _, _, gh, dk, dv = state_gla.shape
    lr = w_a2.shape[1]
    n_experts = w_router.shape[-1]
    past = page_table.shape[1] * cache_kv_latent.shape[2]
    assert 2 * rope == LANES and nope == LANES and heads * vh == d and lr <= LANES and n_experts <= LANES
    scale = LOG2E / math.sqrt(nope + rope)
    n_p = batch * seq
    big_segs = (("gv", gh * dv), ("g_out", gh * dv), ("gate_mla", d), ("gate_gla", d), ("gq", gh * dk), ("gk", gh * dk))
    offs, o = {}, 0
    for key, w in big_segs:
        offs[key] = o
        o += w
    offs["g_lr"] = ql + kvl + 2 * rope
    in_splits = (ql, kvl, rope, gh * dk, gh * dk, gh * dv, gh * dv, lr, d, d)
    in_offsets = tuple(sum(in_splits[:i + 1]) for i in range(len(in_splits) - 1))

    tab_p = jnp.tile(_rope_table(jnp.arange(seq), rope), (batch, 1))
    tab_s = _rope_table(jnp.full((ns,), past, I32), rope)

    h_p = x_prompt.reshape(n_p, d)
    h_s = x_sample.reshape(ns, d)
    kv_p, kr_p, st_p, kv_s, kr_s, st_s = [], [], [], [], [], []
    for l in range(depth):
        seg = dict(zip(("c_q", "c_kv", "k_r", "gq", "gk", "gv", "g_out", "g_lr", "gate_mla", "gate_gla"), jnp.split(w_in[l], in_offsets, axis=1)))
        w_big = jnp.concatenate([seg[k] for k, _ in big_segs], axis=1).astype(BF16)
        w_small = jnp.concatenate([seg["c_q"], seg["c_kv"], seg["k_r"], _swap_halves(seg["k_r"]),
                                   jnp.pad(seg["g_lr"], ((0, 0), (0, LANES - lr)))], axis=1).astype(BF16)
        wq = w_uq[l].reshape(ql, heads, nope + rope)
        w_uq_p = jnp.concatenate([wq[..., :nope], wq[..., nope:], _swap_halves(wq[..., nope:])], axis=-1).transpose(1, 0, 2).astype(BF16)
        w_uk_h = w_uk[l].transpose(1, 0, 2).astype(BF16)
        w_uv_h = w_uv[l].transpose(1, 0, 2).astype(BF16)
        w_uk_f = w_uk[l].reshape(kvl, heads * nope).astype(BF16)
        w_uv_tf = w_uv[l].reshape(kvl, heads * vh).T.astype(BF16)
        w_a2p = jnp.pad(w_a2[l], ((0, LANES - lr), (0, 0)))
        w_out_b = w_out[l].astype(BF16)
        w_router_p = jnp.pad(w_router[l], ((0, 0), (0, LANES - n_experts)))
        b_router_p = jnp.pad(b_router[l], (0, LANES - n_experts), constant_values=NEG_BIG).reshape(1, LANES)
        w_ple_b = w_ple[l].astype(BF16)
        w_ple_gate_b = w_ple_gate[l].astype(BF16)

        def front(h, tab, z_dtype, q_scale):
            a = _rms_cast(h, g_mix[l])
            zb = _matmul(a, w_big, z_dtype, "inproj_big")
            zs = _matmul(a, w_small, F32, "inproj_small")
            q = _q_proj(zs, g_q_a[l], w_uq_p, tab, q_scale)
            return zb, zs, q

        zb, zs, q = front(h_p, tab_p, BF16, scale)
        c_kv, k_r, k_heads, v_t = _kv_prep(zs, g_kv_a[l], tab_p, w_uk_f, w_uv_tf, heads, ql, rope, True)
        y_mla = _flash_prompt(q, k_heads, v_t, zb, offs["gate_mla"], batch, seq)
        y_gla, s_new = _gla_prompt(zb, zs, offs, w_a2p, b_a2[l], g_gla[l], batch, seq, gh, dk, dv)
        h1_p, m_p, route_p = _outproj(y_mla, y_gla, h_p, w_out_b, g_ffn[l], w_router_p, b_router_p)
        kv_p.append(c_kv.reshape(batch, seq, kvl)); kr_p.append(k_r.reshape(batch, seq, rope)); st_p.append(s_new)

        zb, zs, q = front(h_s, tab_s, F32, 1.0)
        c_kv, k_r = _kv_prep(zs, g_kv_a[l], tab_s, w_uk_f, w_uv_tf, heads, ql, rope, False)
        q_lat = _absorb_q(q, w_uk_h).transpose(1, 0, 2)
        q_rope = q[:, :, nope:nope + rope].astype(F32).transpose(1, 0, 2)
        o_lat = _decode_attn(page_table, q_lat, q_rope, c_kv, k_r, cache_kv_latent[l], cache_k_rope[l].transpose(0, 2, 1), scale)
        y_mla = _unabsorb_o(o_lat.transpose(1, 0, 2), w_uv_h, zb, offs["gate_mla"])
        sb = 8
        zf = zb.astype(F32)
        cols = lambda key, w: zf[:, offs[key]:offs[key] + w]
        to_cols = lambda z: z.reshape(ns // sb, sb, gh, dk).transpose(0, 2, 3, 1)
        qk_cols = jnp.concatenate([to_cols(cols("gq", gh * dk)), to_cols(cols("gk", gh * dk))], axis=-1)
        lr_t = zs[:, offs["g_lr"]:offs["g_lr"] + LANES].reshape(ns // sb, sb, LANES).transpose(0, 2, 1)
        wa_t = w_a2p.reshape(LANES, gh, dk).transpose(1, 2, 0)
        b_col = b_a2[l].reshape(gh, dk, 1)
        y_gla, s_new = _gla_decode(qk_cols, lr_t, wa_t, b_col, cols("gv", gh * dv), cols("g_out", gh * dv),
                                   cols("gate_gla", gh * dv), g_gla[l], state_gla[l])
        h1_s, m_s, route_s = _outproj(y_mla, y_gla, h_s, w_out_b, g_ffn[l], w_router_p, b_router_p)
        kv_s.append(c_kv.reshape(ns, 1, kvl)); kr_s.append(k_r.reshape(ns, 1, rope)); st_s.append(s_new)

        m_all = jnp.concatenate([m_p, m_s], axis=0)
        route = jnp.concatenate([route_p, route_s], axis=0)
        buf_tok, dest, meta, rows_total = _route_meta(route, n_experts, MOE_WINDOW_ROWS)
        xs = _dispatch(buf_tok, m_all)
        ys = _moe_experts(meta, xs, w_gu[l], b_gu[l], w_down[l], b_down[l], rows_total, MOE_WINDOW_ROWS)
        ys = ys.reshape(rows_total, d // 2)
        moe = _combine(dest, ys, route)

        last = l == depth - 1
        h_p = _ple_final(h1_p, moe, 0, p_prompt[l].reshape(n_p, -1), w_ple_b, w_ple_gate_b, g_ple[l], g_final, last)
        h_s = _ple_final(h1_s, moe, n_p, p_sample[l].reshape(ns, -1), w_ple_b, w_ple_gate_b, g_ple[l], g_final, last)

    return (h_p.reshape(batch, seq, d), h_s.reshape(ns, 1, d), jnp.stack(kv_p), jnp.stack(kr_p), jnp.stack(st_p),
            jnp.stack(kv_s), jnp.stack(kr_s), jnp.stack(st_s))
```

```python
import functools
import math

import jax
import jax.numpy as jnp
from jax import lax
from jax.experimental import pallas as pl
from jax.experimental.pallas import tpu as pltpu

F32 = jnp.float32
BF16 = jnp.bfloat16
U32 = jnp.uint32
I32 = jnp.int32

EPS = 1e-6
ROPE_THETA = 10000.0
GLA_TAU = 16.0
GLA_CHUNK = 64
GLA_UNROLL = 4
TOP_K = 4
SWIGLU_LIMIT = 7.0
SWIGLU_ALPHA = 1.702
EXPERT_ROW_BLOCK = 128
MOE_SLAB = 256
MOE_WINDOW_ROWS = 1536
LANES = 128
MIB = 1024 * 1024
SUM_ROWS = 16
LOG2E = 1.4426950408889634
HI = lax.Precision.HIGHEST
NEG_BIG = -1e30

_NT = (((1,), (1,)), ((), ()))
_TN = (((0,), (0,)), ((), ()))


def _cparams(sem, vmem_mib=48):
    return pltpu.CompilerParams(dimension_semantics=sem, vmem_limit_bytes=vmem_mib * MIB)


def _tile(n, cap, mult):
    best = None
    for t in range(mult, min(n, cap) + 1, mult):
        if n % t == 0:
            best = t
    assert best is not None, (n, cap, mult)
    return best


def _rms(x, g):
    return x * lax.rsqrt(jnp.mean(x * x, axis=-1, keepdims=True) + EPS) * g


def _log_sigmoid(x):
    return jnp.minimum(x, 0.0) - jnp.log1p(jnp.exp(-jnp.abs(x)))


def _pack_bf16_pair(hi, lo):
    hb = pltpu.bitcast(hi.astype(BF16).astype(F32), U32)
    lb = pltpu.bitcast(lo.astype(BF16).astype(F32), U32)
    return hb | (lb >> 16)


def _unpack_bf16_pair(w):
    hi = pltpu.bitcast(w & jnp.uint32(0xFFFF0000), F32)
    lo = pltpu.bitcast(w << 16, F32)
    return hi, lo


def _rms_cast_kernel(x_ref, g_ref, o_ref):
    o_ref[...] = _rms(x_ref[...], g_ref[...]).astype(o_ref.dtype)


def _rms_cast(x, g):
    n, d = x.shape
    tm = _tile(n, 512, 16)
    return pl.pallas_call(
        _rms_cast_kernel,
        grid=(n // tm,),
        in_specs=[pl.BlockSpec((tm, d), lambda i: (i, 0)), pl.BlockSpec((1, d), lambda i: (0, 0))],
        out_specs=pl.BlockSpec((tm, d), lambda i: (i, 0)),
        out_shape=jax.ShapeDtypeStruct((n, d), BF16),
        compiler_params=_cparams(("parallel",)),
        name="rms_cast",
    )(x, g.reshape(1, d))


def _mm_kernel(x_ref, w_ref, o_ref):
    o_ref[...] = jnp.dot(x_ref[...], w_ref[...], preferred_element_type=F32).astype(o_ref.dtype)


def _matmul(x, w, out_dtype, name):
    m, k = x.shape
    n = w.shape[1]
    tm = _tile(m, 1024, 16)
    tn = _tile(n, 1280, LANES)
    return pl.pallas_call(
        _mm_kernel,
        grid=(n // tn, m // tm),
        in_specs=[pl.BlockSpec((tm, k), lambda j, i: (i, 0)), pl.BlockSpec((k, tn), lambda j, i: (0, j))],
        out_specs=pl.BlockSpec((tm, tn), lambda j, i: (i, j)),
        out_shape=jax.ShapeDtypeStruct((m, n), out_dtype),
        compiler_params=_cparams(("parallel", "parallel")),
        name=name,
    )(x, w)


def _rope_combine(y2, tab):
    half = y2.shape[1] // 2
    t = y2 * tab
    r = t + pltpu.roll(t, half, axis=1)
    lane = lax.broadcasted_iota(I32, r.shape, 1)
    return jnp.where(lane < half, r, 0.0)


def _q_proj_kernel(cq_ref, g_ref, w_ref, tab_ref, q_ref, *, heads, nope, scale):
    cqn = _rms(cq_ref[...], g_ref[...]).astype(BF16)
    tab = tab_ref[...]
    for h in range(heads):
        y = jnp.dot(cqn, w_ref[h], preferred_element_type=F32)
        q_ref[h, :, :nope] = (y[:, :nope] * scale).astype(BF16)
        q_ref[h, :, nope:] = (_rope_combine(y[:, nope:], tab) * scale).astype(BF16)


def _q_proj(zs, g_q_a, w_uq_p, tab, scale):
    n = zs.shape[0]
    heads, ql, width = w_uq_p.shape
    tm = _tile(n, 512, 16)
    return pl.pallas_call(
        functools.partial(_q_proj_kernel, heads=heads, nope=width // 2, scale=scale),
        grid=(n // tm,),
        in_specs=[
            pl.BlockSpec((tm, ql), lambda i: (i, 0)),
            pl.BlockSpec((1, ql), lambda i: (0, 0)),
            pl.BlockSpec((heads, ql, width), lambda i: (0, 0, 0)),
            pl.BlockSpec((tm, LANES), lambda i: (i, 0)),
        ],
        out_specs=pl.BlockSpec((heads, tm, width), lambda i: (0, i, 0)),
        out_shape=jax.ShapeDtypeStruct((heads, n, width), BF16),
        compiler_params=_cparams(("parallel",)),
        name="q_proj",
    )(zs, g_q_a.reshape(1, ql), w_uq_p, tab)


def _kv_prep_kernel(ckv_ref, krs_ref, g_ref, tab_ref, wuk_ref, wuvt_ref, c_ref, kr_ref, k_ref, vt_ref, *, heads, nope, rope, vh):
    c = _rms(ckv_ref[...], g_ref[...])
    c_ref[...] = c
    cb = c.astype(BF16)
    r = _rope_combine(krs_ref[...], tab_ref[...])
    kr_ref[...] = r[:, :rope]
    rb = r.astype(BF16)
    kn = jnp.dot(cb, wuk_ref[...], preferred_element_type=F32).astype(BF16)
    vt = lax.dot_general(wuvt_ref[...], cb, _NT, preferred_element_type=F32).astype(BF16)
    for h in range(heads):
        k_ref[h, :, :nope] = kn[:, h * nope:(h + 1) * nope]
        k_ref[h, :, nope:] = rb
        vt_ref[h, :vh, :] = vt[h * vh:(h + 1) * vh, :]
        vt_ref[h, vh:, :] = jnp.ones((SUM_ROWS, cb.shape[0]), BF16)


def _latent_kernel(ckv_ref, krs_ref, g_ref, tab_ref, c_ref, kr_ref, *, rope):
    c_ref[...] = _rms(ckv_ref[...], g_ref[...])
    kr_ref[...] = _rope_combine(krs_ref[...], tab_ref[...])[:, :rope]


def _kv_prep(zs, g_kv_a, tab, w_uk_f, w_uv_tf, heads, ql, rope, with_kv):
    n = zs.shape[0]
    kvl = w_uk_f.shape[0]
    nope = w_uk_f.shape[1] // heads
    vh = w_uv_tf.shape[0] // heads
    assert ql % kvl == 0 and (ql + kvl) % LANES == 0
    tm = _tile(n, 640, LANES if with_kv else 8)
    in_specs = [
        pl.BlockSpec((tm, kvl), lambda i: (i, ql // kvl)),
        pl.BlockSpec((tm, LANES), lambda i: (i, (ql + kvl) // LANES)),
        pl.BlockSpec((1, kvl), lambda i: (0, 0)),
        pl.BlockSpec((tm, LANES), lambda i: (i, 0)),
    ]
    out_specs = [pl.BlockSpec((tm, kvl), lambda i: (i, 0)), pl.BlockSpec((tm, rope), lambda i: (i, 0))]
    out_shape = [jax.ShapeDtypeStruct((n, kvl), F32), jax.ShapeDtypeStruct((n, rope), F32)]
    args = [zs, zs, g_kv_a.reshape(1, kvl), tab]
    if not with_kv:
        return pl.pallas_call(
            functools.partial(_latent_kernel, rope=rope),
            grid=(n // tm,), in_specs=in_specs, out_specs=out_specs, out_shape=out_shape,
            compiler_params=_cparams(("parallel",)), name="latent",
        )(*args)
    in_specs += [pl.BlockSpec((kvl, heads * nope), lambda i: (0, 0)), pl.BlockSpec((heads * vh, kvl), lambda i: (0, 0))]
    out_specs += [pl.BlockSpec((heads, tm, 2 * nope), lambda i: (0, i, 0)), pl.BlockSpec((heads, vh + SUM_ROWS, tm), lambda i: (0, 0, i))]
    out_shape += [jax.ShapeDtypeStruct((heads, n, 2 * nope), BF16), jax.ShapeDtypeStruct((heads, vh + SUM_ROWS, n), BF16)]
    return pl.pallas_call(
        functools.partial(_kv_prep_kernel, heads=heads, nope=nope, rope=rope, vh=vh),
        grid=(n // tm,), in_specs=in_specs, out_specs=out_specs, out_shape=out_shape,
        compiler_params=_cparams(("parallel",)), name="kv_prep",
    )(*args, w_uk_f, w_uv_tf)


def _flash_kernel(qi_ref, ki_ref, q_ref, k_ref, vt_ref, gate_ref, o_ref, m_scr, acc_scr, s_scr, *, heads, vh):
    p = pl.program_id(1)
    qi = qi_ref[p]
    ki = ki_ref[p]

    @pl.when(ki == 0)
    def _():
        m_scr[...] = jnp.full(m_scr.shape, -jnp.inf, F32)
        acc_scr[...] = jnp.zeros(acc_scr.shape, F32)

    def scores(h, slot):
        s_scr[slot] = lax.dot_general(k_ref[h], q_ref[h], _NT, preferred_element_type=F32)

    def update(h, slot, masked):
        s = s_scr[slot]
        if masked:
            kpos = lax.broadcasted_iota(I32, s.shape, 0)
            qpos = lax.broadcasted_iota(I32, s.shape, 1)
            s = jnp.where(kpos <= qpos, s, -jnp.inf)
        m_prev = m_scr[h]
        m_new = jnp.maximum(m_prev, jnp.max(s, axis=0, keepdims=True))
        pr = jnp.exp2(s - m_new).astype(BF16)
        acc_scr[h] = jnp.exp2(m_prev - m_new) * acc_scr[h] + jnp.dot(vt_ref[h], pr, preferred_element_type=F32)
        m_scr[h] = m_new

    def run(masked):
        scores(0, 0)

        def pair(i, carry):
            h0 = 2 * i
            scores(h0 + 1, 1)
            update(h0, 0, masked)
            scores(jnp.minimum(h0 + 2, heads - 1), 0)
            update(h0 + 1, 1, masked)
            return carry

        lax.fori_loop(0, heads // 2, pair, 0, unroll=4 if heads % 8 == 0 else 1)

    @pl.when(ki < qi)
    def _():
        run(False)

    @pl.when(ki == qi)
    def _():
        run(True)
        for h in range(heads):
            acc = acc_scr[h]
            o = (acc[:vh] / acc[vh:vh + 1]).T
            g = jax.nn.sigmoid(gate_ref[:, h * vh:(h + 1) * vh].astype(F32))
            o_ref[:, h * vh:(h + 1) * vh] = (o * g).astype(o_ref.dtype)


def _flash_prompt(q, k, vt, zb, gate_off, batch, seq):
    heads, n, width = q.shape
    vr = vt.shape[1]
    vh = vr - SUM_ROWS
    d = heads * vh
    assert heads % 2 == 0
    tq = _tile(seq, 512, LANES)
    nq = seq // tq
    pairs = [(a, b) for a in range(nq) for b in range(a + 1)]
    qi = jnp.asarray([a for a, _ in pairs], I32)
    ki = jnp.asarray([b for _, b in pairs], I32)
    assert gate_off % d == 0
    grid_spec = pltpu.PrefetchScalarGridSpec(
        num_scalar_prefetch=2,
        grid=(batch, len(pairs)),
        in_specs=[
            pl.BlockSpec((heads, tq, width), lambda b, p, qi, ki: (0, b * nq + qi[p], 0)),
            pl.BlockSpec((heads, tq, width), lambda b, p, qi, ki: (0, b * nq + ki[p], 0)),
            pl.BlockSpec((heads, vr, tq), lambda b, p, qi, ki: (0, 0, b * nq + ki[p])),
            pl.BlockSpec((tq, d), lambda b, p, qi, ki: (b * nq + qi[p], gate_off // d)),
        ],
        out_specs=pl.BlockSpec((tq, d), lambda b, p, qi, ki: (b * nq + qi[p], 0)),
        scratch_shapes=[pltpu.VMEM((heads, 1, tq), F32), pltpu.VMEM((heads, vr, tq), F32), pltpu.VMEM((2, tq, tq), F32)],
    )
    return pl.pallas_call(
        functools.partial(_flash_kernel, heads=heads, vh=vh),
        grid_spec=grid_spec,
        out_shape=jax.ShapeDtypeStruct((n, d), BF16),
        compiler_params=_cparams(("parallel", "arbitrary")),
        name="flash_prompt",
    )(qi, ki, q, k, vt, zb)


def _absorb_kernel(q_ref, w_ref, o_ref, *, nope):
    o_ref[0] = lax.dot_general(q_ref[0, :, :nope], w_ref[0], _NT, preferred_element_type=F32)


def _absorb_q(qs, w_uk_h):
    heads, ns, width = qs.shape
    _, kvl, nope = w_uk_h.shape
    return pl.pallas_call(
        functools.partial(_absorb_kernel, nope=nope),
        grid=(heads,),
        in_specs=[pl.BlockSpec((1, ns, width), lambda h: (h, 0, 0)), pl.BlockSpec((1, kvl, nope), lambda h: (h, 0, 0))],
        out_specs=pl.BlockSpec((1, ns, kvl), lambda h: (h, 0, 0)),
        out_shape=jax.ShapeDtypeStruct((heads, ns, kvl), F32),
        compiler_params=_cparams(("parallel",)),
        name="absorb_q",
    )(qs, w_uk_h)


def _decode_attn_kernel(pt_ref, ql_ref, qr_ref, cn_ref, krn_ref, cc_hbm, ckr_hbm, o_ref, kc_buf, kr_buf, sem, *, chunk_pages, n_chunks, n_slots, page, scale):
    s = pl.program_id(0)
    n_seq = pl.num_programs(0)

    def copies(seq, chunk, slot):
        out = []
        for j in range(chunk_pages):
            pg = pt_ref[seq, chunk * chunk_pages + j]
            out.append(pltpu.make_async_copy(cc_hbm.at[pg], kc_buf.at[slot, pl.ds(j * page, page)], sem.at[slot, 0]))
            out.append(pltpu.make_async_copy(ckr_hbm.at[pg], kr_buf.at[slot, :, pl.ds(j * page, page)], sem.at[slot, 1]))
        return out

    def start(seq, chunk, slot):
        for c in copies(seq, chunk, slot):
            c.start()

    ahead = n_slots - 1

    @pl.when(s == 0)
    def _():
        for c in range(ahead):
            start(0, c, c)

    ql = ql_ref[0]
    qr = qr_ref[0]
    cn = cn_ref[0]
    krn = krn_ref[0]
    m = (jnp.sum(ql * cn, axis=1, keepdims=True) + jnp.sum(qr * krn, axis=1, keepdims=True)) * scale
    l = jnp.ones_like(m)
    acc = jnp.broadcast_to(cn, ql.shape)
    for c in range(n_chunks):
        slot = c % n_slots
        nxt = c + ahead
        if nxt < n_chunks:
            start(s, nxt, nxt % n_slots)
        else:
            @pl.when(s + 1 < n_seq)
            def _(nxt=nxt):
                start(s + 1, nxt - n_chunks, nxt % n_slots)
        for cp in copies(s, c, slot):
            cp.wait()
        kc = kc_buf[slot]
        kr_t = kr_buf[slot]
        sc = (lax.dot_general(ql, kc, _NT, preferred_element_type=F32)
              + jnp.dot(qr, kr_t, preferred_element_type=F32)) * scale
        m_new = jnp.maximum(m, jnp.max(sc, axis=1, keepdims=True))
        alpha = jnp.exp2(m - m_new)
        pr = jnp.exp2(sc - m_new)
        l = alpha * l + jnp.sum(pr, axis=1, keepdims=True)
        acc = alpha * acc + jnp.dot(pr, kc, preferred_element_type=F32)
        m = m_new
    o_ref[0] = acc / l


def _decode_attn(page_table, q_lat, q_rope, c_new, kr_new, cache_c, cache_kr_t, scale):
    ns, heads, kvl = q_lat.shape
    rope = q_rope.shape[-1]
    n_pages = page_table.shape[1]
    page = cache_c.shape[1]
    assert cache_kr_t.shape[1:] == (rope, page)
    chunk_pages = _tile(n_pages, 16, 1)
    n_chunks = n_pages // chunk_pages
    assert n_chunks % 2 == 0
    n_slots = 4 if n_chunks % 4 == 0 else 2
    keys = chunk_pages * page
    grid_spec = pltpu.PrefetchScalarGridSpec(
        num_scalar_prefetch=1,
        grid=(ns,),
        in_specs=[
            pl.BlockSpec((1, heads, kvl), lambda s, pt: (s, 0, 0)),
            pl.BlockSpec((1, heads, rope), lambda s, pt: (s, 0, 0)),
            pl.BlockSpec((1, 1, kvl), lambda s, pt: (s, 0, 0)),
            pl.BlockSpec((1, 1, rope), lambda s, pt: (s, 0, 0)),
            pl.BlockSpec(memory_space=pl.ANY),
            pl.BlockSpec(memory_space=pl.ANY),
        ],
        out_specs=pl.BlockSpec((1, heads, kvl), lambda s, pt: (s, 0, 0)),
        scratch_shapes=[
            pltpu.VMEM((n_slots, keys, kvl), cache_c.dtype),
            pltpu.VMEM((n_slots, rope, keys), cache_kr_t.dtype),
            pltpu.SemaphoreType.DMA((n_slots, 2)),
        ],
    )
    return pl.pallas_call(
        functools.partial(_decode_attn_kernel, chunk_pages=chunk_pages, n_chunks=n_chunks, n_slots=n_slots, page=page, scale=scale),
        grid_spec=grid_spec,
        out_shape=jax.ShapeDtypeStruct((ns, heads, kvl), F32),
        compiler_params=_cparams(("arbitrary",)),
        name="decode_attn",
    )(page_table, q_lat, q_rope, c_new.reshape(ns, 1, kvl), kr_new.reshape(ns, 1, rope), cache_c, cache_kr_t)


def _unabsorb_kernel(o_ref, w_ref, gate_ref, y_ref):
    y = jnp.dot(o_ref[0].astype(BF16), w_ref[0], preferred_element_type=F32)
    y_ref[...] = (y * jax.nn.sigmoid(gate_ref[...].astype(F32))).astype(y_ref.dtype)


def _unabsorb_o(o_lat_h, w_uv_h, zb, gate_off):
    heads, ns, kvl = o_lat_h.shape
    vh = w_uv_h.shape[-1]
    assert gate_off % vh == 0
    return pl.pallas_call(
        _unabsorb_kernel,
        grid=(heads,),
        in_specs=[
            pl.BlockSpec((1, ns, kvl), lambda h: (h, 0, 0)),
            pl.BlockSpec((1, kvl, vh), lambda h: (h, 0, 0)),
            pl.BlockSpec((ns, vh), lambda h: (0, gate_off // vh + h)),
        ],
        out_specs=pl.BlockSpec((ns, vh), lambda h: (0, h)),
        out_shape=jax.ShapeDtypeStruct((ns, heads * vh), F32),
        compiler_params=_cparams(("parallel",)),
        name="unabsorb_o",
    )(o_lat_h, w_uv_h, zb)


def _segment_mid(bc, s):
    c, dk = bc.shape
    if 2 * s >= 8:
        n = c // (2 * s)
        mid = bc.reshape(n, 2 * s, dk)[:, s - 1:s, :]
        return jnp.broadcast_to(mid, (n, 2 * s, dk)).reshape(c, dk)
    r = lax.broadcasted_iota(I32, bc.shape, 0) & (2 * s - 1)
    prev1 = pltpu.roll(bc, 1, 0)
    if s == 1:
        return jnp.where(r == 0, bc, prev1)
    assert s == 2
    return jnp.where(r == 0, pltpu.roll(bc, c - 1, 0), jnp.where(r == 1, bc, jnp.where(r == 2, prev1, pltpu.roll(bc, 2, 0))))


def _gla_prompt_kernel(q_ref, k_ref, v_ref, go_ref, gg_ref, lr_ref, wa_ref, ba_ref, gn_ref, y_ref, st_ref, st_scr, la_scr, *, n_chunks, c, scale):
    tb = pl.program_id(2)

    @pl.when(tb == 0)
    def _():
        st_scr[...] = jnp.zeros(st_scr.shape, F32)

    x = jnp.dot(lr_ref[...], wa_ref[...], preferred_element_type=F32, precision=HI) + ba_ref[...]
    la_scr[...] = _log_sigmoid(x) / GLA_TAU

    row = lax.broadcasted_iota(I32, (c, c), 0)
    col = lax.broadcasted_iota(I32, (c, c), 1)
    levels = []
    s = c // 2
    while s >= 1:
        levels.append(s)
        s //= 2
    tril = jnp.where(col <= row, 1.0, 0.0)
    masks = []
    for s in levels:
        sh = s.bit_length() - 1
        masks.append(((row >> (sh + 1)) == (col >> (sh + 1))) & (((row >> sh) & 1) == 1) & (((col >> sh) & 1) == 0))
    eye = row == col

    def chunk(ci, carry):
        r0 = pl.multiple_of(ci * c, c)
        q = q_ref[pl.ds(r0, c), :].astype(F32) * scale
        k = k_ref[pl.ds(r0, c), :].astype(F32)
        v = v_ref[pl.ds(r0, c), :]
        bc = jnp.dot(tril, la_scr[pl.ds(r0, c), :], preferred_element_type=F32, precision=HI)
        bl = bc[c - 1:c, :]
        att = jnp.where(eye, jnp.sum(q * k, axis=1, keepdims=True), 0.0)
        for li, s in enumerate(levels):
            w = jnp.exp(-jnp.abs(bc - _segment_mid(bc, s)))
            qh = (q * w).astype(BF16)
            kh = (k * w).astype(BF16)
            a = lax.dot_general(qh, kh, _NT, preferred_element_type=F32)
            att = att + jnp.where(masks[li], a, 0.0)
        st = st_scr[...]
        qt = (q * jnp.exp(bc)).astype(BF16)
        o = (lax.dot_general(qt, st.astype(BF16), _NT, preferred_element_type=F32)
             + jnp.dot(att.astype(BF16), v, preferred_element_type=F32))
        kb = (k * jnp.exp(bl - bc)).astype(BF16)
        st_scr[...] = st * jnp.exp(bl) + lax.dot_general(v, kb, _TN, preferred_element_type=F32)
        on = _rms(o, gn_ref[...])
        go = go_ref[pl.ds(r0, c), :].astype(F32)
        gg = gg_ref[pl.ds(r0, c), :].astype(F32)
        y_ref[pl.ds(r0, c), :] = (on * (go * jax.nn.sigmoid(go)) * jax.nn.sigmoid(gg)).astype(y_ref.dtype)
        return carry

    lax.fori_loop(0, n_chunks, chunk, 0, unroll=math.gcd(n_chunks, GLA_UNROLL))

    @pl.when(tb == pl.num_programs(2) - 1)
    def _():
        st_ref[0, 0] = st_scr[...].T


def _gla_prompt(zb, zs, offs, w_a2p, b_a2, g_gla, batch, seq, gh, dk, dv):
    n = batch * seq
    c = math.gcd(seq, GLA_CHUNK)
    tb = _tile(seq, 512, c)
    nb = seq // tb
    for key, w in (("gq", dk), ("gk", dk), ("gv", dv), ("g_out", dv), ("gate_gla", dv)):
        assert offs[key] % w == 0
    assert offs["g_lr"] % LANES == 0

    def rows(b, h, t):
        return b * nb + t

    in_specs = [
        pl.BlockSpec((tb, dk), lambda b, h, t: (rows(b, h, t), offs["gq"] // dk + h)),
        pl.BlockSpec((tb, dk), lambda b, h, t: (rows(b, h, t), offs["gk"] // dk + h)),
        pl.BlockSpec((tb, dv), lambda b, h, t: (rows(b, h, t), offs["gv"] // dv + h)),
        pl.BlockSpec((tb, dv), lambda b, h, t: (rows(b, h, t), offs["g_out"] // dv + h)),
        pl.BlockSpec((tb, dv), lambda b, h, t: (rows(b, h, t), offs["gate_gla"] // dv + h)),
        pl.BlockSpec((tb, LANES), lambda b, h, t: (rows(b, h, t), offs["g_lr"] // LANES)),
        pl.BlockSpec((LANES, dk), lambda b, h, t: (0, h)),
        pl.BlockSpec((1, dk), lambda b, h, t: (0, h)),
        pl.BlockSpec((1, dv), lambda b, h, t: (0, 0)),
    ]
    out_specs = [
        pl.BlockSpec((tb, dv), lambda b, h, t: (rows(b, h, t), h)),
        pl.BlockSpec((1, 1, dk, dv), lambda b, h, t: (b, h, 0, 0)),
    ]
    return pl.pallas_call(
        functools.partial(_gla_prompt_kernel, n_chunks=tb // c, c=c, scale=dk ** -0.5),
        grid=(batch, gh, nb),
        in_specs=in_specs,
        out_specs=out_specs,
        out_shape=[jax.ShapeDtypeStruct((n, gh * dv), BF16), jax.ShapeDtypeStruct((batch, gh, dk, dv), F32)],
        scratch_shapes=[pltpu.VMEM((dv, dk), F32), pltpu.VMEM((tb, dk), F32)],
        compiler_params=_cparams(("parallel", "parallel", "arbitrary")),
        name="gla_prompt",
    )(zb, zb, zb, zb, zb, zs, w_a2p, b_a2.reshape(1, gh * dk), g_gla.reshape(1, dv))


def _gla_decode_kernel(qk_ref, lrt_ref, wat_ref, bcol_ref, v_ref, go_ref, gg_ref, gn_ref, s0_ref, y_ref, s1_ref, *, sb, scale):
    x = jnp.dot(wat_ref[0], lrt_ref[0], preferred_element_type=F32, precision=HI) + bcol_ref[0]
    a = jnp.exp(_log_sigmoid(x) / GLA_TAU)
    outs = []
    for u in range(sb):
        qc = qk_ref[0, 0, :, u:u + 1] * scale
        kc = qk_ref[0, 0, :, sb + u:sb + u + 1]
        sn = a[:, u:u + 1] * s0_ref[u, 0] + kc * v_ref[u:u + 1, :]
        s1_ref[u, 0] = sn
        outs.append(jnp.sum(qc * sn, axis=0, keepdims=True))
    o = jnp.concatenate(outs, axis=0)
    go = go_ref[...]
    y_ref[...] = _rms(o, gn_ref[...]) * (go * jax.nn.sigmoid(go)) * jax.nn.sigmoid(gg_ref[...])


def _gla_decode(qk_cols, lr_t, wa_t, b_col, gv, g_out, gate_gla, g_gla, state):
    ns, gh, dk, dv = state.shape
    sb = qk_cols.shape[-1] // 2
    return pl.pallas_call(
        functools.partial(_gla_decode_kernel, sb=sb, scale=dk ** -0.5),
        grid=(ns // sb, gh),
        in_specs=[
            pl.BlockSpec((1, 1, dk, 2 * sb), lambda i, h: (i, h, 0, 0)),
            pl.BlockSpec((1, LANES, sb), lambda i, h: (i, 0, 0)),
            pl.BlockSpec((1, dk, LANES), lambda i, h: (h, 0, 0)),
            pl.BlockSpec((1, dk, 1), lambda i, h: (h, 0, 0)),
            pl.BlockSpec((sb, dv), lambda i, h: (i, h)),
            pl.BlockSpec((sb, dv), lambda i, h: (i, h)),
            pl.BlockSpec((sb, dv), lambda i, h: (i, h)),
            pl.BlockSpec((1, dv), lambda i, h: (0, 0)),
            pl.BlockSpec((sb, 1, dk, dv), lambda i, h: (i, h, 0, 0)),
        ],
        out_specs=[pl.BlockSpec((sb, dv), lambda i, h: (i, h)), pl.BlockSpec((sb, 1, dk, dv), lambda i, h: (i, h, 0, 0))],
        out_shape=[jax.ShapeDtypeStruct((ns, gh * dv), F32), jax.ShapeDtypeStruct((ns, gh, dk, dv), F32)],
        compiler_params=_cparams(("parallel", "parallel")),
        name="gla_decode",
    )(qk_cols, lr_t, wa_t, b_col, gv, g_out, gate_gla, g_gla.reshape(1, dv), state)


def _outproj_kernel(ym_ref, yg_ref, h_ref, w_ref, g_ref, wr_ref, br_ref, h1_ref, m_ref, route_ref):
    mix = (ym_ref[...].astype(F32) + yg_ref[...].astype(F32)).astype(BF16)
    h1 = h_ref[...] + jnp.dot(mix, w_ref[...], preferred_element_type=F32)
    h1_ref[...] = h1
    m = _rms(h1, g_ref[...])
    half = m.shape[1] // 2
    m_ref[...] = _pack_bf16_pair(m[:, :half], m[:, half:])
    logits = jnp.dot(m.astype(BF16), wr_ref[...].astype(BF16), preferred_element_type=F32) + br_ref[...]
    lane = lax.broadcasted_iota(I32, logits.shape, 1).astype(F32)
    vals, idxs = [], []
    for _ in range(TOP_K):
        mx = jnp.max(logits, axis=1, keepdims=True)
        ix = jnp.min(jnp.where(logits == mx, lane, float(LANES)), axis=1, keepdims=True)
        vals.append(mx)
        idxs.append(ix)
        logits = jnp.where(lane == ix, -jnp.inf, logits)
    ex = [jnp.exp(v - vals[0]) for v in vals]
    den = ex[0]
    for e in ex[1:]:
        den = den + e
    route = jnp.zeros(logits.shape, F32)
    for kk in range(TOP_K):
        route = jnp.where(lane == float(kk), idxs[kk], route)
        route = jnp.where(lane == float(TOP_K + kk), ex[kk] / den, route)
    route_ref[...] = route


def _outproj(y_mla, y_gla, h, w_out_b, g_ffn, w_router_p, b_router_p):
    n, d = h.shape
    tm = _tile(n, 512, 16)
    row = lambda i: (i, 0)
    fixed = lambda i: (0, 0)
    return pl.pallas_call(
        _outproj_kernel,
        grid=(n // tm,),
        in_specs=[
            pl.BlockSpec((tm, d), row), pl.BlockSpec((tm, d), row), pl.BlockSpec((tm, d), row),
            pl.BlockSpec((d, d), fixed, pipeline_mode=pl.Buffered(1)), pl.BlockSpec((1, d), fixed),
            pl.BlockSpec((d, LANES), fixed), pl.BlockSpec((1, LANES), fixed),
        ],
        out_specs=[pl.BlockSpec((tm, d), row), pl.BlockSpec((tm, d // 2), row), pl.BlockSpec((tm, LANES), row)],
        out_shape=[jax.ShapeDtypeStruct((n, d), F32), jax.ShapeDtypeStruct((n, d // 2), U32), jax.ShapeDtypeStruct((n, LANES), F32)],
        compiler_params=_cparams(("parallel",)),
        name="outproj",
    )(y_mla, y_gla, h, w_out_b, g_ffn.reshape(1, d), w_router_p, b_router_p)


def _dispatch_kernel(idx_ref, x_ref, o_ref, g_scr, *, rows):
    base = pl.program_id(0) * rows

    def body(r, carry):
        tok = idx_ref[base + r]
        g_scr[pl.ds(r, 1), :] = x_ref[pl.ds(tok, 1), :]
        return carry

    lax.fori_loop(0, rows, body, 0, unroll=8)
    hi, lo = _unpack_bf16_pair(g_scr[...])
    o_ref[...] = jnp.concatenate([hi, lo], axis=1).astype(BF16)


def _dispatch(buf_tok, m_packed):
    rows_total = buf_tok.shape[0]
    n, w = m_packed.shape
    rows = _tile(rows_total, 512, 16)
    grid_spec = pltpu.PrefetchScalarGridSpec(
        num_scalar_prefetch=1,
        grid=(rows_total // rows,),
        in_specs=[pl.BlockSpec(memory_space=pltpu.VMEM)],
        out_specs=pl.BlockSpec((rows, 2 * w), lambda i, idx: (i, 0)),
        scratch_shapes=[pltpu.VMEM((rows, w), U32)],
    )
    return pl.pallas_call(
        functools.partial(_dispatch_kernel, rows=rows),
        grid_spec=grid_spec,
        out_shape=jax.ShapeDtypeStruct((rows_total, 2 * w), BF16),
        compiler_params=_cparams(("arbitrary",), 56),
        name="moe_dispatch",
    )(buf_tok, m_packed)


def _for_row_chunks(b0, b1, rb, max_blocks, fn):
    n = b1 - b0
    size = 1 << (max_blocks.bit_length() - 1)
    while size >= 1:
        start = b0 + (n & ~(2 * size - 1))

        @pl.when((n & size) != 0)
        def _(start=start, size=size):
            fn(pl.multiple_of(start * rb, rb), size * rb)

        size //= 2


def _moe_kernel(ve_ref, blk_ref, nb_ref, zf_ref, xs_ref, wg_ref, wu_ref, bg_ref, bu_ref, wd_ref, bd_ref, ys_hbm,
                y_scr, o_scr, sem, *, n_ff, n_slab, dn, rb):
    v = pl.program_id(0)
    j = pl.program_id(1)
    blk0 = blk_ref[v]
    nb = nb_ref[v]
    b0, b1 = 0, nb
    fill = zf_ref[v] == 1
    work = jnp.logical_not(fill)
    max_blocks = xs_ref.shape[0] // rb
    d = y_scr.shape[1]
    slab = d // n_slab

    def out_copy(b):
        return pltpu.make_async_copy(o_scr.at[pl.ds(pl.multiple_of(b * rb, rb), rb)], ys_hbm.at[blk0 + b], sem.at[0])

    def flush():
        lax.fori_loop(0, nb, lambda b, c: (out_copy(b).start(), c)[1], 0)
        lax.fori_loop(0, nb, lambda b, c: (out_copy(b).wait(), c)[1], 0)

    @pl.when(fill & (j == 0))
    def _():
        def zero(b, carry):
            o_scr[pl.ds(pl.multiple_of(b * rb, rb), rb), :] = jnp.zeros((rb, o_scr.shape[1]), U32)
            return carry
        lax.fori_loop(0, nb, zero, 0)
        flush()

    @pl.when(work & (j == 0))
    def _():
        def init(r0, rows):
            y_scr[pl.ds(r0, rows), :] = jnp.broadcast_to(bd_ref[0], (rows, d))
        _for_row_chunks(b0, b1, rb, max_blocks, init)

    @pl.when(work)
    def _():
        def mlp(r0, rows):
            x = xs_ref[pl.ds(r0, rows), :]
            g = jnp.dot(x, wg_ref[0].astype(BF16), preferred_element_type=F32) + bg_ref[0]
            u = jnp.dot(x, wu_ref[0].astype(BF16), preferred_element_type=F32) + bu_ref[0]
            gate = jnp.minimum(g, SWIGLU_LIMIT)
            up = jnp.clip(u, -SWIGLU_LIMIT, SWIGLU_LIMIT)
            glu = gate * jax.nn.sigmoid(gate * SWIGLU_ALPHA)
            a = ((up + 1.0) * glu).astype(BF16)
            for c0 in range(0, d, dn):
                w = wd_ref[0, :, c0:c0 + dn].astype(BF16)
                y_scr[pl.ds(r0, rows), c0:c0 + dn] += jnp.dot(a, w, preferred_element_type=F32)
        _for_row_chunks(b0, b1, rb, max_blocks, mlp)

    @pl.when(work & (j == n_ff - 1))
    def _():
        def emit(r0, rows):
            for s in range(n_slab):
                y = y_scr[pl.ds(r0, rows), s * slab:(s + 1) * slab]
                o_scr[pl.ds(r0, rows), s * (slab // 2):(s + 1) * (slab // 2)] = _pack_bf16_pair(y[:, :slab // 2], y[:, slab // 2:])
        _for_row_chunks(b0, b1, rb, max_blocks, emit)
        flush()


def _moe_experts(meta, xs, w_gu, b_gu, w_down, b_down, rows_total, window_rows):
    v_exp, v_blk, v_nb, v_zero = meta
    d = xs.shape[1]
    e, _, f2 = w_gu.shape
    rb = EXPERT_ROW_BLOCK
    ff = f2 // 2
    tf = _tile(ff, 256, LANES)
    n_ff = ff // tf
    n_slab = d // _tile(d, MOE_SLAB, 2 * LANES)
    n_visits = v_exp.shape[0]
    ff_idx = lambda j, zf: jnp.where(zf == 1, n_ff - 1, j)
    grid_spec = pltpu.PrefetchScalarGridSpec(
        num_scalar_prefetch=4,
        grid=(n_visits, n_ff),
        in_specs=[
            pl.BlockSpec((pl.Element(window_rows), pl.Element(d)), lambda v, j, ve, blk, nb, zf: (blk[v] * rb, 0)),
            pl.BlockSpec((1, d, tf), lambda v, j, ve, blk, nb, zf: (ve[v], 0, ff_idx(j, zf[v]))),
            pl.BlockSpec((1, d, tf), lambda v, j, ve, blk, nb, zf: (ve[v], 0, n_ff + ff_idx(j, zf[v]))),
            pl.BlockSpec((1, 1, tf), lambda v, j, ve, blk, nb, zf: (ve[v], 0, ff_idx(j, zf[v]))),
            pl.BlockSpec((1, 1, tf), lambda v, j, ve, blk, nb, zf: (ve[v], 0, n_ff + ff_idx(j, zf[v]))),
            pl.BlockSpec((1, tf, d), lambda v, j, ve, blk, nb, zf: (ve[v], ff_idx(j, zf[v]), 0)),
            pl.BlockSpec((1, 1, d), lambda v, j, ve, blk, nb, zf: (ve[v], 0, 0)),
        ],
        out_specs=pl.BlockSpec(memory_space=pl.ANY),
        scratch_shapes=[pltpu.VMEM((window_rows, d), F32), pltpu.VMEM((window_rows, d // 2), U32), pltpu.SemaphoreType.DMA((1,))],
    )
    return pl.pallas_call(
        functools.partial(_moe_kernel, n_ff=n_ff, n_slab=n_slab, dn=_tile(d, 512, LANES), rb=rb),
        grid_spec=grid_spec,
        out_shape=jax.ShapeDtypeStruct((rows_total // rb, rb, d // 2), U32),
        compiler_params=_cparams(("arbitrary", "arbitrary"), 56),
        name="moe_experts",
    )(v_exp, v_blk, v_nb, v_zero, xs, w_gu, w_gu, b_gu.reshape(e, 1, f2), b_gu.reshape(e, 1, f2), w_down, b_down.reshape(e, 1, d))


def _combine_kernel(dest_ref, ys_ref, route_ref, o_ref, stage, *, tm):
    base = pl.program_id(1) * tm

    def body(t, carry):
        for kk in range(TOP_K):
            r = dest_ref[(base + t) * TOP_K + kk]
            stage[kk, pl.ds(t, 1), :] = ys_ref[pl.ds(r, 1), :]
        return carry

    lax.fori_loop(0, tm, body, 0, unroll=4)
    acc_hi = acc_lo = None
    for kk in range(TOP_K):
        g = route_ref[:, TOP_K + kk:TOP_K + kk + 1]
        hi, lo = _unpack_bf16_pair(stage[kk])
        acc_hi = g * hi if acc_hi is None else acc_hi + g * hi
        acc_lo = g * lo if acc_lo is None else acc_lo + g * lo
    o_ref[...] = jnp.concatenate([acc_hi, acc_lo], axis=1)


def _combine(dest, ys, route):
    rows_total, packed = ys.shape
    wh = MOE_SLAB // 2
    n_down = packed // wh
    n = route.shape[0]
    tm = _tile(n, 256, 8)
    grid_spec = pltpu.PrefetchScalarGridSpec(
        num_scalar_prefetch=1,
        grid=(n_down, n // tm),
        in_specs=[
            pl.BlockSpec((rows_total, wh), lambda c, i, dest: (0, c)),
            pl.BlockSpec((tm, LANES), lambda c, i, dest: (i, 0)),
        ],
        out_specs=pl.BlockSpec((tm, 2 * wh), lambda c, i, dest: (i, c)),
        scratch_shapes=[pltpu.VMEM((TOP_K, tm, wh), U32)],
    )
    return pl.pallas_call(
        functools.partial(_combine_kernel, tm=tm),
        grid_spec=grid_spec,
        out_shape=jax.ShapeDtypeStruct((n, n_down * 2 * wh), F32),
        compiler_params=_cparams(("arbitrary", "arbitrary"), 56),
        name="moe_combine",
    )(dest, ys, route)


def _dest_kernel(route_ref, starts_ref, o_ref, cnt_scr):
    @pl.when(pl.program_id(0) == 0)
    def _():
        cnt_scr[...] = jnp.zeros(cnt_scr.shape, F32)

    tb = route_ref.shape[0]
    lane = lax.broadcasted_iota(I32, (tb, LANES), 1).astype(F32)
    row = lax.broadcasted_iota(I32, (tb, tb), 0)
    col = lax.broadcasted_iota(I32, (tb, tb), 1)
    before = jnp.where(col < row, 1.0, 0.0).astype(BF16)
    base = starts_ref[...] + cnt_scr[...]
    out = jnp.zeros((tb, LANES), F32)
    for kk in range(TOP_K):
        hit = lane == route_ref[:, kk:kk + 1]
        seen = jnp.dot(before, jnp.where(hit, 1.0, 0.0).astype(BF16), preferred_element_type=F32)
        dest = jnp.sum(jnp.where(hit, seen + base, 0.0), axis=1, keepdims=True)
        out = jnp.where(lane == float(kk), dest, out)
        base = base + jnp.sum(jnp.where(hit, 1.0, 0.0), axis=0, keepdims=True)
    cnt_scr[...] = base - starts_ref[...]
    o_ref[...] = out


def _dest_rows(route, starts_row):
    n = route.shape[0]
    tb = _tile(n, 512, 8)
    return pl.pallas_call(
        _dest_kernel,
        grid=(n // tb,),
        in_specs=[pl.BlockSpec((tb, LANES), lambda i: (i, 0)), pl.BlockSpec((1, LANES), lambda i: (0, 0))],
        out_specs=pl.BlockSpec((tb, LANES), lambda i: (i, 0)),
        out_shape=jax.ShapeDtypeStruct((n, LANES), F32),
        scratch_shapes=[pltpu.VMEM((1, LANES), F32)],
        compiler_params=_cparams(("arbitrary",)),
        name="moe_dest",
    )(route, starts_row)


def _route_meta(route, n_experts, window_rows):
    n_tok = route.shape[0]
    n_asg = n_tok * TOP_K
    rb = EXPERT_ROW_BLOCK
    e_flat = route[:, :TOP_K].astype(I32).reshape(n_asg)
    tok_flat = (jnp.arange(n_asg, dtype=I32) // TOP_K).astype(I32)
    counts = jnp.sum(jax.nn.one_hot(e_flat, n_experts, dtype=I32), axis=0)
    padded = (counts + rb - 1) // rb * rb
    ends = jnp.cumsum(padded)
    starts = ends - padded
    starts_row = jnp.pad(starts.astype(F32), (0, LANES - n_experts)).reshape(1, LANES)
    dest = _dest_rows(route, starts_row)[:, :TOP_K].astype(I32).reshape(n_asg)
    rows_total = (-(-n_asg // rb) + n_experts) * rb
    rows_alloc = -(-(rows_total + window_rows) // 512) * 512
    buf_tok = jnp.zeros((rows_alloc,), I32).at[dest].set(tok_flat, unique_indices=True, mode="promise_in_bounds")
    mb = window_rows // rb
    n_blk = (padded // rb).astype(I32)
    vpe = -(-n_blk // mb)
    cv = jnp.cumsum(vpe).astype(I32)
    n_work = cv[-1]
    total_blk = (ends[-1] // rb).astype(I32)
    tail_blk = rows_total // rb - total_blk
    n_fill = -(-tail_blk // mb)
    n_visits = n_experts + rows_total // rb // mb + 1 + -(-n_experts // mb)
    pos = jnp.arange(n_visits, dtype=I32)
    is_work = pos < n_work
    e = jnp.minimum(jnp.searchsorted(cv, pos, side="right"), n_experts - 1).astype(I32)
    k = pos - (cv[e] - vpe[e])
    f = pos - n_work
    is_fill = (f >= 0) & (f < n_fill)
    blk = jnp.where(is_work, (starts[e] // rb).astype(I32) + k * mb, total_blk + jnp.clip(f, 0, n_fill) * mb)
    blk = jnp.minimum(blk, rows_total // rb)
    nb = jnp.where(is_work, jnp.clip(n_blk[e] - k * mb, 0, mb), jnp.where(is_fill, jnp.clip(tail_blk - f * mb, 0, mb), 0))
    v_exp = jnp.where(is_work, e, e[jnp.maximum(n_work - 1, 0)])
    v_zero = jnp.logical_not(is_work).astype(I32)
    return buf_tok, dest, (v_exp.astype(I32), blk.astype(I32), nb.astype(I32), v_zero), rows_total


def _ple_kernel(h1_ref, moe_ref, p_ref, wp_ref, wg_ref, gp_ref, gf_ref, y_ref, *, final):
    h2 = h1_ref[...] + moe_ref[...]
    u = _rms(h2, gp_ref[...]).astype(BF16)
    gate = jax.nn.sigmoid(jnp.dot(u, wg_ref[...], preferred_element_type=F32))
    pe = jnp.dot(p_ref[...].astype(BF16), wp_ref[...], preferred_element_type=F32)
    y = h2 + pe * gate
    y_ref[...] = _rms(y, gf_ref[...]) if final else y


def _ple_final(h1, moe, row_off, p, w_ple_b, w_ple_gate_b, g_ple, g_final, final):
    n, d = h1.shape
    pd = p.shape[1]
    tm = _tile(n, 512, 8)
    assert row_off % tm == 0
    row = lambda i: (i, 0)
    fixed = lambda i: (0, 0)
    return pl.pallas_call(
        functools.partial(_ple_kernel, final=final),
        grid=(n // tm,),
        in_specs=[
            pl.BlockSpec((tm, d), row), pl.BlockSpec((tm, d), lambda i: (row_off // tm + i, 0)), pl.BlockSpec((tm, pd), row),
            pl.BlockSpec((pd, d), fixed), pl.BlockSpec((d, d), fixed, pipeline_mode=pl.Buffered(1)),
            pl.BlockSpec((1, d), fixed), pl.BlockSpec((1, d), fixed),
        ],
        out_specs=pl.BlockSpec((tm, d), row),
        out_shape=jax.ShapeDtypeStruct((n, d), F32),
        compiler_params=_cparams(("parallel",)),
        name="ple_final",
    )(h1, moe, p, w_ple_b, w_ple_gate_b, g_ple.reshape(1, d), g_final.reshape(1, d))


def _rope_table(pos, rope):
    half = rope // 2
    inv_freq = jnp.power(ROPE_THETA, -jnp.arange(half, dtype=F32) / half)
    ang = pos.astype(F32)[:, None] * inv_freq[None, :]
    cos, sin = jnp.cos(ang), jnp.sin(ang)
    return jnp.concatenate([cos, cos, -sin, sin], axis=1)


def _swap_halves(w):
    half = w.shape[-1] // 2
    return jnp.concatenate([w[..., half:], w[..., :half]], axis=-1)


def kernel(x_prompt, x_sample, cache_kv_latent, cache_k_rope, state_gla, page_table, p_prompt, p_sample, g_mix, w_in, g_q_a, w_uq, g_kv_a, w_uk, w_uv, w_a2, b_a2, g_gla, w_out, g_ffn, w_router, b_router, w_gu, b_gu, w_down, b_down, g_ple, w_ple, w_ple_gate, g_final):
    depth = w_in.shape[0]
    batch, seq, d = x_prompt.shape
    ns, dec_seq, _ = x_sample.shape
    assert dec_seq == 1
    ql, kvl = g_q_a.shape[-1], g_kv_a.shape[-1]
    rope = cache_k_rope.shape[-1]
    heads, nope = w_uk.shape[2], w_uk.shape[3]
    vh = w_uv.shape[3]
    _, _, gh, dk, dv = state_gla.shape
    lr = w_a2.shape[1]
    n_experts = w_router.shape[-1]
    past = page_table.shape[1] * cache_kv_latent.shape[2]
    assert 2 * rope == LANES and nope == LANES and heads * vh == d and lr <= LANES and n_experts <= LANES
    scale = LOG2E / math.sqrt(nope + rope)
    n_p = batch * seq
    big_segs = (("gv", gh * dv), ("g_out", gh * dv), ("gate_mla", d), ("gate_gla", d), ("gq", gh * dk), ("gk", gh * dk))
    offs, o = {}, 0
    for key, w in big_segs:
        offs[key] = o
        o += w
    offs["g_lr"] = ql + kvl + 2 * rope
    in_splits = (ql, kvl, rope, gh * dk, gh * dk, gh * dv, gh * dv, lr, d, d)
    in_offsets = tuple(sum(in_splits[:i + 1]) for i in range(len(in_splits) - 1))

    tab_p = jnp.tile(_rope_table(jnp.arange(seq), rope), (batch, 1))
    tab_s = _rope_table(jnp.full((ns,), past, I32), rope)

    h_p = x_prompt.reshape(n_p, d)
    h_s = x_sample.reshape(ns, d)
    kv_p, kr_p, st_p, kv_s, kr_s, st_s = [], [], [], [], [], []
    for l in range(depth):
        seg = dict(zip(("c_q", "c_kv", "k_r", "gq", "gk", "gv", "g_out", "g_lr", "gate_mla", "gate_gla"), jnp.split(w_in[l], in_offsets, axis=1)))
        w_big = jnp.concatenate([seg[k] for k, _ in big_segs], axis=1).astype(BF16)
        w_small = jnp.concatenate([seg["c_q"], seg["c_kv"], seg["k_r"], _swap_halves(seg["k_r"]),
                                   jnp.pad(seg["g_lr"], ((0, 0), (0, LANES - lr)))], axis=1).astype(BF16)
        wq = w_uq[l].reshape(ql, heads, nope + rope)
        w_uq_p = jnp.concatenate([wq[..., :nope], wq[..., nope:], _swap_halves(wq[..., nope:])], axis=-1).transpose(1, 0, 2).astype(BF16)
        w_uk_h = w_uk[l].transpose(1, 0, 2).astype(BF16)
        w_uv_h = w_uv[l].transpose(1, 0, 2).astype(BF16)
        w_uk_f = w_uk[l].reshape(kvl, heads * nope).astype(BF16)
        w_uv_tf = w_uv[l].reshape(kvl, heads * vh).T.astype(BF16)
        w_a2p = jnp.pad(w_a2[l], ((0, LANES - lr), (0, 0)))
        w_out_b = w_out[l].astype(BF16)
        w_router_p = jnp.pad(w_router[l], ((0, 0), (0, LANES - n_experts)))
        b_router_p = jnp.pad(b_router[l], (0, LANES - n_experts), constant_values=NEG_BIG).reshape(1, LANES)
        w_ple_b = w_ple[l].astype(BF16)
        w_ple_gate_b = w_ple_gate[l].astype(BF16)

        def front(h, tab, z_dtype, q_scale):
            a = _rms_cast(h, g_mix[l])
            zb = _matmul(a, w_big, z_dtype, "inproj_big")
            zs = _matmul(a, w_small, F32, "inproj_small")
            q = _q_proj(zs, g_q_a[l], w_uq_p, tab, q_scale)
            return zb, zs, q

        zb, zs, q = front(h_p, tab_p, BF16, scale)
        c_kv, k_r, k_heads, v_t = _kv_prep(zs, g_kv_a[l], tab_p, w_uk_f, w_uv_tf, heads, ql, rope, True)
        y_mla = _flash_prompt(q, k_heads, v_t, zb, offs["gate_mla"], batch, seq)
        y_gla, s_new = _gla_prompt(zb, zs, offs, w_a2p, b_a2[l], g_gla[l], batch, seq, gh, dk, dv)
        h1_p, m_p, route_p = _outproj(y_mla, y_gla, h_p, w_out_b, g_ffn[l], w_router_p, b_router_p)
        kv_p.append(c_kv.reshape(batch, seq, kvl)); kr_p.append(k_r.reshape(batch, seq, rope)); st_p.append(s_new)

        zb, zs, q = front(h_s, tab_s, F32, 1.0)
        c_kv, k_r = _kv_prep(zs, g_kv_a[l], tab_s, w_uk_f, w_uv_tf, heads, ql, rope, False)
        q_lat = _absorb_q(q, w_uk_h).transpose(1, 0, 2)
        q_rope = q[:, :, nope:nope + rope].astype(F32).transpose(1, 0, 2)
        o_lat = _decode_attn(page_table, q_lat, q_rope, c_kv, k_r, cache_kv_latent[l], cache_k_rope[l].transpose(0, 2, 1), scale)
        y_mla = _unabsorb_o(o_lat.transpose(1, 0, 2), w_uv_h, zb, offs["gate_mla"])
        sb = 8
        zf = zb.astype(F32)
        cols = lambda key, w: zf[:, offs[key]:offs[key] + w]
        to_cols = lambda z: z.reshape(ns // sb, sb, gh, dk).transpose(0, 2, 3, 1)
        qk_cols = jnp.concatenate([to_cols(cols("gq", gh * dk)), to_cols(cols("gk", gh * dk))], axis=-1)
        lr_t = zs[:, offs["g_lr"]:offs["g_lr"] + LANES].reshape(ns // sb, sb, LANES).transpose(0, 2, 1)
        wa_t = w_a2p.reshape(LANES, gh, dk).transpose(1, 2, 0)
        b_col = b_a2[l].reshape(gh, dk, 1)
        y_gla, s_new = _gla_decode(qk_cols, lr_t, wa_t, b_col, cols("gv", gh * dv), cols("g_out", gh * dv),
                                   cols("gate_gla", gh * dv), g_gla[l], state_gla[l])
        h1_s, m_s, route_s = _outproj(y_mla, y_gla, h_s, w_out_b, g_ffn[l], w_router_p, b_router_p)
        kv_s.append(c_kv.reshape(ns, 1, kvl)); kr_s.append(k_r.reshape(ns, 1, rope)); st_s.append(s_new)

        m_all = jnp.concatenate([m_p, m_s], axis=0)
        route = jnp.concatenate([route_p, route_s], axis=0)
        buf_tok, dest, meta, rows_total = _route_meta(route, n_experts, MOE_WINDOW_ROWS)
        xs = _dispatch(buf_tok, m_all)
        ys = _moe_experts(meta, xs, w_gu[l], b_gu[l], w_down[l], b_down[l], rows_total, MOE_WINDOW_ROWS)
        ys = ys.reshape(rows_total, d // 2)
        moe = _combine(dest, ys, route)

        last = l == depth - 1
        h_p = _ple_final(h1_p, moe, 0, p_prompt[l].reshape(n_p, -1), w_ple_b, w_ple_gate_b, g_ple[l], g_final, last)
        h_s = _ple_final(h1_s, moe, n_p, p_sample[l].reshape(ns, -1), w_ple_b, w_ple_gate_b, g_ple[l], g_final, last)

    return (h_p.reshape(batch, seq, d), h_s.reshape(ns, 1, d), jnp.stack(kv_p), jnp.stack(kr_p), jnp.stack(st_p),
            jnp.stack(kv_s), jnp.stack(kr_s), jnp.stack(st_s))
```
